```python
import jax, jax.numpy as jnp
from jax import lax
import numpy as np

D_MODEL = 4096
BATCH = 2
SEQ = 8192
DEPTH = 2

N_MIXERS = 2
HEAD_DIM = 128
ROPE_THETA = 10000.0
NORM_EPS = 1e-6
DILATED_GROUPS = ((128, 1), (512, 4), (2048, 16))
N_GROUPS = 3
A_HEADS_PER_GROUP = 16
A_BLOCK = 128
A_IN_WIDTH = N_GROUPS * 3 * A_HEADS_PER_GROUP * HEAD_DIM
A_OUT_WIDTH = A_HEADS_PER_GROUP * HEAD_DIM
B_HEADS = 32
B_KV_HEADS = 8
IDX_HEADS = 32
IDX_DIM = 128
IDX_Q_RANK = 1024
TOPK_MAX = 256
B_QBLOCK = 128
B_Q_WIDTH = B_HEADS * HEAD_DIM
B_KV_WIDTH = B_KV_HEADS * HEAD_DIM
B_IN_WIDTH = B_Q_WIDTH + 2 * B_KV_WIDTH + IDX_Q_RANK + IDX_DIM + IDX_HEADS
D_FF = 4 * D_MODEL
N_A_LAYERS = (DEPTH + 1) // 2
N_B_LAYERS = DEPTH // 2

kernel_name = "hybrid_dilated_dsa_sqrelu_trunk"


def rms_norm(x, g):
    xf = x.astype(jnp.float32)
    y = xf * lax.rsqrt(jnp.mean(xf * xf, axis=-1, keepdims=True) + NORM_EPS)
    return (y * g.astype(jnp.float32)).astype(x.dtype)


def layer_norm(x, g, b):
    xf = x.astype(jnp.float32)
    mu = jnp.mean(xf, axis=-1, keepdims=True)
    xc = xf - mu
    y = xc * lax.rsqrt(jnp.mean(xc * xc, axis=-1, keepdims=True) + NORM_EPS)
    return (y * g.astype(jnp.float32) + b.astype(jnp.float32)).astype(x.dtype)


def rope_tables(positions, dim):
    inv_freq = ROPE_THETA ** (-jnp.arange(0, dim, 2, dtype=jnp.float32) / dim)
    ang = positions.astype(jnp.float32)[..., None] * inv_freq
    return jnp.cos(ang), jnp.sin(ang)


def apply_rope(x, cos, sin):
    extra = (1,) * (x.ndim - 3)
    c = cos.reshape(cos.shape[:2] + extra + cos.shape[2:])
    s = sin.reshape(sin.shape[:2] + extra + sin.shape[2:])
    x1, x2 = jnp.split(x.astype(jnp.float32), 2, axis=-1)
    return jnp.concatenate([x1 * c - x2 * s, x2 * c + x1 * s], axis=-1).astype(x.dtype)


def dilated_window_attention(q, k, v, window, dilation):
    B, S, H, hd = q.shape
    r = dilation
    w = window // dilation
    n = S // r
    nb = -(-n // A_BLOCK)
    n_pad = nb * A_BLOCK

    def to_sub(t):
        t = t.reshape(B, n, r, H, hd).transpose(0, 2, 1, 3, 4)
        t = jnp.pad(t, ((0, 0), (0, 0), (0, n_pad - n), (0, 0), (0, 0)))
        return t.reshape(B, r, nb, A_BLOCK, H, hd)

    def with_prev(t):
        prev = jnp.pad(t, ((0, 0), (0, 0), (1, 0), (0, 0), (0, 0), (0, 0)))[:, :, :-1]
        return jnp.concatenate([prev, t], axis=3)

    qs, ks, vs = to_sub(q), to_sub(k), to_sub(v)
    kb, vb = with_prev(ks), with_prev(vs)
    logits = jnp.einsum('bcnqhd,bcnkhd->bcnhqk', qs.astype(jnp.float32),
                        kb.astype(jnp.float32)) * (hd ** -0.5)
    qi = jnp.arange(A_BLOCK)[:, None] + A_BLOCK
    kj = jnp.arange(2 * A_BLOCK)[None, :]
    dist = qi - kj
    band = (dist >= 0) & (dist <= w)
    has_prev = (jnp.arange(nb)[:, None, None] > 0) | (kj >= A_BLOCK)[None]
    mask = band[None] & has_prev
    logits = jnp.where(mask[None, None, :, None], logits, -jnp.inf)
    m = jnp.max(logits, axis=-1, keepdims=True)
    p = jnp.exp(logits - m)
    l = jnp.sum(p, axis=-1, keepdims=True)
    o = jnp.einsum('bcnhqk,bcnkhd->bcnqhd', p / l, vb.astype(jnp.float32))
    lse = (m + jnp.log(l))[..., 0]
    o = o.reshape(B, r, n_pad, H, hd)[:, :, :n].transpose(0, 2, 1, 3, 4).reshape(B, S, H, hd)
    lse = lse.transpose(0, 1, 2, 4, 3).reshape(B, r, n_pad, H)[:, :, :n]
    lse = lse.transpose(0, 2, 1, 3).reshape(B, S, H)
    return o, lse


def mixer_dilated(h, cos, sin, w_in, w_out):
    B, S, _ = h.shape
    qkv = (h @ w_in).reshape(B, S, N_GROUPS, 3, A_HEADS_PER_GROUP, HEAD_DIM)
    outs, lses = [], []
    for g, (window, dil) in enumerate(DILATED_GROUPS):
        q = apply_rope(qkv[:, :, g, 0], cos, sin)
        k = apply_rope(qkv[:, :, g, 1], cos, sin)
        o, lse = dilated_window_attention(q, k, qkv[:, :, g, 2], window, dil)
        outs.append(o)
        lses.append(lse)
    alpha = jax.nn.softmax(jnp.stack(lses, axis=0), axis=0)
    o = jnp.sum(alpha[..., None] * jnp.stack(outs, axis=0), axis=0)
    return o.reshape(B, S, A_OUT_WIDTH).astype(h.dtype) @ w_out


def mixer_sparse(h, cos, sin, w_in, idx_q_norm, w_idx_up, idx_k_w, idx_k_b, w_out):
    B, S, _ = h.shape
    splits = [int(s) for s in np.cumsum([B_Q_WIDTH, B_KV_WIDTH, B_KV_WIDTH, IDX_Q_RANK, IDX_DIM])]
    q, k, v, iq_lat, ik, iw = jnp.split(h @ w_in, splits, axis=-1)
    q = apply_rope(q.reshape(B, S, B_HEADS, HEAD_DIM), cos, sin)
    k = apply_rope(k.reshape(B, S, B_KV_HEADS, HEAD_DIM), cos, sin)
    v = v.reshape(B, S, B_KV_HEADS, HEAD_DIM)
    iq = (rms_norm(iq_lat, idx_q_norm) @ w_idx_up).reshape(B, S, IDX_HEADS, IDX_DIM)
    iq = apply_rope(iq, cos, sin)
    ik = apply_rope(layer_norm(ik, idx_k_w, idx_k_b), cos, sin).astype(jnp.float32)
    iw = iw.astype(jnp.float32) * (IDX_HEADS ** -0.5 * IDX_DIM ** -0.5)
    n_sel = min(TOPK_MAX, S // 4)
    nb = S // B_QBLOCK
    kpos = jnp.arange(S)
    bidx = jnp.arange(B)[:, None, None]

    def blocks(t):
        return t.reshape((B, nb, B_QBLOCK) + t.shape[2:]).swapaxes(0, 1)

    def one_block(args):
        q_b, iq_b, iw_b, start = args
        tpos = start + jnp.arange(B_QBLOCK)
        rel = jax.nn.relu(jnp.einsum('bqhd,bkd->bqhk', iq_b.astype(jnp.float32), ik))
        score = jnp.einsum('bqhk,bqh->bqk', rel, iw_b)
        causal = kpos[None, :] <= tpos[:, None]
        score = jnp.where(causal[None], score, -jnp.inf)
        _, sel = lax.top_k(score, n_sel)
        valid = sel <= tpos[None, :, None]
        k_sel = k[bidx, sel].astype(jnp.float32)
        v_sel = v[bidx, sel].astype(jnp.float32)
        qg = q_b.reshape(B, B_QBLOCK, B_KV_HEADS, B_HEADS // B_KV_HEADS, HEAD_DIM).astype(jnp.float32)
        logits = jnp.einsum('bqgrd,bqkgd->bqgrk', qg, k_sel) * (HEAD_DIM ** -0.5)
        logits = jnp.where(valid[:, :, None, None, :], logits, -jnp.inf)
        p = jax.nn.softmax(logits, axis=-1)
        o = jnp.einsum('bqgrk,bqkgd->bqgrd', p, v_sel)
        return o.reshape(B, B_QBLOCK, B_Q_WIDTH).astype(h.dtype)

    starts = jnp.arange(nb) * B_QBLOCK
    o = lax.map(one_block, (blocks(q), blocks(iq), blocks(iw), starts))
    o = o.swapaxes(0, 1).reshape(B, S, B_Q_WIDTH)
    return o @ w_out


def setup_inputs(seed: int = 0) -> dict:
    key = jax.random.key(seed)
    ks = jax.random.split(key, 16)

    def nrm(k, shape, fan_in):
        return jax.random.normal(k, shape, jnp.float32) * (fan_in ** -0.5)

    def gain(k, shape):
        return 1.0 + 0.02 * jax.random.normal(k, shape, jnp.float32)

    x = jax.random.normal(ks[0], (BATCH, SEQ, D_MODEL), jnp.float32)
    positions = jnp.broadcast_to(jnp.arange(SEQ, dtype=jnp.int32)[None, :], (BATCH, SEQ))
    return {
        "x": x,
        "positions": positions,
        "attn_pre_norm": gain(ks[1], (DEPTH, D_MODEL)),
        "attn_post_norm": gain(ks[2], (DEPTH, D_MODEL)),
        "mlp_pre_norm": gain(ks[3], (DEPTH, D_MODEL)),
        "mlp_post_norm": gain(ks[4], (DEPTH, D_MODEL)),
        "w_in_a": nrm(ks[5], (N_A_LAYERS, D_MODEL, A_IN_WIDTH), D_MODEL),
        "w_out_a": nrm(ks[6], (N_A_LAYERS, A_OUT_WIDTH, D_MODEL), A_OUT_WIDTH),
        "w_in_b": nrm(ks[7], (N_B_LAYERS, D_MODEL, B_IN_WIDTH), D_MODEL),
        "idx_q_norm": gain(ks[8], (N_B_LAYERS, IDX_Q_RANK)),
        "w_idx_up": nrm(ks[9], (N_B_LAYERS, IDX_Q_RANK, IDX_HEADS * IDX_DIM), IDX_Q_RANK),
        "idx_k_norm_w": gain(ks[10], (N_B_LAYERS, IDX_DIM)),
        "idx_k_norm_b": 0.02 * jax.random.normal(ks[11], (N_B_LAYERS, IDX_DIM), jnp.float32),
        "w_out_b": nrm(ks[12], (N_B_LAYERS, B_Q_WIDTH, D_MODEL), B_Q_WIDTH),
        "w_up": nrm(ks[13], (DEPTH, D_MODEL, D_FF), D_MODEL),
        "w_down": nrm(ks[14], (DEPTH, D_FF, D_MODEL), D_FF),
    }


def reference(x, positions, attn_pre_norm, attn_post_norm, mlp_pre_norm, mlp_post_norm,
              w_in_a, w_out_a, w_in_b, idx_q_norm, w_idx_up, idx_k_norm_w, idx_k_norm_b,
              w_out_b, w_up, w_down):
    cos, sin = rope_tables(positions, HEAD_DIM)
    h = x
    for i in range(DEPTH):
        j = i // N_MIXERS
        a = rms_norm(h, attn_pre_norm[i])
        if i % N_MIXERS == 0:
            m = mixer_dilated(a, cos, sin, w_in_a[j], w_out_a[j])
        else:
            m = mixer_sparse(a, cos, sin, w_in_b[j], idx_q_norm[j], w_idx_up[j],
                             idx_k_norm_w[j], idx_k_norm_b[j], w_out_b[j])
        h = h + rms_norm(m, attn_post_norm[i])
        f = rms_norm(h, mlp_pre_norm[i])
        f = jnp.square(jax.nn.relu(f @ w_up[i])) @ w_down[i]
        h = h + rms_norm(f, mlp_post_norm[i])
    return h
```

```python
import functools

import jax
import jax.numpy as jnp
from jax import lax
from jax.experimental import pallas as pl
from jax.experimental.pallas import tpu as pltpu

HEAD_DIM = 128
ROPE_THETA = 10000.0
NORM_EPS = 1e-6
DILATED_GROUPS = ((128, 1), (512, 4), (2048, 16))
N_GROUPS = 3
A_HEADS_PER_GROUP = 16
A_BLOCK = 128
A_GROUP_WIDTH = A_HEADS_PER_GROUP * HEAD_DIM
B_HEADS = 32
B_KV_HEADS = 8
IDX_HEADS = 32
IDX_DIM = 128
IDX_Q_RANK = 1024
TOPK_MAX = 256
B_Q_WIDTH = B_HEADS * HEAD_DIM
B_KV_WIDTH = B_KV_HEADS * HEAD_DIM
B_QKV_WIDTH = B_Q_WIDTH + 2 * B_KV_WIDTH

LANES = 128
VMEM_LIMIT_BYTES = 56 * 2**20
MASK_VALUE = -1e30
INT_MIN = -2**31

F32 = jnp.float32
BF16 = jnp.bfloat16


def _tile(dim, pref):
    t = min(dim, pref)
    while dim % t:
        t //= 2
    return t


def _params(semantics):
    return pltpu.CompilerParams(dimension_semantics=semantics, vmem_limit_bytes=VMEM_LIMIT_BYTES)


def _rms(x, g):
    return x * lax.rsqrt(jnp.mean(x * x, axis=-1, keepdims=True) + NORM_EPS) * g


def _norm_kernel(x_ref, g_ref, a_ref):
    a_ref[...] = _rms(x_ref[...], g_ref[...]).astype(a_ref.dtype)


def rms_norm_cast(x, g):
    m, d = x.shape
    tm = _tile(m, 256)
    return pl.pallas_call(
        _norm_kernel,
        out_shape=jax.ShapeDtypeStruct((m, d), BF16),
        grid=(m // tm,),
        in_specs=[pl.BlockSpec((tm, d), lambda i: (i, 0)), pl.BlockSpec((1, d), lambda i: (0, 0))],
        out_specs=pl.BlockSpec((tm, d), lambda i: (i, 0)),
        compiler_params=_params(("parallel",)),
        name="rms_norm_cast",
    )(x, g.reshape(1, d))


def _residual_kernel(h_ref, m_ref, gpost_ref, *rest, with_next):
    h = h_ref[...] + _rms(m_ref[...].astype(F32), gpost_ref[...])
    if with_next:
        gpre_ref, h_out_ref, a_ref = rest
        a_ref[...] = _rms(h, gpre_ref[...]).astype(a_ref.dtype)
    else:
        (h_out_ref,) = rest
    h_out_ref[...] = h


def residual_norm(h, m, g_post, g_pre_next=None):
    rows, d = h.shape
    tm = _tile(rows, 256)
    with_next = g_pre_next is not None
    row_spec = pl.BlockSpec((tm, d), lambda i: (i, 0))
    g_spec = pl.BlockSpec((1, d), lambda i: (0, 0))
    args = [h, m, g_post.reshape(1, d)]
    in_specs = [row_spec, row_spec, g_spec]
    out_shape = [jax.ShapeDtypeStruct((rows, d), F32)]
    out_specs = [row_spec]
    if with_next:
        args.append(g_pre_next.reshape(1, d))
        in_specs.append(g_spec)
        out_shape.append(jax.ShapeDtypeStruct((rows, d), BF16))
        out_specs.append(row_spec)
    out = pl.pallas_call(
        functools.partial(_residual_kernel, with_next=with_next),
        out_shape=out_shape,
        grid=(rows // tm,),
        in_specs=in_specs,
        out_specs=out_specs,
        compiler_params=_params(("parallel",)),
        name="residual_norm",
    )(*args)
    return (out[0], out[1]) if with_next else (out[0], None)


def _rope(x, cos2, sin2):
    return x * cos2 + pltpu.roll(x, HEAD_DIM // 2, 1) * sin2


def _matmul_kernel(a_ref, w_ref, *rest, nk, act, rope_pred):
    if rope_pred is not None:
        cos_ref, sin_ref, *rest = rest
    o_ref, *scratch = rest

    def finish(acc):
        if act == "relu2":
            r = jnp.maximum(acc, 0.0)
            acc = r * r
        if rope_pred is None:
            o_ref[...] = acc.astype(o_ref.dtype)
            return
        do_rope = rope_pred(pl.program_id(1))

        @pl.when(do_rope)
        def _():
            cos2 = cos_ref[...]
            sin2 = sin_ref[...]
            for c in range(o_ref.shape[1] // HEAD_DIM):
                sl = slice(c * HEAD_DIM, (c + 1) * HEAD_DIM)
                o_ref[:, sl] = _rope(acc[:, sl], cos2, sin2).astype(o_ref.dtype)

        @pl.when(jnp.logical_not(do_rope))
        def _():
            o_ref[...] = acc.astype(o_ref.dtype)

    prod = jnp.dot(a_ref[...], w_ref[...], preferred_element_type=F32)
    if nk == 1:
        finish(prod)
        return
    (acc_ref,) = scratch
    k = pl.program_id(2)

    @pl.when(k == 0)
    def _():
        acc_ref[...] = prod

    @pl.when(jnp.logical_and(k > 0, k < nk - 1))
    def _():
        acc_ref[...] += prod

    @pl.when(k == nk - 1)
    def _():
        finish(acc_ref[...] + prod)


def matmul(a, w, *, out_dtype=BF16, act=None, rope=None, tm=1024, tn=1024, tk=4096):
    m, kdim = a.shape
    n = w.shape[1]
    tm, tn, tk = _tile(m, tm), _tile(n, tn), _tile(kdim, tk)
    nk = kdim // tk
    in_specs = [pl.BlockSpec((tm, tk), lambda i, j, k: (i, k)),
                pl.BlockSpec((tk, tn), lambda i, j, k: (k, j))]
    args = [a, w]
    rope_pred = None
    if rope is not None:
        cos2, sin2, rope_pred = rope
        tab_spec = pl.BlockSpec((tm, HEAD_DIM), lambda i, j, k: (i, 0))
        in_specs += [tab_spec, tab_spec]
        args += [cos2, sin2]
    scratch = [pltpu.VMEM((tm, tn), F32)] if nk > 1 else []
    return pl.pallas_call(
        functools.partial(_matmul_kernel, nk=nk, act=act, rope_pred=rope_pred),
        out_shape=jax.ShapeDtypeStruct((m, n), out_dtype),
        grid=(m // tm, n // tn, nk),
        in_specs=in_specs,
        out_specs=pl.BlockSpec((tm, tn), lambda i, j, k: (i, j)),
        scratch_shapes=scratch,
        compiler_params=_params(("parallel", "parallel", "arbitrary")),
        name="matmul",
    )(*args)


def _dilated_kernel(q_ref, kp_ref, kc_ref, vp_ref, vc_ref, o_ref, lse_ref, *, w):
    has_prev = pl.program_id(2) > 0
    qi = lax.broadcasted_iota(jnp.int32, (A_BLOCK, A_BLOCK), 0)
    kj = lax.broadcasted_iota(jnp.int32, (A_BLOCK, A_BLOCK), 1)
    mask_prev = jnp.logical_and(qi + A_BLOCK - kj <= w, has_prev)
    mask_cur = jnp.logical_and(kj <= qi, qi - kj <= w)
    scale = HEAD_DIM ** -0.5
    nt = (((1,), (1,)), ((), ()))
    for h in range(A_HEADS_PER_GROUP):
        sl = slice(h * HEAD_DIM, (h + 1) * HEAD_DIM)
        q = q_ref[:, sl]
        s_prev = lax.dot_general(q, kp_ref[:, sl], nt, preferred_element_type=F32) * scale
        s_cur = lax.dot_general(q, kc_ref[:, sl], nt, preferred_element_type=F32) * scale
        s_prev = jnp.where(mask_prev, s_prev, MASK_VALUE)
        s_cur = jnp.where(mask_cur, s_cur, MASK_VALUE)
        m = jnp.maximum(jnp.max(s_prev, axis=1, keepdims=True), jnp.max(s_cur, axis=1, keepdims=True))
        p_prev = jnp.exp(s_prev - m)
        p_cur = jnp.exp(s_cur - m)
        l = jnp.sum(p_prev, axis=1, keepdims=True) + jnp.sum(p_cur, axis=1, keepdims=True)
        o = jnp.dot(p_prev.astype(BF16), vp_ref[:, sl], preferred_element_type=F32)
        o += jnp.dot(p_cur.astype(BF16), vc_ref[:, sl], preferred_element_type=F32)
        o_ref[:, sl] = o / l
        lse_ref[:, sl] = jnp.broadcast_to(m + jnp.log(l), (A_BLOCK, HEAD_DIM))


def dilated_attention(qkv, batch, seq, group):
    window, r = DILATED_GROUPS[group]
    w = window // r
    assert w <= A_BLOCK
    n = seq // r
    assert n % A_BLOCK == 0
    nb = n // A_BLOCK
    width = qkv.shape[1]
    tiles_per_row = width // A_GROUP_WIDTH
    view = qkv.reshape(batch, n, r * width)

    def spec(which, prev):
        def index(b, c, blk):
            row = jnp.maximum(blk - 1, 0) if prev else blk
            return (b, row, c * tiles_per_row + group * 3 + which)
        return pl.BlockSpec((None, A_BLOCK, A_GROUP_WIDTH), index)

    out_spec = pl.BlockSpec((None, A_BLOCK, A_GROUP_WIDTH), lambda b, c, blk: (b, blk, c))
    out_sds = jax.ShapeDtypeStruct((batch, n, r * A_GROUP_WIDTH), F32)
    o, lse = pl.pallas_call(
        functools.partial(_dilated_kernel, w=w),
        out_shape=[out_sds, out_sds],
        grid=(batch, r, nb),
        in_specs=[spec(0, False), spec(1, True), spec(1, False), spec(2, True), spec(2, False)],
        out_specs=[out_spec, out_spec],
        compiler_params=_params(("parallel", "parallel", "arbitrary")),
        name="dilated_attention",
    )(view, view, view, view, view)
    return o.reshape(batch * seq, A_GROUP_WIDTH), lse.reshape(batch * seq, A_GROUP_WIDTH)


def _merge_kernel(o0_ref, o1_ref, o2_ref, l0_ref, l1_ref, l2_ref, out_ref):
    l0, l1, l2 = l0_ref[...], l1_ref[...], l2_ref[...]
    m = jnp.maximum(jnp.maximum(l0, l1), l2)
    e0, e1, e2 = jnp.exp(l0 - m), jnp.exp(l1 - m), jnp.exp(l2 - m)
    num = e0 * o0_ref[...] + e1 * o1_ref[...] + e2 * o2_ref[...]
    out_ref[...] = (num / (e0 + e1 + e2)).astype(out_ref.dtype)


def merge_groups(outs, lses):
    rows, width = outs[0].shape
    tm = _tile(rows, 256)
    spec = pl.BlockSpec((tm, width), lambda i: (i, 0))
    return pl.pallas_call(
        _merge_kernel,
        out_shape=jax.ShapeDtypeStruct((rows, width), BF16),
        grid=(rows // tm,),
        in_specs=[spec] * 6,
        out_specs=spec,
        compiler_params=_params(("parallel",)),
        name="merge_groups",
    )(*outs, *lses)


def _indexer_prep_kernel(lat_ref, gq_ref, lnw_ref, lnb_ref, cos_ref, sin_ref, iqn_ref, ik_ref, iw_ref):
    iqn_ref[...] = _rms(lat_ref[:, :IDX_Q_RANK], gq_ref[...]).astype(iqn_ref.dtype)
    k = lat_ref[:, IDX_Q_RANK:IDX_Q_RANK + IDX_DIM]
    kc = k - jnp.mean(k, axis=-1, keepdims=True)
    kn = kc * lax.rsqrt(jnp.mean(kc * kc, axis=-1, keepdims=True) + NORM_EPS) * lnw_ref[...] + lnb_ref[...]
    ik_ref[...] = _rope(kn, cos_ref[...], sin_ref[...]).astype(ik_ref.dtype)
    iw_ref[...] = lat_ref[:, IDX_Q_RANK + IDX_DIM:] * (IDX_HEADS ** -0.5 * IDX_DIM ** -0.5)


def indexer_prep(lat, gq, lnw, lnb, cos2, sin2):
    rows, width = lat.shape
    tm = _tile(rows, 512)
    row = lambda wd: pl.BlockSpec((tm, wd), lambda i: (i, 0))
    vec = lambda wd: pl.BlockSpec((1, wd), lambda i: (0, 0))
    return pl.pallas_call(
        _indexer_prep_kernel,
        out_shape=[jax.ShapeDtypeStruct((rows, IDX_Q_RANK), BF16),
                   jax.ShapeDtypeStruct((rows, IDX_DIM), BF16),
                   jax.ShapeDtypeStruct((rows, LANES), F32)],
        grid=(rows // tm,),
        in_specs=[row(width), vec(IDX_Q_RANK), vec(IDX_DIM), vec(IDX_DIM), row(HEAD_DIM), row(HEAD_DIM)],
        out_specs=[row(IDX_Q_RANK), row(IDX_DIM), row(LANES)],
        compiler_params=_params(("parallel",)),
        name="indexer_prep",
    )(lat, gq.reshape(1, -1), lnw.reshape(1, -1), lnb.reshape(1, -1), cos2, sin2)


IDX_TQ = 128
IDX_TS = 512


def _indexer_kernel(iq_ref, ik_ref, iw_ref, bias_ref, key_ref, wb_ref, *, n_sel):
    tq, seq = key_ref.shape
    i = pl.program_id(1)
    n_chunks = ((i + 1) * tq + IDX_TS - 1) // IDX_TS
    key_ref[...] = jnp.full((tq, seq), INT_MIN, jnp.int32)
    iw = iw_ref[...]
    for h in range(IDX_HEADS):
        wb_ref[h] = jnp.broadcast_to(iw[:, h:h + 1], (tq, LANES))
    t_pos = i * tq + lax.broadcasted_iota(jnp.int32, (tq, IDX_TS), 0)
    lane = lax.broadcasted_iota(jnp.int32, (tq, IDX_TS), 1)
    nt = (((1,), (1,)), ((), ()))

    def score_chunk(c, carry):
        start = pl.multiple_of(c * IDX_TS, IDX_TS)
        keys = ik_ref[pl.ds(start, IDX_TS), :]
        acc = jnp.zeros((tq, IDX_TS), F32)
        for h in range(IDX_HEADS):
            rel = lax.dot_general(iq_ref[:, h * IDX_DIM:(h + 1) * IDX_DIM], keys, nt,
                                  preferred_element_type=F32)
            wb = wb_ref[h]
            acc += jnp.maximum(rel, 0.0) * jnp.concatenate([wb] * (IDX_TS // LANES), axis=1)
        bits = lax.bitcast_convert_type(acc, jnp.int32)
        ordered = jnp.where(bits >= 0, bits, bits ^ jnp.int32(0x7FFFFFFF))
        key_ref[:, pl.ds(start, IDX_TS)] = jnp.where(start + lane <= t_pos, ordered, INT_MIN)
        return carry

    lax.fori_loop(0, n_chunks, score_chunk, 0)

    def count_ge(cand):
        cand_w = jnp.concatenate([cand] * (IDX_TS // LANES), axis=1)

        def body(c, cnt):
            start = pl.multiple_of(c * IDX_TS, IDX_TS)
            ge = jnp.where(key_ref[:, pl.ds(start, IDX_TS)] >= cand_w, 1.0, 0.0)
            for b in range(IDX_TS // LANES):
                cnt = cnt + ge[:, b * LANES:(b + 1) * LANES]
            return cnt

        cnt = lax.fori_loop(0, n_chunks, body, jnp.zeros((tq, LANES), F32))
        return jnp.sum(cnt, axis=1, keepdims=True)

    def bit_step(step, thr):
        cand = thr + lax.shift_left(jnp.int32(1), 31 - step)
        return jnp.where(count_ge(cand) >= n_sel, cand, thr)

    thr = lax.fori_loop(0, 32, bit_step, jnp.full((tq, LANES), INT_MIN, jnp.int32))
    thr = jnp.maximum(thr, INT_MIN + 1)
    thr_w = jnp.concatenate([thr] * (IDX_TS // LANES), axis=1)
    for c in range(seq // IDX_TS):
        sl = slice(c * IDX_TS, (c + 1) * IDX_TS)
        bias_ref[:, sl] = jnp.where(key_ref[:, sl] >= thr_w, 0.0, MASK_VALUE).astype(bias_ref.dtype)


def indexer_mask(iq, ik, iw, batch, seq):
    n_sel = min(TOPK_MAX, seq // 4)
    tq = IDX_TQ
    assert seq % IDX_TS == 0 and seq % tq == 0
    return pl.pallas_call(
        functools.partial(_indexer_kernel, n_sel=n_sel),
        out_shape=jax.ShapeDtypeStruct((batch, seq, seq), BF16),
        grid=(batch, seq // tq),
        in_specs=[pl.BlockSpec((None, tq, IDX_HEADS * IDX_DIM), lambda b, i: (b, i, 0)),
                  pl.BlockSpec((None, seq, IDX_DIM), lambda b, i: (b, 0, 0)),
                  pl.BlockSpec((None, tq, LANES), lambda b, i: (b, i, 0))],
        out_specs=pl.BlockSpec((None, tq, seq), lambda b, i: (b, i, 0)),
        scratch_shapes=[pltpu.VMEM((tq, seq), jnp.int32), pltpu.VMEM((IDX_HEADS, tq, LANES), F32)],
        compiler_params=_params(("parallel", "arbitrary")),
        name="indexer_mask",
    )(iq.reshape(batch, seq, -1), ik.reshape(batch, seq, -1), iw.reshape(batch, seq, -1))


B_TQ = 256
B_TS = 512
B_REP = B_HEADS // B_KV_HEADS


def _sparse_attn_kernel(q_ref, k_ref, v_ref, bias_ref, o_ref, qs_ref, m_ref, l_ref, acc_ref):
    tq, ts = bias_ref.shape
    i, j = pl.program_id(1), pl.program_id(2)
    last = ((i + 1) * tq - 1) // ts
    nt = (((1,), (1,)), ((), ()))

    @pl.when(j == 0)
    def _():
        scale = HEAD_DIM ** -0.5
        for g in range(B_KV_HEADS):
            for r in range(B_REP):
                h = g * B_REP + r
                qh = q_ref[:, h * HEAD_DIM:(h + 1) * HEAD_DIM].astype(F32) * scale
                qs_ref[g, r * tq:(r + 1) * tq, :] = qh.astype(qs_ref.dtype)
        m_ref[...] = jnp.full(m_ref.shape, MASK_VALUE, F32)
        l_ref[...] = jnp.zeros(l_ref.shape, F32)
        acc_ref[...] = jnp.zeros(acc_ref.shape, F32)

    @pl.when(j <= last)
    def _():
        bias = bias_ref[...].astype(F32)
        for g in range(B_KV_HEADS):
            kg = k_ref[:, g * HEAD_DIM:(g + 1) * HEAD_DIM]
            vg = v_ref[:, g * HEAD_DIM:(g + 1) * HEAD_DIM]
            s = lax.dot_general(qs_ref[g], kg, nt, preferred_element_type=F32)
            s = (s.reshape(B_REP, tq, ts) + bias[None]).reshape(B_REP * tq, ts)
            m_prev = m_ref[g]
            m_new = jnp.maximum(m_prev, jnp.max(s, axis=1, keepdims=True))
            alpha = jnp.exp(m_prev - m_new)
            p = jnp.exp(s - m_new[:, :1])
            l_ref[g] = alpha * l_ref[g] + jnp.sum(p, axis=1, keepdims=True)
            acc_ref[g] = alpha * acc_ref[g] + jnp.dot(p.astype(BF16), vg, preferred_element_type=F32)
            m_ref[g] = m_new

    @pl.when(j == last)
    def _():
        for g in range(B_KV_HEADS):
            out = acc_ref[g] / l_ref[g]
            for r in range(B_REP):
                h = g * B_REP + r
                o_ref[:, h * HEAD_DIM:(h + 1) * HEAD_DIM] = out[r * tq:(r + 1) * tq].astype(o_ref.dtype)


def sparse_attention(qkv, bias, batch, seq):
    tq, ts = _tile(seq, B_TQ), _tile(seq, B_TS)
    view = qkv.reshape(batch, seq, B_QKV_WIDTH)
    k_tile = B_Q_WIDTH // B_KV_WIDTH

    def kv_block(i, j):
        return jnp.minimum(j, ((i + 1) * tq - 1) // ts)

    rows = B_REP * tq
    out = pl.pallas_call(
        _sparse_attn_kernel,
        out_shape=jax.ShapeDtypeStruct((batch, seq, B_Q_WIDTH), BF16),
        grid=(batch, seq // tq, seq // ts),
        in_specs=[pl.BlockSpec((None, tq, B_Q_WIDTH), lambda b, i, j: (b, i, 0)),
                  pl.BlockSpec((None, ts, B_KV_WIDTH), lambda b, i, j: (b, kv_block(i, j), k_tile)),
                  pl.BlockSpec((None, ts, B_KV_WIDTH), lambda b, i, j: (b, kv_block(i, j), k_tile + 1)),
                  pl.BlockSpec((None, tq, ts), lambda b, i, j: (b, i, kv_block(i, j)))],
        out_specs=pl.BlockSpec((None, tq, B_Q_WIDTH), lambda b, i, j: (b, i, 0)),
        scratch_shapes=[pltpu.VMEM((B_KV_HEADS, rows, HEAD_DIM), BF16),
                        pltpu.VMEM((B_KV_HEADS, rows, LANES), F32),
                        pltpu.VMEM((B_KV_HEADS, rows, LANES), F32),
                        pltpu.VMEM((B_KV_HEADS, rows, HEAD_DIM), F32)],
        compiler_params=_params(("parallel", "parallel", "arbitrary")),
        name="sparse_attention",
    )(view, view, view, bias)
    return out.reshape(batch * seq, B_Q_WIDTH)


def _rope_tables(positions):
    inv_freq = ROPE_THETA ** (-jnp.arange(0, HEAD_DIM, 2, dtype=F32) / HEAD_DIM)
    ang = positions.astype(F32).reshape(-1, 1) * inv_freq
    cos, sin = jnp.cos(ang), jnp.sin(ang)
    return jnp.concatenate([cos, cos], axis=-1), jnp.concatenate([-sin, sin], axis=-1)


def _mlp(f_in, w_up, w_down):
    u = matmul(f_in, w_up.astype(BF16), act="relu2")
    return matmul(u, w_down.astype(BF16))


def _mixer_dilated(a, cos2, sin2, w_in, w_out, batch, seq):
    tn = 1024
    per_part = A_GROUP_WIDTH // tn
    rope_pred = lambda j: (j // per_part) % 3 != 2
    qkv = matmul(a, w_in.astype(BF16), rope=(cos2, sin2, rope_pred), tn=tn)
    outs, lses = zip(*[dilated_attention(qkv, batch, seq, g) for g in range(N_GROUPS)])
    o = merge_groups(outs, lses)
    return matmul(o, w_out.astype(BF16))


def _mixer_sparse(a, cos2, sin2, w_in, idx_q_norm, w_idx_up, idx_k_w, idx_k_b, w_out, batch, seq):
    tn = 1024
    n_rope = (B_Q_WIDTH + B_KV_WIDTH) // tn
    w_qkv = w_in[:, :B_QKV_WIDTH].astype(BF16)
    w_idx = w_in[:, B_QKV_WIDTH:]
    w_idx = jnp.pad(w_idx, ((0, 0), (0, IDX_Q_RANK + IDX_DIM + LANES - w_idx.shape[1]))).astype(BF16)
    qkv = matmul(a, w_qkv, rope=(cos2, sin2, lambda j: j < n_rope), tn=tn)
    lat = matmul(a, w_idx, out_dtype=F32, tn=w_idx.shape[1])
    iq_n, ik, iw = indexer_prep(lat, idx_q_norm, idx_k_w, idx_k_b, cos2, sin2)
    iq = matmul(iq_n, w_idx_up.astype(BF16), rope=(cos2, sin2, lambda j: j >= 0), tn=tn)
    bias = indexer_mask(iq, ik, iw, batch, seq)
    o = sparse_attention(qkv, bias, batch, seq)
    return matmul(o, w_out.astype(BF16))


def kernel(x, positions, attn_pre_norm, attn_post_norm, mlp_pre_norm, mlp_post_norm, w_in_a, w_out_a,
           w_in_b, idx_q_norm, w_idx_up, idx_k_norm_w, idx_k_norm_b, w_out_b, w_up, w_down):
    batch, seq, d_model = x.shape
    depth = attn_pre_norm.shape[0]
    cos2, sin2 = _rope_tables(positions)
    h = x.reshape(batch * seq, d_model)
    a = rms_norm_cast(h, attn_pre_norm[0])
    for i in range(depth):
        j = i // 2
        if i % 2 == 0:
            m = _mixer_dilated(a, cos2, sin2, w_in_a[j], w_out_a[j], batch, seq)
        else:
            m = _mixer_sparse(a, cos2, sin2, w_in_b[j], idx_q_norm[j], w_idx_up[j],
                              idx_k_norm_w[j], idx_k_norm_b[j], w_out_b[j], batch, seq)
        h, f_in = residual_norm(h, m, attn_post_norm[i], mlp_pre_norm[i])
        f = _mlp(f_in, w_up[i], w_down[i])
        h, a = residual_norm(h, f, mlp_post_norm[i], attn_pre_norm[i + 1] if i + 1 < depth else None)
    return h.reshape(batch, seq, d_model)
```

```python
import functools
import math

import jax
import jax.numpy as jnp
from jax import lax
from jax.experimental import pallas as pl
from jax.experimental.pallas import tpu as pltpu

HEAD_DIM = 128
ROPE_THETA = 10000.0
NORM_EPS = 1e-6
DILATED_GROUPS = ((128, 1), (512, 4), (2048, 16))
N_GROUPS = 3
A_HEADS_PER_GROUP = 16
A_BLOCK = 128
A_GROUP_WIDTH = A_HEADS_PER_GROUP * HEAD_DIM
B_HEADS = 32
B_KV_HEADS = 8
IDX_HEADS = 32
IDX_DIM = 128
IDX_Q_RANK = 1024
TOPK_MAX = 256
B_Q_WIDTH = B_HEADS * HEAD_DIM
B_KV_WIDTH = B_KV_HEADS * HEAD_DIM
B_QKV_WIDTH = B_Q_WIDTH + 2 * B_KV_WIDTH

LANES = 128
VMEM_LIMIT_BYTES = 56 * 2**20
MASK_VALUE = -1e30
INT_MIN = -2**31
LOG2E = math.log2(math.e)
NT_DIMS = (((1,), (1,)), ((), ()))

F32 = jnp.float32
BF16 = jnp.bfloat16


def _tile(dim, pref):
    t = min(dim, pref)
    while dim % t:
        t //= 2
    return t


def _params(semantics):
    return pltpu.CompilerParams(dimension_semantics=semantics, vmem_limit_bytes=VMEM_LIMIT_BYTES)


def _lanes(x, width):
    return jnp.concatenate([x] * (width // LANES), axis=1)


def _rms(x, g):
    return x * lax.rsqrt(jnp.mean(x * x, axis=-1, keepdims=True) + NORM_EPS) * g


def _norm_kernel(x_ref, g_ref, a_ref):
    a_ref[...] = _rms(x_ref[...], g_ref[...]).astype(a_ref.dtype)


def rms_norm_cast(x, g):
    m, d = x.shape
    tm = _tile(m, 256)
    return pl.pallas_call(
        _norm_kernel,
        out_shape=jax.ShapeDtypeStruct((m, d), BF16),
        grid=(m // tm,),
        in_specs=[pl.BlockSpec((tm, d), lambda i: (i, 0)), pl.BlockSpec((1, d), lambda i: (0, 0))],
        out_specs=pl.BlockSpec((tm, d), lambda i: (i, 0)),
        compiler_params=_params(("parallel",)),
        name="rms_norm_cast",
    )(x, g.reshape(1, d))


def _residual_kernel(h_ref, m_ref, gpost_ref, *rest, with_next):
    h = h_ref[...] + _rms(m_ref[...].astype(F32), gpost_ref[...])
    if with_next:
        gpre_ref, h_out_ref, a_ref = rest
        a_ref[...] = _rms(h, gpre_ref[...]).astype(a_ref.dtype)
    else:
        (h_out_ref,) = rest
    h_out_ref[...] = h


def residual_norm(h, m, g_post, g_pre_next=None):
    rows, d = h.shape
    tm = _tile(rows, 256)
    with_next = g_pre_next is not None
    row_spec = pl.BlockSpec((tm, d), lambda i: (i, 0))
    g_spec = pl.BlockSpec((1, d), lambda i: (0, 0))
    args = [h, m, g_post.reshape(1, d)]
    in_specs = [row_spec, row_spec, g_spec]
    out_shape = [jax.ShapeDtypeStruct((rows, d), F32)]
    out_specs = [row_spec]
    if with_next:
        args.append(g_pre_next.reshape(1, d))
        in_specs.append(g_spec)
        out_shape.append(jax.ShapeDtypeStruct((rows, d), BF16))
        out_specs.append(row_spec)
    out = pl.pallas_call(
        functools.partial(_residual_kernel, with_next=with_next),
        out_shape=out_shape,
        grid=(rows // tm,),
        in_specs=in_specs,
        out_specs=out_specs,
        compiler_params=_params(("parallel",)),
        name="residual_norm",
    )(*args)
    return (out[0], out[1]) if with_next else (out[0], None)


def _rope(x, cos2, sin2):
    return x * cos2 + pltpu.roll(x, HEAD_DIM // 2, 1) * sin2


def _matmul_kernel(a_ref, w_ref, *rest, nk, act, rope_pred):
    if rope_pred is not None:
        cos_ref, sin_ref, *rest = rest
    o_ref, *scratch = rest

    def finish(acc):
        if act == "relu2":
            r = jnp.maximum(acc, 0.0)
            acc = r * r
        if rope_pred is None:
            o_ref[...] = acc.astype(o_ref.dtype)
            return
        do_rope = rope_pred(pl.program_id(1))

        @pl.when(do_rope)
        def _():
            cos2 = cos_ref[...]
            sin2 = sin_ref[...]
            for c in range(o_ref.shape[1] // HEAD_DIM):
                sl = slice(c * HEAD_DIM, (c + 1) * HEAD_DIM)
                o_ref[:, sl] = _rope(acc[:, sl], cos2, sin2).astype(o_ref.dtype)

        @pl.when(jnp.logical_not(do_rope))
        def _():
            o_ref[...] = acc.astype(o_ref.dtype)

    prod = jnp.dot(a_ref[...], w_ref[...], preferred_element_type=F32)
    if nk == 1:
        finish(prod)
        return
    (acc_ref,) = scratch
    k = pl.program_id(2)

    @pl.when(k == 0)
    def _():
        acc_ref[...] = prod

    @pl.when(jnp.logical_and(k > 0, k < nk - 1))
    def _():
        acc_ref[...] += prod

    @pl.when(k == nk - 1)
    def _():
        finish(acc_ref[...] + prod)


def matmul(a, w, *, out_dtype=BF16, act=None, rope=None, tm=1024, tn=1024, tk=4096):
    m, kdim = a.shape
    n = w.shape[1]
    tm, tn, tk = _tile(m, tm), _tile(n, tn), _tile(kdim, tk)
    nk = kdim // tk
    in_specs = [pl.BlockSpec((tm, tk), lambda i, j, k: (i, k)),
                pl.BlockSpec((tk, tn), lambda i, j, k: (k, j))]
    args = [a, w]
    rope_pred = None
    if rope is not None:
        cos2, sin2, rope_pred = rope
        tab_spec = pl.BlockSpec((tm, HEAD_DIM), lambda i, j, k: (i, 0))
        in_specs += [tab_spec, tab_spec]
        args += [cos2, sin2]
    scratch = [pltpu.VMEM((tm, tn), F32)] if nk > 1 else []
    return pl.pallas_call(
        functools.partial(_matmul_kernel, nk=nk, act=act, rope_pred=rope_pred),
        out_shape=jax.ShapeDtypeStruct((m, n), out_dtype),
        grid=(m // tm, n // tn, nk),
        in_specs=in_specs,
        out_specs=pl.BlockSpec((tm, tn), lambda i, j, k: (i, j)),
        scratch_shapes=scratch,
        compiler_params=_params(("parallel", "parallel", "arbitrary")),
        name="matmul",
    )(*args)


A_PERIOD = 16


def _to_residue_major(x, batch, seq):
    u = seq // A_PERIOD
    return x.reshape(batch, u, 4, 4, -1).transpose(0, 3, 2, 1, 4).reshape(batch * seq, -1)


def _from_residue_major(x, batch, seq):
    u = seq // A_PERIOD
    return x.reshape(batch, 4, 4, u, -1).transpose(0, 3, 2, 1, 4).reshape(batch * seq, -1)


def _dilated_kernel(q_ref, kp_ref, kc_ref, vp_ref, vc_ref, o_ref, lse_ref, *, w, rows, local_index, blk_axis):
    has_prev = pl.program_id(blk_axis) > 0
    qi = local_index(lax.broadcasted_iota(jnp.int32, (rows, rows), 0))
    kj = local_index(lax.broadcasted_iota(jnp.int32, (rows, rows), 1))
    mask_prev = jnp.logical_and(qi + rows - kj <= w, has_prev)
    mask_cur = jnp.logical_and(kj <= qi, qi - kj <= w)
    scale = HEAD_DIM ** -0.5
    lead = (slice(None),) * (len(q_ref.shape) - 1)
    out_shape = o_ref.shape[:-1] + (HEAD_DIM,)
    for h in range(A_HEADS_PER_GROUP):
        idx = lead + (slice(h * HEAD_DIM, (h + 1) * HEAD_DIM),)
        load = lambda ref: ref[idx].reshape(rows, HEAD_DIM)
        q = load(q_ref)
        s_prev = lax.dot_general(q, load(kp_ref), NT_DIMS, preferred_element_type=F32) * scale
        s_cur = lax.dot_general(q, load(kc_ref), NT_DIMS, preferred_element_type=F32) * scale
        s_prev = jnp.where(mask_prev, s_prev, MASK_VALUE)
        s_cur = jnp.where(mask_cur, s_cur, MASK_VALUE)
        m = jnp.maximum(jnp.max(s_prev, axis=1, keepdims=True), jnp.max(s_cur, axis=1, keepdims=True))
        p_prev = jnp.exp(s_prev - m)
        p_cur = jnp.exp(s_cur - m)
        l = jnp.sum(p_prev, axis=1, keepdims=True) + jnp.sum(p_cur, axis=1, keepdims=True)
        o = jnp.dot(p_prev.astype(BF16), load(vp_ref), preferred_element_type=F32)
        o += jnp.dot(p_cur.astype(BF16), load(vc_ref), preferred_element_type=F32)
        o_ref[idx] = (o / l).astype(o_ref.dtype).reshape(out_shape)
        lse_ref[idx] = jnp.broadcast_to(m + jnp.log(l), (rows, HEAD_DIM)).reshape(out_shape)


def dilated_attention(qkv, batch, seq, group):
    window, r = DILATED_GROUPS[group]
    w = window // r
    u = seq // A_PERIOD
    width = qkv.shape[1]
    view = qkv.reshape(batch, 4, 4, u, width)
    if r == 16:
        rows, n_blk = A_BLOCK, u // A_BLOCK
        block = (None, None, None, rows, A_GROUP_WIDTH)
        grid = (batch, 4, 4, n_blk)
        place = lambda g, blk: (g[0], g[1], g[2], blk)
        local_index = lambda rho: rho
    elif r == 4:
        rows, n_blk = A_BLOCK, u // (A_BLOCK // 4)
        block = (None, None, 4, rows // 4, A_GROUP_WIDTH)
        grid = (batch, 4, n_blk)
        place = lambda g, blk: (g[0], g[1], 0, blk)
        local_index = lambda rho: 4 * (rho % (rows // 4)) + rho // (rows // 4)
    else:
        assert r == 1
        rows, n_blk = 2 * A_BLOCK, u // (2 * A_BLOCK // A_PERIOD)
        per = rows // A_PERIOD
        block = (None, 4, 4, per, A_GROUP_WIDTH)
        grid = (batch, n_blk)
        place = lambda g, blk: (g[0], 0, 0, blk)
        local_index = lambda rho: A_PERIOD * (rho % per) + 4 * ((rho // per) % 4) + rho // (4 * per)
    assert w <= rows and n_blk * rows * r == seq

    def spec(which, prev):
        def index(*g):
            blk = jnp.maximum(g[-1] - 1, 0) if prev else g[-1]
            return place(g, blk) + (group * 3 + which,)
        return pl.BlockSpec(block, index)

    out_spec = pl.BlockSpec(block, lambda *g: place(g, g[-1]) + (0,))
    o, lse = pl.pallas_call(
        functools.partial(_dilated_kernel, w=w, rows=rows, local_index=local_index, blk_axis=len(grid) - 1),
        out_shape=[jax.ShapeDtypeStruct((batch, 4, 4, u, A_GROUP_WIDTH), BF16),
                   jax.ShapeDtypeStruct((batch, 4, 4, u, A_GROUP_WIDTH), F32)],
        grid=grid,
        in_specs=[spec(0, False), spec(1, True), spec(1, False), spec(2, True), spec(2, False)],
        out_specs=[out_spec, out_spec],
        compiler_params=_params(("parallel",) * (len(grid) - 1) + ("arbitrary",)),
        name="dilated_attention",
    )(view, view, view, view, view)
    return o.reshape(batch * seq, A_GROUP_WIDTH), lse.reshape(batch * seq, A_GROUP_WIDTH)


def _merge_kernel(o0_ref, o1_ref, o2_ref, l0_ref, l1_ref, l2_ref, out_ref):
    l0, l1, l2 = l0_ref[...], l1_ref[...], l2_ref[...]
    m = jnp.maximum(jnp.maximum(l0, l1), l2)
    e0, e1, e2 = jnp.exp(l0 - m), jnp.exp(l1 - m), jnp.exp(l2 - m)
    num = e0 * o0_ref[...].astype(F32) + e1 * o1_ref[...].astype(F32) + e2 * o2_ref[...].astype(F32)
    out_ref[...] = (num / (e0 + e1 + e2)).astype(out_ref.dtype)


def merge_groups(outs, lses):
    rows, width = outs[0].shape
    tm = _tile(rows, 256)
    spec = pl.BlockSpec((tm, width), lambda i: (i, 0))
    return pl.pallas_call(
        _merge_kernel,
        out_shape=jax.ShapeDtypeStruct((rows, width), BF16),
        grid=(rows // tm,),
        in_specs=[spec] * 6,
        out_specs=spec,
        compiler_params=_params(("parallel",)),
        name="merge_groups",
    )(*outs, *lses)


def _indexer_prep_kernel(lat_ref, gq_ref, lnw_ref, lnb_ref, cos_ref, sin_ref, iqn_ref, ik_ref, iw_ref):
    iqn_ref[...] = _rms(lat_ref[:, :IDX_Q_RANK], gq_ref[...]).astype(iqn_ref.dtype)
    k = lat_ref[:, IDX_Q_RANK:IDX_Q_RANK + IDX_DIM]
    kc = k - jnp.mean(k, axis=-1, keepdims=True)
    kn = kc * lax.rsqrt(jnp.mean(kc * kc, axis=-1, keepdims=True) + NORM_EPS) * lnw_ref[...] + lnb_ref[...]
    ik_ref[...] = _rope(kn, cos_ref[...], sin_ref[...]).astype(ik_ref.dtype)
    iw_ref[...] = lat_ref[:, IDX_Q_RANK + IDX_DIM:] * (IDX_HEADS ** -0.5 * IDX_DIM ** -0.5)


def indexer_prep(lat, gq, lnw, lnb, cos2, sin2):
    rows, width = lat.shape
    tm = _tile(rows, 512)
    row = lambda wd: pl.BlockSpec((tm, wd), lambda i: (i, 0))
    vec = lambda wd: pl.BlockSpec((1, wd), lambda i: (0, 0))
    return pl.pallas_call(
        _indexer_prep_kernel,
        out_shape=[jax.ShapeDtypeStruct((rows, IDX_Q_RANK), BF16),
                   jax.ShapeDtypeStruct((rows, IDX_DIM), BF16),
                   jax.ShapeDtypeStruct((rows, LANES), F32)],
        grid=(rows // tm,),
        in_specs=[row(width), vec(IDX_Q_RANK), vec(IDX_DIM), vec(IDX_DIM), row(HEAD_DIM), row(HEAD_DIM)],
        out_specs=[row(IDX_Q_RANK), row(IDX_DIM), row(LANES)],
        compiler_params=_params(("parallel",)),
        name="indexer_prep",
    )(lat, gq.reshape(1, -1), lnw.reshape(1, -1), lnb.reshape(1, -1), cos2, sin2)


IDX_TQ = 128
IDX_TS = 512
IDX_HEAD_BATCH = 8


def _indexer_kernel(iq_ref, ik_ref, iw_ref, bias_ref, key_ref, qs_ref, wb_ref, *, n_sel):
    tq, seq = key_ref.shape
    i = pl.program_id(1)
    n_chunks = ((i + 1) * tq + IDX_TS - 1) // IDX_TS
    key_ref[...] = jnp.full((tq, seq), INT_MIN, jnp.int32)
    iw = iw_ref[...]
    for h in range(IDX_HEADS):
        qs_ref[h * tq:(h + 1) * tq, :] = iq_ref[:, h * IDX_DIM:(h + 1) * IDX_DIM]
        wb_ref[h] = jnp.broadcast_to(iw[:, h:h + 1], (tq, LANES))
    t_pos = i * tq + lax.broadcasted_iota(jnp.int32, (tq, IDX_TS), 0)
    lane = lax.broadcasted_iota(jnp.int32, (tq, IDX_TS), 1)
    hb_rows = IDX_HEAD_BATCH * tq

    def score_chunk(c, carry):
        start = pl.multiple_of(c * IDX_TS, IDX_TS)
        keys = ik_ref[pl.ds(start, IDX_TS), :]
        acc = jnp.zeros((tq, IDX_TS), F32)
        for hb in range(IDX_HEADS // IDX_HEAD_BATCH):
            rel = lax.dot_general(qs_ref[hb * hb_rows:(hb + 1) * hb_rows, :], keys, NT_DIMS,
                                  preferred_element_type=F32)
            for hh in range(IDX_HEAD_BATCH):
                wb = _lanes(wb_ref[hb * IDX_HEAD_BATCH + hh], IDX_TS)
                acc += jnp.maximum(rel[hh * tq:(hh + 1) * tq], 0.0) * wb
        bits = lax.bitcast_convert_type(acc, jnp.int32)
        ordered = jnp.where(bits >= 0, bits, bits ^ jnp.int32(0x7FFFFFFF))
        key_ref[:, pl.ds(start, IDX_TS)] = jnp.where(start + lane <= t_pos, ordered, INT_MIN)
        return carry

    lax.fori_loop(0, n_chunks, score_chunk, 0)

    def count_ge(cand):
        cand_w = _lanes(cand, IDX_TS)

        def body(c, cnt):
            start = pl.multiple_of(c * IDX_TS, IDX_TS)
            ge = jnp.where(key_ref[:, pl.ds(start, IDX_TS)] >= cand_w, 1.0, 0.0)
            for b in range(IDX_TS // LANES):
                cnt = cnt + ge[:, b * LANES:(b + 1) * LANES]
            return cnt

        cnt = lax.fori_loop(0, n_chunks, body, jnp.zeros((tq, LANES), F32))
        return jnp.sum(cnt, axis=1, keepdims=True)

    def bit_step(step, thr):
        cand = thr + lax.shift_left(jnp.int32(1), 31 - step)
        return jnp.where(count_ge(cand) >= n_sel, cand, thr)

    thr = lax.fori_loop(0, 32, bit_step, jnp.full((tq, LANES), INT_MIN, jnp.int32))
    thr = jnp.maximum(thr, INT_MIN + 1)
    thr_w = _lanes(thr, IDX_TS)
    for c in range(seq // IDX_TS):
        sl = slice(c * IDX_TS, (c + 1) * IDX_TS)
        bias_ref[:, sl] = jnp.where(key_ref[:, sl] >= thr_w, 0.0, MASK_VALUE).astype(bias_ref.dtype)


def indexer_mask(iq, ik, iw, batch, seq):
    n_sel = min(TOPK_MAX, seq // 4)
    tq = IDX_TQ
    assert seq % IDX_TS == 0 and seq % tq == 0
    return pl.pallas_call(
        functools.partial(_indexer_kernel, n_sel=n_sel),
        out_shape=jax.ShapeDtypeStruct((batch, seq, seq), BF16),
        grid=(batch, seq // tq),
        in_specs=[pl.BlockSpec((None, tq, IDX_HEADS * IDX_DIM), lambda b, i: (b, i, 0)),
                  pl.BlockSpec((None, seq, IDX_DIM), lambda b, i: (b, 0, 0)),
                  pl.BlockSpec((None, tq, LANES), lambda b, i: (b, i, 0))],
        out_specs=pl.BlockSpec((None, tq, seq), lambda b, i: (b, i, 0)),
        scratch_shapes=[pltpu.VMEM((tq, seq), jnp.int32),
                        pltpu.VMEM((IDX_HEADS * tq, IDX_DIM), BF16),
                        pltpu.VMEM((IDX_HEADS, tq, LANES), F32)],
        compiler_params=_params(("parallel", "arbitrary")),
        name="indexer_mask",
    )(iq.reshape(batch, seq, -1), ik.reshape(batch, seq, -1), iw.reshape(batch, seq, -1))


B_TQ = 256
B_TS = 512
B_REP = B_HEADS // B_KV_HEADS
B_CHUNK = 128


def _sparse_attn_kernel(q_ref, k_ref, v_ref, bias_ref, o_ref,
                        qs_ref, vo_ref, biasf_ref, s_ref, p_ref, alpha_ref, m_ref, accl_ref):
    tq, ts = bias_ref.shape
    rows = B_REP * tq
    i, j = pl.program_id(1), pl.program_id(2)
    last = ((i + 1) * tq - 1) // ts

    @pl.when(j == 0)
    def _():
        scale = HEAD_DIM ** -0.5 * LOG2E
        for g in range(B_KV_HEADS):
            for r in range(B_REP):
                h = g * B_REP + r
                qh = q_ref[:, h * HEAD_DIM:(h + 1) * HEAD_DIM].astype(F32) * scale
                qs_ref[g, r * tq:(r + 1) * tq, :] = qh.astype(qs_ref.dtype)
        m_ref[...] = jnp.full(m_ref.shape, MASK_VALUE, F32)
        accl_ref[...] = jnp.zeros(accl_ref.shape, F32)

    @pl.when(j <= last)
    def _():
        biasf_ref[...] = bias_ref[...].astype(F32)
        ones = jnp.ones((ts, LANES), BF16)
        for g in range(B_KV_HEADS):
            vo_ref[g, :, :HEAD_DIM] = v_ref[:, g * HEAD_DIM:(g + 1) * HEAD_DIM]
            vo_ref[g, :, HEAD_DIM:] = ones

        def logits(g):
            kg = k_ref[:, g * HEAD_DIM:(g + 1) * HEAD_DIM]
            s_ref[g % 2] = lax.dot_general(qs_ref[g], kg, NT_DIMS, preferred_element_type=F32)

        logits(0)
        for g in range(B_KV_HEADS):
            if g + 1 < B_KV_HEADS:
                logits(g + 1)
            par = g % 2
            for c in range(rows // B_CHUNK):
                rs = slice(c * B_CHUNK, (c + 1) * B_CHUNK)
                qrow = (c * B_CHUNK) % tq
                s = s_ref[par, rs, :] + biasf_ref[qrow:qrow + B_CHUNK, :]
                m_prev = m_ref[g, rs, :]
                m_new = jnp.maximum(m_prev, jnp.max(s, axis=1, keepdims=True))
                p_ref[par, rs, :] = jnp.exp2(s - _lanes(m_new, ts)).astype(BF16)
                alpha_ref[par, rs, :] = jnp.exp2(m_prev - m_new)
                m_ref[g, rs, :] = m_new
            pv = jnp.dot(p_ref[par], vo_ref[g], preferred_element_type=F32)
            alpha = alpha_ref[par]
            accl_ref[g] = jnp.concatenate([alpha, alpha], axis=1) * accl_ref[g] + pv

    @pl.when(j == last)
    def _():
        for g in range(B_KV_HEADS):
            out = accl_ref[g, :, :HEAD_DIM] / accl_ref[g, :, HEAD_DIM:]
            for r in range(B_REP):
                h = g * B_REP + r
                o_ref[:, h * HEAD_DIM:(h + 1) * HEAD_DIM] = out[r * tq:(r + 1) * tq].astype(o_ref.dtype)


def sparse_attention(qkv, bias, batch, seq):
    tq, ts = _tile(seq, B_TQ), _tile(seq, B_TS)
    view = qkv.reshape(batch, seq, B_QKV_WIDTH)
    k_tile = B_Q_WIDTH // B_KV_WIDTH

    def kv_block(i, j):
        return jnp.minimum(j, ((i + 1) * tq - 1) // ts)

    rows = B_REP * tq
    out = pl.pallas_call(
        _sparse_attn_kernel,
        out_shape=jax.ShapeDtypeStruct((batch, seq, B_Q_WIDTH), BF16),
        grid=(batch, seq // tq, seq // ts),
        in_specs=[pl.BlockSpec((None, tq, B_Q_WIDTH), lambda b, i, j: (b, i, 0)),
                  pl.BlockSpec((None, ts, B_KV_WIDTH), lambda b, i, j: (b, kv_block(i, j), k_tile)),
                  pl.BlockSpec((None, ts, B_KV_WIDTH), lambda b, i, j: (b, kv_block(i, j), k_tile + 1)),
                  pl.BlockSpec((None, tq, ts), lambda b, i, j: (b, i, kv_block(i, j)))],
        out_specs=pl.BlockSpec((None, tq, B_Q_WIDTH), lambda b, i, j: (b, i, 0)),
        scratch_shapes=[pltpu.VMEM((B_KV_HEADS, rows, HEAD_DIM), BF16),
                        pltpu.VMEM((B_KV_HEADS, ts, 2 * HEAD_DIM), BF16),
                        pltpu.VMEM((tq, ts), F32),
                        pltpu.VMEM((2, rows, ts), F32),
                        pltpu.VMEM((2, rows, ts), BF16),
                        pltpu.VMEM((2, rows, LANES), F32),
                        pltpu.VMEM((B_KV_HEADS, rows, LANES), F32),
                        pltpu.VMEM((B_KV_HEADS, rows, 2 * HEAD_DIM), F32)],
        compiler_params=_params(("parallel", "parallel", "arbitrary")),
        name="sparse_attention",
    )(view, view, view, bias)
    return out.reshape(batch * seq, B_Q_WIDTH)


def _rope_tables(positions):
    inv_freq = ROPE_THETA ** (-jnp.arange(0, HEAD_DIM, 2, dtype=F32) / HEAD_DIM)
    ang = positions.astype(F32).reshape(-1, 1) * inv_freq
    cos, sin = jnp.cos(ang), jnp.sin(ang)
    return jnp.concatenate([cos, cos], axis=-1), jnp.concatenate([-sin, sin], axis=-1)


def _mlp(f_in, w_up, w_down):
    u = matmul(f_in, w_up.astype(BF16), act="relu2")
    return matmul(u, w_down.astype(BF16))


def _mixer_dilated(a, cos2, sin2, w_in, w_out, batch, seq):
    tn = 1024
    per_part = A_GROUP_WIDTH // tn
    rope_pred = lambda j: (j // per_part) % 3 != 2
    a, cos2, sin2 = (_to_residue_major(t, batch, seq) for t in (a, cos2, sin2))
    qkv = matmul(a, w_in.astype(BF16), rope=(cos2, sin2, rope_pred), tn=tn)
    outs, lses = zip(*[dilated_attention(qkv, batch, seq, g) for g in range(N_GROUPS)])
    o = _from_residue_major(merge_groups(outs, lses), batch, seq)
    return matmul(o, w_out.astype(BF16))


def _mixer_sparse(a, cos2, sin2, w_in, idx_q_norm, w_idx_up, idx_k_w, idx_k_b, w_out, batch, seq):
    tn = 1024
    n_rope = (B_Q_WIDTH + B_KV_WIDTH) // tn
    w_qkv = w_in[:, :B_QKV_WIDTH].astype(BF16)
    w_idx = w_in[:, B_QKV_WIDTH:]
    w_idx = jnp.pad(w_idx, ((0, 0), (0, IDX_Q_RANK + IDX_DIM + LANES - w_idx.shape[1]))).astype(BF16)
    qkv = matmul(a, w_qkv, rope=(cos2, sin2, lambda j: j < n_rope), tn=tn)
    lat = matmul(a, w_idx, out_dtype=F32, tn=w_idx.shape[1])
    iq_n, ik, iw = indexer_prep(lat, idx_q_norm, idx_k_w, idx_k_b, cos2, sin2)
    iq = matmul(iq_n, w_idx_up.astype(BF16), rope=(cos2, sin2, lambda j: j >= 0), tn=tn)
    bias = indexer_mask(iq, ik, iw, batch, seq)
    o = sparse_attention(qkv, bias, batch, seq)
    return matmul(o, w_out.astype(BF16))


def kernel(x, positions, attn_pre_norm, attn_post_norm, mlp_pre_norm, mlp_post_norm, w_in_a, w_out_a,
           w_in_b, idx_q_norm, w_idx_up, idx_k_norm_w, idx_k_norm_b, w_out_b, w_up, w_down):
    batch, seq, d_model = x.shape
    depth = attn_pre_norm.shape[0]
    cos2, sin2 = _rope_tables(positions)
    h = x.reshape(batch * seq, d_model)
    a = rms_norm_cast(h, attn_pre_norm[0])
    for i in range(depth):
        j = i // 2
        if i % 2 == 0:
            m = _mixer_dilated(a, cos2, sin2, w_in_a[j], w_out_a[j], batch, seq)
        else:
            m = _mixer_sparse(a, cos2, sin2, w_in_b[j], idx_q_norm[j], w_idx_up[j],
                              idx_k_norm_w[j], idx_k_norm_b[j], w_out_b[j], batch, seq)
        h, f_in = residual_norm(h, m, attn_post_norm[i], mlp_pre_norm[i])
        f = _mlp(f_in, w_up[i], w_down[i])
        h, a = residual_norm(h, f, mlp_post_norm[i], attn_pre_norm[i + 1] if i + 1 < depth else None)
    return h.reshape(batch, seq, d_model)
```

```python
import functools
import math

import jax
import jax.numpy as jnp
from jax import lax
from jax.experimental import pallas as pl
from jax.experimental.pallas import tpu as pltpu

HEAD_DIM = 128
ROPE_THETA = 10000.0
NORM_EPS = 1e-6
DILATED_GROUPS = ((128, 1), (512, 4), (2048, 16))
N_GROUPS = 3
A_HEADS_PER_GROUP = 16
A_BLOCK = 128
A_GROUP_WIDTH = A_HEADS_PER_GROUP * HEAD_DIM
B_HEADS = 32
B_KV_HEADS = 8
IDX_HEADS = 32
IDX_DIM = 128
IDX_Q_RANK = 1024
TOPK_MAX = 256
B_Q_WIDTH = B_HEADS * HEAD_DIM
B_KV_WIDTH = B_KV_HEADS * HEAD_DIM
B_QKV_WIDTH = B_Q_WIDTH + 2 * B_KV_WIDTH

LANES = 128
VMEM_LIMIT_BYTES = 56 * 2**20
MASK_VALUE = -1e30
INT_MIN = -2**31
LOG2E = math.log2(math.e)
NT_DIMS = (((1,), (1,)), ((), ()))

F32 = jnp.float32
BF16 = jnp.bfloat16


def _tile(dim, pref):
    t = min(dim, pref)
    if dim % t:
        t = 1 << (t.bit_length() - 1)
    while dim % t:
        t //= 2
    return t


def _params(semantics):
    return pltpu.CompilerParams(dimension_semantics=semantics, vmem_limit_bytes=VMEM_LIMIT_BYTES)


def _lanes(x, width):
    return jnp.concatenate([x] * (width // LANES), axis=1)


def _rms(x, g):
    return x * lax.rsqrt(jnp.mean(x * x, axis=-1, keepdims=True) + NORM_EPS) * g


def _norm_kernel(x_ref, g_ref, a_ref):
    a_ref[...] = _rms(x_ref[...], g_ref[...]).astype(a_ref.dtype)


def rms_norm_cast(x, g):
    m, d = x.shape
    tm = _tile(m, 256)
    return pl.pallas_call(
        _norm_kernel,
        out_shape=jax.ShapeDtypeStruct((m, d), BF16),
        grid=(m // tm,),
        in_specs=[pl.BlockSpec((tm, d), lambda i: (i, 0)), pl.BlockSpec((1, d), lambda i: (0, 0))],
        out_specs=pl.BlockSpec((tm, d), lambda i: (i, 0)),
        compiler_params=_params(("parallel",)),
        name="rms_norm_cast",
    )(x, g.reshape(1, d))


def _residual_kernel(h_ref, m_ref, gpost_ref, *rest, with_next):
    h = h_ref[...] + _rms(m_ref[...].astype(F32), gpost_ref[...])
    if with_next:
        gpre_ref, h_out_ref, a_ref = rest
        a_ref[...] = _rms(h, gpre_ref[...]).astype(a_ref.dtype)
    else:
        (h_out_ref,) = rest
    h_out_ref[...] = h


def residual_norm(h, m, g_post, g_pre_next=None):
    rows, d = h.shape
    tm = _tile(rows, 256)
    with_next = g_pre_next is not None
    row_spec = pl.BlockSpec((tm, d), lambda i: (i, 0))
    g_spec = pl.BlockSpec((1, d), lambda i: (0, 0))
    args = [h, m, g_post.reshape(1, d)]
    in_specs = [row_spec, row_spec, g_spec]
    out_shape = [jax.ShapeDtypeStruct((rows, d), F32)]
    out_specs = [row_spec]
    if with_next:
        args.append(g_pre_next.reshape(1, d))
        in_specs.append(g_spec)
        out_shape.append(jax.ShapeDtypeStruct((rows, d), BF16))
        out_specs.append(row_spec)
    out = pl.pallas_call(
        functools.partial(_residual_kernel, with_next=with_next),
        out_shape=out_shape,
        grid=(rows // tm,),
        in_specs=in_specs,
        out_specs=out_specs,
        compiler_params=_params(("parallel",)),
        name="residual_norm",
    )(*args)
    return (out[0], out[1]) if with_next else (out[0], None)


CAST_BLOCK_BYTES = 8 * 2**20


def _cast_kernel(w_ref, o_ref):
    n = w_ref.shape[1]
    o_ref[:, :n] = w_ref[...].astype(o_ref.dtype)
    if o_ref.shape[1] > n:
        o_ref[:, n:] = jnp.zeros((o_ref.shape[0], o_ref.shape[1] - n), o_ref.dtype)


def cast_weight(w, n_pad=None):
    kdim, n = w.shape
    n_pad = n if n_pad is None else n_pad
    tk = _tile(kdim, max(8, CAST_BLOCK_BYTES // (4 * n)))
    return pl.pallas_call(
        _cast_kernel,
        out_shape=jax.ShapeDtypeStruct((kdim, n_pad), BF16),
        grid=(kdim // tk,),
        in_specs=[pl.BlockSpec((tk, n), lambda i: (i, 0))],
        out_specs=pl.BlockSpec((tk, n_pad), lambda i: (i, 0)),
        compiler_params=_params(("parallel",)),
        name="cast_weight",
    )(w)


def _rope(x, cos2, sin2):
    return x * cos2 + pltpu.roll(x, HEAD_DIM // 2, 1) * sin2


def _matmul_kernel(a_ref, w_ref, *rest, nk, act, rope_pred):
    if rope_pred is not None:
        cos_ref, sin_ref, *rest = rest
    o_ref, *scratch = rest

    def finish(acc):
        if act == "relu2":
            r = jnp.maximum(acc, 0.0)
            acc = r * r
        if rope_pred is None:
            o_ref[...] = acc.astype(o_ref.dtype)
            return
        do_rope = rope_pred(pl.program_id(1))

        @pl.when(do_rope)
        def _():
            cos2 = cos_ref[...]
            sin2 = sin_ref[...]
            for c in range(o_ref.shape[1] // HEAD_DIM):
                sl = slice(c * HEAD_DIM, (c + 1) * HEAD_DIM)
                o_ref[:, sl] = _rope(acc[:, sl], cos2, sin2).astype(o_ref.dtype)

        @pl.when(jnp.logical_not(do_rope))
        def _():
            o_ref[...] = acc.astype(o_ref.dtype)

    prod = jnp.dot(a_ref[...], w_ref[...], preferred_element_type=F32)
    if nk == 1:
        finish(prod)
        return
    (acc_ref,) = scratch
    k = pl.program_id(2)

    @pl.when(k == 0)
    def _():
        acc_ref[...] = prod

    @pl.when(jnp.logical_and(k > 0, k < nk - 1))
    def _():
        acc_ref[...] += prod

    @pl.when(k == nk - 1)
    def _():
        finish(acc_ref[...] + prod)


def matmul(a, w, *, cols=None, out_dtype=BF16, act=None, rope=None, tm=1024, tn=1024, tk=4096):
    m, kdim = a.shape
    col0, n = (0, w.shape[1]) if cols is None else cols
    tm, tn, tk = _tile(m, tm), _tile(n, tn), _tile(kdim, tk)
    assert col0 % tn == 0
    j0 = col0 // tn
    nk = kdim // tk
    in_specs = [pl.BlockSpec((tm, tk), lambda i, j, k: (i, k)),
                pl.BlockSpec((tk, tn), lambda i, j, k: (k, j + j0))]
    args = [a, w]
    rope_pred = None
    if rope is not None:
        cos2, sin2, rope_pred = rope
        tab_spec = pl.BlockSpec((tm, HEAD_DIM), lambda i, j, k: (i, 0))
        in_specs += [tab_spec, tab_spec]
        args += [cos2, sin2]
    scratch = [pltpu.VMEM((tm, tn), F32)] if nk > 1 else []
    return pl.pallas_call(
        functools.partial(_matmul_kernel, nk=nk, act=act, rope_pred=rope_pred),
        out_shape=jax.ShapeDtypeStruct((m, n), out_dtype),
        grid=(m // tm, n // tn, nk),
        in_specs=in_specs,
        out_specs=pl.BlockSpec((tm, tn), lambda i, j, k: (i, j)),
        scratch_shapes=scratch,
        compiler_params=_params(("parallel", "parallel", "arbitrary")),
        name="matmul",
    )(*args)


A_PERIOD = 16


def _to_residue_major(x, batch, seq):
    u = seq // A_PERIOD
    return x.reshape(batch, u, 4, 4, -1).transpose(0, 3, 2, 1, 4).reshape(batch * seq, -1)


def _from_residue_major(x, batch, seq):
    u = seq // A_PERIOD
    return x.reshape(batch, 4, 4, u, -1).transpose(0, 3, 2, 1, 4).reshape(batch * seq, -1)


def _dilated_kernel(q_ref, kp_ref, kc_ref, vp_ref, vc_ref, o_ref, lse_ref, *, w, rows, local_index, blk_axis):
    has_prev = pl.program_id(blk_axis) > 0
    qi = local_index(lax.broadcasted_iota(jnp.int32, (rows, rows), 0))
    kj = local_index(lax.broadcasted_iota(jnp.int32, (rows, rows), 1))
    mask_prev = jnp.logical_and(qi + rows - kj <= w, has_prev)
    mask_cur = jnp.logical_and(kj <= qi, qi - kj <= w)
    scale = HEAD_DIM ** -0.5
    lead = (slice(None),) * (len(q_ref.shape) - 1)
    out_shape = o_ref.shape[:-1] + (HEAD_DIM,)
    for h in range(A_HEADS_PER_GROUP):
        idx = lead + (slice(h * HEAD_DIM, (h + 1) * HEAD_DIM),)
        load = lambda ref: ref[idx].reshape(rows, HEAD_DIM)
        q = load(q_ref)
        s_prev = lax.dot_general(q, load(kp_ref), NT_DIMS, preferred_element_type=F32) * scale
        s_cur = lax.dot_general(q, load(kc_ref), NT_DIMS, preferred_element_type=F32) * scale
        s_prev = jnp.where(mask_prev, s_prev, MASK_VALUE)
        s_cur = jnp.where(mask_cur, s_cur, MASK_VALUE)
        m = jnp.maximum(jnp.max(s_prev, axis=1, keepdims=True), jnp.max(s_cur, axis=1, keepdims=True))
        p_prev = jnp.exp(s_prev - m)
        p_cur = jnp.exp(s_cur - m)
        l = jnp.sum(p_prev, axis=1, keepdims=True) + jnp.sum(p_cur, axis=1, keepdims=True)
        o = jnp.dot(p_prev.astype(BF16), load(vp_ref), preferred_element_type=F32)
        o += jnp.dot(p_cur.astype(BF16), load(vc_ref), preferred_element_type=F32)
        o_ref[idx] = (o / l).astype(o_ref.dtype).reshape(out_shape)
        lse_ref[idx] = jnp.broadcast_to(m + jnp.log(l), (rows, HEAD_DIM)).reshape(out_shape)


def dilated_attention(qkv, batch, seq, group):
    window, r = DILATED_GROUPS[group]
    w = window // r
    u = seq // A_PERIOD
    width = qkv.shape[1]
    view = qkv.reshape(batch, 4, 4, u, width)
    if r == 16:
        rows, n_blk = A_BLOCK, u // A_BLOCK
        block = (None, None, None, rows, A_GROUP_WIDTH)
        grid = (batch, 4, 4, n_blk)
        place = lambda g, blk: (g[0], g[1], g[2], blk)
        local_index = lambda rho: rho
    elif r == 4:
        rows, n_blk = A_BLOCK, u // (A_BLOCK // 4)
        block = (None, None, 4, rows // 4, A_GROUP_WIDTH)
        grid = (batch, 4, n_blk)
        place = lambda g, blk: (g[0], g[1], 0, blk)
        local_index = lambda rho: 4 * (rho % (rows // 4)) + rho // (rows // 4)
    else:
        assert r == 1
        rows, n_blk = 2 * A_BLOCK, u // (2 * A_BLOCK // A_PERIOD)
        per = rows // A_PERIOD
        block = (None, 4, 4, per, A_GROUP_WIDTH)
        grid = (batch, n_blk)
        place = lambda g, blk: (g[0], 0, 0, blk)
        local_index = lambda rho: A_PERIOD * (rho % per) + 4 * ((rho // per) % 4) + rho // (4 * per)
    assert w <= rows and n_blk * rows * r == seq

    def spec(which, prev):
        def index(*g):
            blk = jnp.maximum(g[-1] - 1, 0) if prev else g[-1]
            return place(g, blk) + (group * 3 + which,)
        return pl.BlockSpec(block, index)

    out_spec = pl.BlockSpec(block, lambda *g: place(g, g[-1]) + (0,))
    o, lse = pl.pallas_call(
        functools.partial(_dilated_kernel, w=w, rows=rows, local_index=local_index, blk_axis=len(grid) - 1),
        out_shape=[jax.ShapeDtypeStruct((batch, 4, 4, u, A_GROUP_WIDTH), BF16),
                   jax.ShapeDtypeStruct((batch, 4, 4, u, A_GROUP_WIDTH), F32)],
        grid=grid,
        in_specs=[spec(0, False), spec(1, True), spec(1, False), spec(2, True), spec(2, False)],
        out_specs=[out_spec, out_spec],
        compiler_params=_params(("parallel",) * (len(grid) - 1) + ("arbitrary",)),
        name="dilated_attention",
    )(view, view, view, view, view)
    return o.reshape(batch * seq, A_GROUP_WIDTH), lse.reshape(batch * seq, A_GROUP_WIDTH)


def _merge_kernel(o0_ref, o1_ref, o2_ref, l0_ref, l1_ref, l2_ref, out_ref):
    l0, l1, l2 = l0_ref[...], l1_ref[...], l2_ref[...]
    m = jnp.maximum(jnp.maximum(l0, l1), l2)
    e0, e1, e2 = jnp.exp(l0 - m), jnp.exp(l1 - m), jnp.exp(l2 - m)
    num = e0 * o0_ref[...].astype(F32) + e1 * o1_ref[...].astype(F32) + e2 * o2_ref[...].astype(F32)
    out_ref[...] = (num / (e0 + e1 + e2)).astype(out_ref.dtype)


def merge_groups(outs, lses):
    rows, width = outs[0].shape
    tm = _tile(rows, 256)
    spec = pl.BlockSpec((tm, width), lambda i: (i, 0))
    return pl.pallas_call(
        _merge_kernel,
        out_shape=jax.ShapeDtypeStruct((rows, width), BF16),
        grid=(rows // tm,),
        in_specs=[spec] * 6,
        out_specs=spec,
        compiler_params=_params(("parallel",)),
        name="merge_groups",
    )(*outs, *lses)


def _indexer_prep_kernel(lat_ref, gq_ref, lnw_ref, lnb_ref, cos_ref, sin_ref, iqn_ref, ik_ref, iw_ref):
    iqn_ref[...] = _rms(lat_ref[:, :IDX_Q_RANK], gq_ref[...]).astype(iqn_ref.dtype)
    k = lat_ref[:, IDX_Q_RANK:IDX_Q_RANK + IDX_DIM]
    kc = k - jnp.mean(k, axis=-1, keepdims=True)
    kn = kc * lax.rsqrt(jnp.mean(kc * kc, axis=-1, keepdims=True) + NORM_EPS) * lnw_ref[...] + lnb_ref[...]
    ik_ref[...] = _rope(kn, cos_ref[...], sin_ref[...]).astype(ik_ref.dtype)
    iw_ref[...] = lat_ref[:, IDX_Q_RANK + IDX_DIM:] * (IDX_HEADS ** -0.5 * IDX_DIM ** -0.5)


def indexer_prep(lat, gq, lnw, lnb, cos2, sin2):
    rows, width = lat.shape
    tm = _tile(rows, 512)
    row = lambda wd: pl.BlockSpec((tm, wd), lambda i: (i, 0))
    vec = lambda wd: pl.BlockSpec((1, wd), lambda i: (0, 0))
    return pl.pallas_call(
        _indexer_prep_kernel,
        out_shape=[jax.ShapeDtypeStruct((rows, IDX_Q_RANK), BF16),
                   jax.ShapeDtypeStruct((rows, IDX_DIM), BF16),
                   jax.ShapeDtypeStruct((rows, LANES), F32)],
        grid=(rows // tm,),
        in_specs=[row(width), vec(IDX_Q_RANK), vec(IDX_DIM), vec(IDX_DIM), row(HEAD_DIM), row(HEAD_DIM)],
        out_specs=[row(IDX_Q_RANK), row(IDX_DIM), row(LANES)],
        compiler_params=_params(("parallel",)),
        name="indexer_prep",
    )(lat, gq.reshape(1, -1), lnw.reshape(1, -1), lnb.reshape(1, -1), cos2, sin2)


IDX_TQ = 128
IDX_TS = 512
IDX_HEAD_BATCH = 8


def _indexer_kernel(iq_ref, ik_ref, iw_ref, bias_ref, key_ref, qs_ref, wb_ref, *, n_sel):
    tq, seq = key_ref.shape
    i = pl.program_id(1)
    n_chunks = ((i + 1) * tq + IDX_TS - 1) // IDX_TS
    key_ref[...] = jnp.full((tq, seq), INT_MIN, jnp.int32)
    iw = iw_ref[...]
    for h in range(IDX_HEADS):
        qs_ref[h * tq:(h + 1) * tq, :] = iq_ref[:, h * IDX_DIM:(h + 1) * IDX_DIM]
        wb_ref[h] = jnp.broadcast_to(iw[:, h:h + 1], (tq, LANES))
    t_pos = i * tq + lax.broadcasted_iota(jnp.int32, (tq, IDX_TS), 0)
    lane = lax.broadcasted_iota(jnp.int32, (tq, IDX_TS), 1)
    hb_rows = IDX_HEAD_BATCH * tq

    def score_chunk(c, carry):
        start = pl.multiple_of(c * IDX_TS, IDX_TS)
        keys = ik_ref[pl.ds(start, IDX_TS), :]
        acc = jnp.zeros((tq, IDX_TS), F32)
        for hb in range(IDX_HEADS // IDX_HEAD_BATCH):
            rel = lax.dot_general(qs_ref[hb * hb_rows:(hb + 1) * hb_rows, :], keys, NT_DIMS,
                                  preferred_element_type=F32)
            for hh in range(IDX_HEAD_BATCH):
                wb = _lanes(wb_ref[hb * IDX_HEAD_BATCH + hh], IDX_TS)
                acc += jnp.maximum(rel[hh * tq:(hh + 1) * tq], 0.0) * wb
        bits = lax.bitcast_convert_type(acc, jnp.int32)
        ordered = jnp.where(bits >= 0, bits, bits ^ jnp.int32(0x7FFFFFFF))
        key_ref[:, pl.ds(start, IDX_TS)] = jnp.where(start + lane <= t_pos, ordered, INT_MIN)
        return carry

    lax.fori_loop(0, n_chunks, score_chunk, 0)

    def count_ge(cand):
        cand_w = _lanes(cand, IDX_TS)

        def body(c, cnt):
            start = pl.multiple_of(c * IDX_TS, IDX_TS)
            ge = jnp.where(key_ref[:, pl.ds(start, IDX_TS)] >= cand_w, 1.0, 0.0)
            for b in range(IDX_TS // LANES):
                cnt = cnt + ge[:, b * LANES:(b + 1) * LANES]
            return cnt

        cnt = lax.fori_loop(0, n_chunks, body, jnp.zeros((tq, LANES), F32))
        return jnp.sum(cnt, axis=1, keepdims=True)

    thr0 = jnp.full((tq, LANES), INT_MIN + 1, jnp.int32)
    open0 = jnp.where(count_ge(thr0) > n_sel, 1.0, 0.0)

    def bit_cond(state):
        step, _, n_open = state
        return jnp.logical_and(step < 32, n_open > 0.0)

    def bit_step(state):
        step, (thr, still_open), _ = state
        base = jnp.where(step == 0, INT_MIN, thr)
        cand = base + lax.shift_left(jnp.int32(1), 31 - step)
        cnt = count_ge(cand)
        take = jnp.logical_and(cnt >= n_sel, still_open > 0.0)
        thr = jnp.where(take, cand, thr)
        still_open = jnp.where(cnt == n_sel, 0.0, still_open)
        return step + 1, (thr, still_open), jnp.max(still_open)

    _, (thr, _), _ = lax.while_loop(bit_cond, bit_step, (jnp.int32(0), (thr0, open0), jnp.max(open0)))
    thr_w = _lanes(thr, IDX_TS)
    for c in range(seq // IDX_TS):
        sl = slice(c * IDX_TS, (c + 1) * IDX_TS)
        bias_ref[:, sl] = jnp.where(key_ref[:, sl] >= thr_w, 0.0, MASK_VALUE).astype(bias_ref.dtype)


def indexer_mask(iq, ik, iw, batch, seq):
    n_sel = min(TOPK_MAX, seq // 4)
    tq = IDX_TQ
    assert seq % IDX_TS == 0 and seq % tq == 0
    return pl.pallas_call(
        functools.partial(_indexer_kernel, n_sel=n_sel),
        out_shape=jax.ShapeDtypeStruct((batch, seq, seq), BF16),
        grid=(batch, seq // tq),
        in_specs=[pl.BlockSpec((None, tq, IDX_HEADS * IDX_DIM), lambda b, i: (b, i, 0)),
                  pl.BlockSpec((None, seq, IDX_DIM), lambda b, i: (b, 0, 0)),
                  pl.BlockSpec((None, tq, LANES), lambda b, i: (b, i, 0))],
        out_specs=pl.BlockSpec((None, tq, seq), lambda b, i: (b, i, 0)),
        scratch_shapes=[pltpu.VMEM((tq, seq), jnp.int32),
                        pltpu.VMEM((IDX_HEADS * tq, IDX_DIM), BF16),
                        pltpu.VMEM((IDX_HEADS, tq, LANES), F32)],
        compiler_params=_params(("parallel", "arbitrary")),
        name="indexer_mask",
    )(iq.reshape(batch, seq, -1), ik.reshape(batch, seq, -1), iw.reshape(batch, seq, -1))


B_TQ = 256
B_TS = 512
B_REP = B_HEADS // B_KV_HEADS
B_CHUNK = 128
B_UNIT = 512


def _sparse_attn_kernel(qblk_ref, kblk_ref, q_ref, k_ref, v_ref, bias_ref, o_ref,
                        qs_ref, vo_ref, biasf_ref, s_ref, p_ref, alpha_ref, m_ref, accl_ref):
    tq, ts = bias_ref.shape
    rows = B_REP * tq
    step = pl.program_id(1)
    i, j = qblk_ref[step], kblk_ref[step]
    last = ((i + 1) * tq - 1) // ts

    @pl.when(j == 0)
    def _():
        scale = HEAD_DIM ** -0.5 * LOG2E
        for g in range(B_KV_HEADS):
            for r in range(B_REP):
                h = g * B_REP + r
                qh = q_ref[:, h * HEAD_DIM:(h + 1) * HEAD_DIM].astype(F32) * scale
                qs_ref[g, r * tq:(r + 1) * tq, :] = qh.astype(qs_ref.dtype)
        m_ref[...] = jnp.full(m_ref.shape, MASK_VALUE, F32)
        accl_ref[...] = jnp.zeros(accl_ref.shape, F32)

    biasf_ref[...] = bias_ref[...].astype(F32)
    ones = jnp.ones((ts, LANES), BF16)
    for g in range(B_KV_HEADS):
        vo_ref[g, :, :HEAD_DIM] = v_ref[:, g * HEAD_DIM:(g + 1) * HEAD_DIM]
        vo_ref[g, :, HEAD_DIM:] = ones

    unit_rows = min(B_UNIT, rows)
    units = [(g, r0) for g in range(B_KV_HEADS) for r0 in range(0, rows, unit_rows)]

    def logits(u):
        g, r0 = units[u]
        kg = k_ref[:, g * HEAD_DIM:(g + 1) * HEAD_DIM]
        s_ref[u % 2] = lax.dot_general(qs_ref[g, r0:r0 + unit_rows, :], kg, NT_DIMS,
                                       preferred_element_type=F32)

    logits(0)
    for u, (g, r0) in enumerate(units):
        if u + 1 < len(units):
            logits(u + 1)
        par = u % 2
        for c in range(0, unit_rows, B_CHUNK):
            rs = slice(c, c + B_CHUNK)
            gs = slice(r0 + c, r0 + c + B_CHUNK)
            qrow = (r0 + c) % tq
            s = s_ref[par, rs, :] + biasf_ref[qrow:qrow + B_CHUNK, :]
            m_prev = m_ref[g, gs, :]
            m_new = jnp.maximum(m_prev, jnp.max(s, axis=1, keepdims=True))
            p_ref[par, rs, :] = jnp.exp2(s - _lanes(m_new, ts)).astype(BF16)
            alpha_ref[par, rs, :] = jnp.exp2(m_prev - m_new)
            m_ref[g, gs, :] = m_new
        pv = jnp.dot(p_ref[par], vo_ref[g], preferred_element_type=F32)
        alpha = alpha_ref[par]
        us = slice(r0, r0 + unit_rows)
        accl_ref[g, us, :] = jnp.concatenate([alpha, alpha], axis=1) * accl_ref[g, us, :] + pv

    @pl.when(j == last)
    def _():
        for g in range(B_KV_HEADS):
            out = accl_ref[g, :, :HEAD_DIM] / accl_ref[g, :, HEAD_DIM:]
            for r in range(B_REP):
                h = g * B_REP + r
                o_ref[:, h * HEAD_DIM:(h + 1) * HEAD_DIM] = out[r * tq:(r + 1) * tq].astype(o_ref.dtype)


def sparse_attention(qkv, bias, batch, seq):
    tq, ts = _tile(seq, B_TQ), _tile(seq, B_TS)
    view = qkv.reshape(batch, seq, B_QKV_WIDTH)
    k_tile = B_Q_WIDTH // B_KV_WIDTH
    pairs = [(i, j) for i in range(seq // tq) for j in range(((i + 1) * tq - 1) // ts + 1)]
    qblk = jnp.asarray([p[0] for p in pairs], jnp.int32)
    kblk = jnp.asarray([p[1] for p in pairs], jnp.int32)
    rows = B_REP * tq
    unit_rows = min(B_UNIT, rows)
    grid_spec = pltpu.PrefetchScalarGridSpec(
        num_scalar_prefetch=2,
        grid=(batch, len(pairs)),
        in_specs=[pl.BlockSpec((None, tq, B_Q_WIDTH), lambda b, s, qb, kb: (b, qb[s], 0)),
                  pl.BlockSpec((None, ts, B_KV_WIDTH), lambda b, s, qb, kb: (b, kb[s], k_tile)),
                  pl.BlockSpec((None, ts, B_KV_WIDTH), lambda b, s, qb, kb: (b, kb[s], k_tile + 1)),
                  pl.BlockSpec((None, tq, ts), lambda b, s, qb, kb: (b, qb[s], kb[s]))],
        out_specs=pl.BlockSpec((None, tq, B_Q_WIDTH), lambda b, s, qb, kb: (b, qb[s], 0)),
        scratch_shapes=[pltpu.VMEM((B_KV_HEADS, rows, HEAD_DIM), BF16),
                        pltpu.VMEM((B_KV_HEADS, ts, 2 * HEAD_DIM), BF16),
                        pltpu.VMEM((tq, ts), F32),
                        pltpu.VMEM((2, unit_rows, ts), F32),
                        pltpu.VMEM((2, unit_rows, ts), BF16),
                        pltpu.VMEM((2, unit_rows, LANES), F32),
                        pltpu.VMEM((B_KV_HEADS, rows, LANES), F32),
                        pltpu.VMEM((B_KV_HEADS, rows, 2 * HEAD_DIM), F32)])
    out = pl.pallas_call(
        _sparse_attn_kernel,
        out_shape=jax.ShapeDtypeStruct((batch, seq, B_Q_WIDTH), BF16),
        grid_spec=grid_spec,
        compiler_params=_params(("parallel", "arbitrary")),
        name="sparse_attention",
    )(qblk, kblk, view, view, view, bias)
    return out.reshape(batch * seq, B_Q_WIDTH)


def _rope_tables(positions):
    inv_freq = ROPE_THETA ** (-jnp.arange(0, HEAD_DIM, 2, dtype=F32) / HEAD_DIM)
    ang = positions.astype(F32).reshape(-1, 1) * inv_freq
    cos, sin = jnp.cos(ang), jnp.sin(ang)
    return jnp.concatenate([cos, cos], axis=-1), jnp.concatenate([-sin, sin], axis=-1)


def _mlp(f_in, w_up, w_down):
    u = matmul(f_in, cast_weight(w_up), act="relu2")
    return matmul(u, cast_weight(w_down))


def _mixer_dilated(a, cos2, sin2, w_in, w_out, batch, seq):
    tn = 1024
    per_part = A_GROUP_WIDTH // tn
    rope_pred = lambda j: (j // per_part) % 3 != 2
    a, cos2, sin2 = (_to_residue_major(t, batch, seq) for t in (a, cos2, sin2))
    qkv = matmul(a, cast_weight(w_in), rope=(cos2, sin2, rope_pred), tn=tn)
    outs, lses = zip(*[dilated_attention(qkv, batch, seq, g) for g in range(N_GROUPS)])
    o = _from_residue_major(merge_groups(outs, lses), batch, seq)
    return matmul(o, cast_weight(w_out))


def _mixer_sparse(a, cos2, sin2, w_in, idx_q_norm, w_idx_up, idx_k_w, idx_k_b, w_out, batch, seq):
    tn = 1024
    n_rope = (B_Q_WIDTH + B_KV_WIDTH) // tn
    idx_width = IDX_Q_RANK + IDX_DIM + LANES
    w_all = cast_weight(w_in, n_pad=B_QKV_WIDTH + idx_width)
    qkv = matmul(a, w_all, cols=(0, B_QKV_WIDTH), rope=(cos2, sin2, lambda j: j < n_rope), tn=tn)
    lat = matmul(a, w_all, cols=(B_QKV_WIDTH, idx_width), out_dtype=F32, tn=256)
    iq_n, ik, iw = indexer_prep(lat, idx_q_norm, idx_k_w, idx_k_b, cos2, sin2)
    iq = matmul(iq_n, cast_weight(w_idx_up), rope=(cos2, sin2, lambda j: j >= 0), tn=tn)
    bias = indexer_mask(iq, ik, iw, batch, seq)
    o = sparse_attention(qkv, bias, batch, seq)
    return matmul(o, cast_weight(w_out))


def kernel(x, positions, attn_pre_norm, attn_post_norm, mlp_pre_norm, mlp_post_norm, w_in_a, w_out_a,
           w_in_b, idx_q_norm, w_idx_up, idx_k_norm_w, idx_k_norm_b, w_out_b, w_up, w_down):
    batch, seq, d_model = x.shape
    depth = attn_pre_norm.shape[0]
    cos2, sin2 = _rope_tables(positions)
    h = x.reshape(batch * seq, d_model)
    a = rms_norm_cast(h, attn_pre_norm[0])
    for i in range(depth):
        j = i // 2
        if i % 2 == 0:
            m = _mixer_dilated(a, cos2, sin2, w_in_a[j], w_out_a[j], batch, seq)
        else:
            m = _mixer_sparse(a, cos2, sin2, w_in_b[j], idx_q_norm[j], w_idx_up[j],
                              idx_k_norm_w[j], idx_k_norm_b[j], w_out_b[j], batch, seq)
        h, f_in = residual_norm(h, m, attn_post_norm[i], mlp_pre_norm[i])
        f = _mlp(f_in, w_up[i], w_down[i])
        h, a = residual_norm(h, f, mlp_post_norm[i], attn_pre_norm[i + 1] if i + 1 < depth else None)
    return h.reshape(batch, seq, d_model)
```

```python
import functools
import math

import jax
import jax.numpy as jnp
from jax import lax
from jax.experimental import pallas as pl
from jax.experimental.pallas import tpu as pltpu

HEAD_DIM = 128
ROPE_THETA = 10000.0
NORM_EPS = 1e-6
DILATED_GROUPS = ((128, 1), (512, 4), (2048, 16))
N_GROUPS = 3
A_HEADS_PER_GROUP = 16
A_BLOCK = 128
A_GROUP_WIDTH = A_HEADS_PER_GROUP * HEAD_DIM
B_HEADS = 32
B_KV_HEADS = 8
IDX_HEADS = 32
IDX_DIM = 128
IDX_Q_RANK = 1024
TOPK_MAX = 256
B_Q_WIDTH = B_HEADS * HEAD_DIM
B_KV_WIDTH = B_KV_HEADS * HEAD_DIM
B_QKV_WIDTH = B_Q_WIDTH + 2 * B_KV_WIDTH

LANES = 128
VMEM_LIMIT_BYTES = 56 * 2**20
MASK_VALUE = -1e30
INT_MIN = -2**31
LOG2E = math.log2(math.e)
NT_DIMS = (((1,), (1,)), ((), ()))

F32 = jnp.float32
BF16 = jnp.bfloat16


def _tile(dim, pref):
    t = min(dim, pref)
    if dim % t:
        t = 1 << (t.bit_length() - 1)
    while dim % t:
        t //= 2
    return t


def _params(semantics):
    return pltpu.CompilerParams(dimension_semantics=semantics, vmem_limit_bytes=VMEM_LIMIT_BYTES)


def _lanes(x, width):
    return jnp.concatenate([x] * (width // LANES), axis=1)


def _rms(x, g):
    return x * lax.rsqrt(jnp.mean(x * x, axis=-1, keepdims=True) + NORM_EPS) * g


def _norm_kernel(x_ref, g_ref, a_ref):
    a_ref[...] = _rms(x_ref[...], g_ref[...]).astype(a_ref.dtype)


def rms_norm_cast(x, g):
    m, d = x.shape
    tm = _tile(m, 256)
    return pl.pallas_call(
        _norm_kernel,
        out_shape=jax.ShapeDtypeStruct((m, d), BF16),
        grid=(m // tm,),
        in_specs=[pl.BlockSpec((tm, d), lambda i: (i, 0)), pl.BlockSpec((1, d), lambda i: (0, 0))],
        out_specs=pl.BlockSpec((tm, d), lambda i: (i, 0)),
        compiler_params=_params(("parallel",)),
        name="rms_norm_cast",
    )(x, g.reshape(1, d))


def _residual_kernel(h_ref, m_ref, gpost_ref, *rest, with_next):
    h = h_ref[...] + _rms(m_ref[...].astype(F32), gpost_ref[...])
    if with_next:
        gpre_ref, h_out_ref, a_ref = rest
        a_ref[...] = _rms(h, gpre_ref[...]).astype(a_ref.dtype)
    else:
        (h_out_ref,) = rest
    h_out_ref[...] = h


def residual_norm(h, m, g_post, g_pre_next=None):
    rows, d = h.shape
    tm = _tile(rows, 256)
    with_next = g_pre_next is not None
    row_spec = pl.BlockSpec((tm, d), lambda i: (i, 0))
    g_spec = pl.BlockSpec((1, d), lambda i: (0, 0))
    args = [h, m, g_post.reshape(1, d)]
    in_specs = [row_spec, row_spec, g_spec]
    out_shape = [jax.ShapeDtypeStruct((rows, d), F32)]
    out_specs = [row_spec]
    if with_next:
        args.append(g_pre_next.reshape(1, d))
        in_specs.append(g_spec)
        out_shape.append(jax.ShapeDtypeStruct((rows, d), BF16))
        out_specs.append(row_spec)
    out = pl.pallas_call(
        functools.partial(_residual_kernel, with_next=with_next),
        out_shape=out_shape,
        grid=(rows // tm,),
        in_specs=in_specs,
        out_specs=out_specs,
        compiler_params=_params(("parallel",)),
        name="residual_norm",
    )(*args)
    return (out[0], out[1]) if with_next else (out[0], None)


CAST_BLOCK_BYTES = 8 * 2**20


def _cast_kernel(w_ref, o_ref):
    n = w_ref.shape[1]
    o_ref[:, :n] = w_ref[...].astype(o_ref.dtype)
    if o_ref.shape[1] > n:
        o_ref[:, n:] = jnp.zeros((o_ref.shape[0], o_ref.shape[1] - n), o_ref.dtype)


def cast_weight(w, layer, n_pad=None):
    _, kdim, n = w.shape
    n_pad = n if n_pad is None else n_pad
    tk = _tile(kdim, max(8, CAST_BLOCK_BYTES // (4 * n)))
    return pl.pallas_call(
        _cast_kernel,
        out_shape=jax.ShapeDtypeStruct((kdim, n_pad), BF16),
        grid=(kdim // tk,),
        in_specs=[pl.BlockSpec((None, tk, n), lambda i: (layer, i, 0))],
        out_specs=pl.BlockSpec((tk, n_pad), lambda i: (i, 0)),
        compiler_params=_params(("parallel",)),
        name="cast_weight",
    )(w)


def _rope(x, cos2, sin2):
    return x * cos2 + pltpu.roll(x, HEAD_DIM // 2, 1) * sin2


def _matmul_kernel(a_ref, w_ref, *rest, nk, act, rope_pred):
    if rope_pred is not None:
        cos_ref, sin_ref, *rest = rest
    o_ref, *scratch = rest

    def finish(acc):
        if act == "relu2":
            r = jnp.maximum(acc, 0.0)
            acc = r * r
        if rope_pred is None:
            o_ref[...] = acc.astype(o_ref.dtype)
            return
        do_rope = rope_pred(pl.program_id(1))

        @pl.when(do_rope)
        def _():
            cos2 = cos_ref[...]
            sin2 = sin_ref[...]
            for c in range(o_ref.shape[1] // HEAD_DIM):
                sl = slice(c * HEAD_DIM, (c + 1) * HEAD_DIM)
                o_ref[:, sl] = _rope(acc[:, sl], cos2, sin2).astype(o_ref.dtype)

        @pl.when(jnp.logical_not(do_rope))
        def _():
            o_ref[...] = acc.astype(o_ref.dtype)

    prod = jnp.dot(a_ref[...], w_ref[...], preferred_element_type=F32)
    if nk == 1:
        finish(prod)
        return
    (acc_ref,) = scratch
    k = pl.program_id(2)

    @pl.when(k == 0)
    def _():
        acc_ref[...] = prod

    @pl.when(jnp.logical_and(k > 0, k < nk - 1))
    def _():
        acc_ref[...] += prod

    @pl.when(k == nk - 1)
    def _():
        finish(acc_ref[...] + prod)


def matmul(a, w, *, cols=None, out_dtype=BF16, act=None, rope=None, tm=1024, tn=1024, tk=4096):
    m, kdim = a.shape
    col0, n = (0, w.shape[1]) if cols is None else cols
    tm, tn, tk = _tile(m, tm), _tile(n, tn), _tile(kdim, tk)
    assert col0 % tn == 0
    j0 = col0 // tn
    nk = kdim // tk
    in_specs = [pl.BlockSpec((tm, tk), lambda i, j, k: (i, k)),
                pl.BlockSpec((tk, tn), lambda i, j, k: (k, j + j0))]
    args = [a, w]
    rope_pred = None
    if rope is not None:
        cos2, sin2, rope_pred = rope
        tab_spec = pl.BlockSpec((tm, HEAD_DIM), lambda i, j, k: (i, 0))
        in_specs += [tab_spec, tab_spec]
        args += [cos2, sin2]
    scratch = [pltpu.VMEM((tm, tn), F32)] if nk > 1 else []
    return pl.pallas_call(
        functools.partial(_matmul_kernel, nk=nk, act=act, rope_pred=rope_pred),
        out_shape=jax.ShapeDtypeStruct((m, n), out_dtype),
        grid=(m // tm, n // tn, nk),
        in_specs=in_specs,
        out_specs=pl.BlockSpec((tm, tn), lambda i, j, k: (i, j)),
        scratch_shapes=scratch,
        compiler_params=_params(("parallel", "parallel", "arbitrary")),
        name="matmul",
    )(*args)


A_PERIOD = 16


def _to_residue_major(x, batch, seq):
    u = seq // A_PERIOD
    return x.reshape(batch, u, 4, 4, -1).transpose(0, 3, 2, 1, 4).reshape(batch * seq, -1)


def _from_residue_major(x, batch, seq):
    u = seq // A_PERIOD
    return x.reshape(batch, 4, 4, u, -1).transpose(0, 3, 2, 1, 4).reshape(batch * seq, -1)


def _dilated_kernel(q_ref, kp_ref, kc_ref, vp_ref, vc_ref, o_ref, lse_ref, s_ref, vo_ref,
                    *, w, rows, local_index, blk_axis):
    has_prev = pl.program_id(blk_axis) > 0
    qi = local_index(lax.broadcasted_iota(jnp.int32, (rows, rows), 0))
    kj = local_index(lax.broadcasted_iota(jnp.int32, (rows, rows), 1))
    mask_prev = jnp.logical_and(qi + rows - kj <= w, has_prev)
    mask_cur = jnp.logical_and(kj <= qi, qi - kj <= w)
    scale = HEAD_DIM ** -0.5
    lead = (slice(None),) * (len(q_ref.shape) - 1)
    out_shape = o_ref.shape[:-1] + (HEAD_DIM,)
    for h in range(A_HEADS_PER_GROUP):
        idx = lead + (slice(h * HEAD_DIM, (h + 1) * HEAD_DIM),)
        load = lambda ref: ref[idx].reshape(rows, HEAD_DIM)
        q = load(q_ref)
        s_prev = lax.dot_general(q, load(kp_ref), NT_DIMS, preferred_element_type=F32) * scale
        s_cur = lax.dot_general(q, load(kc_ref), NT_DIMS, preferred_element_type=F32) * scale
        s_prev = jnp.where(mask_prev, s_prev, MASK_VALUE)
        s_cur = jnp.where(mask_cur, s_cur, MASK_VALUE)
        m = jnp.maximum(jnp.max(s_prev, axis=1, keepdims=True), jnp.max(s_cur, axis=1, keepdims=True))
        p_prev = jnp.exp(s_prev - m)
        p_cur = jnp.exp(s_cur - m)
        l = jnp.sum(p_prev, axis=1, keepdims=True) + jnp.sum(p_cur, axis=1, keepdims=True)
        o = jnp.dot(p_prev.astype(BF16), load(vp_ref), preferred_element_type=F32)
        o += jnp.dot(p_cur.astype(BF16), load(vc_ref), preferred_element_type=F32)
        o_ref[idx] = (o / l).astype(o_ref.dtype).reshape(out_shape)
        lse_ref[idx] = jnp.broadcast_to(m + jnp.log(l), (rows, HEAD_DIM)).reshape(out_shape)


def dilated_attention(qkv, batch, seq, group):
    window, r = DILATED_GROUPS[group]
    w = window // r
    u = seq // A_PERIOD
    width = qkv.shape[1]
    view = qkv.reshape(batch, 4, 4, u, width)
    if r == 16:
        rows, n_blk = A_BLOCK, u // A_BLOCK
        block = (None, None, None, rows, A_GROUP_WIDTH)
        grid = (batch, 4, 4, n_blk)
        place = lambda g, blk: (g[0], g[1], g[2], blk)
        local_index = lambda rho: rho
    elif r == 4:
        rows, n_blk = A_BLOCK, u // (A_BLOCK // 4)
        block = (None, None, 4, rows // 4, A_GROUP_WIDTH)
        grid = (batch, 4, n_blk)
        place = lambda g, blk: (g[0], g[1], 0, blk)
        local_index = lambda rho: 4 * (rho % (rows // 4)) + rho // (rows // 4)
    else:
        assert r == 1
        rows, n_blk = 2 * A_BLOCK, u // (2 * A_BLOCK // A_PERIOD)
        per = rows // A_PERIOD
        block = (None, 4, 4, per, A_GROUP_WIDTH)
        grid = (batch, n_blk)
        place = lambda g, blk: (g[0], 0, 0, blk)
        local_index = lambda rho: A_PERIOD * (rho % per) + 4 * ((rho // per) % 4) + rho // (4 * per)
    assert w <= rows and n_blk * rows * r == seq

    def spec(which, prev):
        def index(*g):
            blk = jnp.maximum(g[-1] - 1, 0) if prev else g[-1]
            return place(g, blk) + (group * 3 + which,)
        return pl.BlockSpec(block, index)

    out_spec = pl.BlockSpec(block, lambda *g: place(g, g[-1]) + (0,))
    o, lse = pl.pallas_call(
        functools.partial(_dilated_kernel, w=w, rows=rows, local_index=local_index, blk_axis=len(grid) - 1),
        out_shape=[jax.ShapeDtypeStruct((batch, 4, 4, u, A_GROUP_WIDTH), BF16),
                   jax.ShapeDtypeStruct((batch, 4, 4, u, A_GROUP_WIDTH), F32)],
        grid=grid,
        in_specs=[spec(0, False), spec(1, True), spec(1, False), spec(2, True), spec(2, False)],
        out_specs=[out_spec, out_spec],
        scratch_shapes=[pltpu.VMEM((2, rows, 2 * rows), F32),
                        pltpu.VMEM((2, 2 * rows, 2 * HEAD_DIM), BF16)],
        compiler_params=_params(("parallel",) * (len(grid) - 1) + ("arbitrary",)),
        name="dilated_attention",
    )(view, view, view, view, view)
    return o.reshape(batch * seq, A_GROUP_WIDTH), lse.reshape(batch * seq, A_GROUP_WIDTH)


def _merge_kernel(o0_ref, o1_ref, o2_ref, l0_ref, l1_ref, l2_ref, out_ref):
    l0, l1, l2 = l0_ref[...], l1_ref[...], l2_ref[...]
    m = jnp.maximum(jnp.maximum(l0, l1), l2)
    e0, e1, e2 = jnp.exp(l0 - m), jnp.exp(l1 - m), jnp.exp(l2 - m)
    num = e0 * o0_ref[...].astype(F32) + e1 * o1_ref[...].astype(F32) + e2 * o2_ref[...].astype(F32)
    out_ref[...] = (num / (e0 + e1 + e2)).astype(out_ref.dtype)


def merge_groups(outs, lses):
    rows, width = outs[0].shape
    tm = _tile(rows, 256)
    spec = pl.BlockSpec((tm, width), lambda i: (i, 0))
    return pl.pallas_call(
        _merge_kernel,
        out_shape=jax.ShapeDtypeStruct((rows, width), BF16),
        grid=(rows // tm,),
        in_specs=[spec] * 6,
        out_specs=spec,
        compiler_params=_params(("parallel",)),
        name="merge_groups",
    )(*outs, *lses)


def _indexer_prep_kernel(lat_ref, gq_ref, lnw_ref, lnb_ref, cos_ref, sin_ref, iqn_ref, ik_ref, iw_ref):
    iqn_ref[...] = _rms(lat_ref[:, :IDX_Q_RANK], gq_ref[...]).astype(iqn_ref.dtype)
    k = lat_ref[:, IDX_Q_RANK:IDX_Q_RANK + IDX_DIM]
    kc = k - jnp.mean(k, axis=-1, keepdims=True)
    kn = kc * lax.rsqrt(jnp.mean(kc * kc, axis=-1, keepdims=True) + NORM_EPS) * lnw_ref[...] + lnb_ref[...]
    ik_ref[...] = _rope(kn, cos_ref[...], sin_ref[...]).astype(ik_ref.dtype)
    iw_ref[...] = lat_ref[:, IDX_Q_RANK + IDX_DIM:] * (IDX_HEADS ** -0.5 * IDX_DIM ** -0.5)


def indexer_prep(lat, gq, lnw, lnb, cos2, sin2):
    rows, width = lat.shape
    tm = _tile(rows, 512)
    row = lambda wd: pl.BlockSpec((tm, wd), lambda i: (i, 0))
    vec = lambda wd: pl.BlockSpec((1, wd), lambda i: (0, 0))
    return pl.pallas_call(
        _indexer_prep_kernel,
        out_shape=[jax.ShapeDtypeStruct((rows, IDX_Q_RANK), BF16),
                   jax.ShapeDtypeStruct((rows, IDX_DIM), BF16),
                   jax.ShapeDtypeStruct((rows, LANES), F32)],
        grid=(rows // tm,),
        in_specs=[row(width), vec(IDX_Q_RANK), vec(IDX_DIM), vec(IDX_DIM), row(HEAD_DIM), row(HEAD_DIM)],
        out_specs=[row(IDX_Q_RANK), row(IDX_DIM), row(LANES)],
        compiler_params=_params(("parallel",)),
        name="indexer_prep",
    )(lat, gq.reshape(1, -1), lnw.reshape(1, -1), lnb.reshape(1, -1), cos2, sin2)


IDX_TQ = 128
IDX_TS = 512
IDX_HEAD_BATCH = 8


def _indexer_kernel(iq_ref, ik_ref, iw_ref, bias_ref, key_ref, qs_ref, wb_ref, *, n_sel):
    tq, seq = key_ref.shape
    i = pl.program_id(1)
    n_chunks = ((i + 1) * tq + IDX_TS - 1) // IDX_TS
    key_ref[...] = jnp.full((tq, seq), INT_MIN, jnp.int32)
    iw = iw_ref[...]
    for h in range(IDX_HEADS):
        qs_ref[h * tq:(h + 1) * tq, :] = iq_ref[:, h * IDX_DIM:(h + 1) * IDX_DIM]
        wb_ref[h] = jnp.broadcast_to(iw[:, h:h + 1], (tq, LANES))
    t_pos = i * tq + lax.broadcasted_iota(jnp.int32, (tq, IDX_TS), 0)
    lane = lax.broadcasted_iota(jnp.int32, (tq, IDX_TS), 1)
    hb_rows = IDX_HEAD_BATCH * tq

    def score_chunk(c, carry):
        start = pl.multiple_of(c * IDX_TS, IDX_TS)
        keys = ik_ref[pl.ds(start, IDX_TS), :]
        acc = jnp.zeros((tq, IDX_TS), F32)
        for hb in range(IDX_HEADS // IDX_HEAD_BATCH):
            rel = lax.dot_general(qs_ref[hb * hb_rows:(hb + 1) * hb_rows, :], keys, NT_DIMS,
                                  preferred_element_type=F32)
            for hh in range(IDX_HEAD_BATCH):
                wb = _lanes(wb_ref[hb * IDX_HEAD_BATCH + hh], IDX_TS)
                acc += jnp.maximum(rel[hh * tq:(hh + 1) * tq], 0.0) * wb
        bits = lax.bitcast_convert_type(acc, jnp.int32)
        ordered = jnp.where(bits >= 0, bits, bits ^ jnp.int32(0x7FFFFFFF))
        key_ref[:, pl.ds(start, IDX_TS)] = jnp.where(start + lane <= t_pos, ordered, INT_MIN)
        return carry

    lax.fori_loop(0, n_chunks, score_chunk, 0)

    def count_ge(cand):
        cand_w = _lanes(cand, IDX_TS)

        def body(c, cnt):
            start = pl.multiple_of(c * IDX_TS, IDX_TS)
            ge = jnp.where(key_ref[:, pl.ds(start, IDX_TS)] >= cand_w, 1.0, 0.0)
            for b in range(IDX_TS // LANES):
                cnt = cnt + ge[:, b * LANES:(b + 1) * LANES]
            return cnt

        cnt = lax.fori_loop(0, n_chunks, body, jnp.zeros((tq, LANES), F32))
        return jnp.sum(cnt, axis=1, keepdims=True)

    thr0 = jnp.full((tq, LANES), INT_MIN + 1, jnp.int32)
    open0 = jnp.where(count_ge(thr0) > n_sel, 1.0, 0.0)

    def bit_cond(state):
        step, _, n_open = state
        return jnp.logical_and(step < 32, n_open > 0.0)

    def bit_step(state):
        step, (thr, still_open), _ = state
        base = jnp.where(step == 0, INT_MIN, thr)
        cand = base + lax.shift_left(jnp.int32(1), 31 - step)
        cnt = count_ge(cand)
        take = jnp.logical_and(cnt >= n_sel, still_open > 0.0)
        thr = jnp.where(take, cand, thr)
        still_open = jnp.where(cnt == n_sel, 0.0, still_open)
        return step + 1, (thr, still_open), jnp.max(still_open)

    _, (thr, _), _ = lax.while_loop(bit_cond, bit_step, (jnp.int32(0), (thr0, open0), jnp.max(open0)))
    thr_w = _lanes(thr, IDX_TS)
    for c in range(seq // IDX_TS):
        sl = slice(c * IDX_TS, (c + 1) * IDX_TS)
        bias_ref[:, sl] = jnp.where(key_ref[:, sl] >= thr_w, 0.0, MASK_VALUE).astype(bias_ref.dtype)


def indexer_mask(iq, ik, iw, batch, seq):
    n_sel = min(TOPK_MAX, seq // 4)
    tq = IDX_TQ
    assert seq % IDX_TS == 0 and seq % tq == 0
    return pl.pallas_call(
        functools.partial(_indexer_kernel, n_sel=n_sel),
        out_shape=jax.ShapeDtypeStruct((batch, seq, seq), BF16),
        grid=(batch, seq // tq),
        in_specs=[pl.BlockSpec((None, tq, IDX_HEADS * IDX_DIM), lambda b, i: (b, i, 0)),
                  pl.BlockSpec((None, seq, IDX_DIM), lambda b, i: (b, 0, 0)),
                  pl.BlockSpec((None, tq, LANES), lambda b, i: (b, i, 0))],
        out_specs=pl.BlockSpec((None, tq, seq), lambda b, i: (b, i, 0)),
        scratch_shapes=[pltpu.VMEM((tq, seq), jnp.int32),
                        pltpu.VMEM((IDX_HEADS * tq, IDX_DIM), BF16),
                        pltpu.VMEM((IDX_HEADS, tq, LANES), F32)],
        compiler_params=_params(("parallel", "arbitrary")),
        name="indexer_mask",
    )(iq.reshape(batch, seq, -1), ik.reshape(batch, seq, -1), iw.reshape(batch, seq, -1))


B_TQ = 256
B_TS = 512
B_REP = B_HEADS // B_KV_HEADS
B_CHUNK = 128
B_UNIT = 512


def _sparse_attn_kernel(qblk_ref, kblk_ref, q_ref, k_ref, v_ref, bias_ref, o_ref,
                        qs_ref, vo_ref, biasf_ref, s_ref, p_ref, alpha_ref, m_ref, accl_ref):
    tq, ts = bias_ref.shape
    rows = B_REP * tq
    step = pl.program_id(1)
    i, j = qblk_ref[step], kblk_ref[step]
    last = ((i + 1) * tq - 1) // ts

    @pl.when(j == 0)
    def _():
        scale = HEAD_DIM ** -0.5 * LOG2E
        for g in range(B_KV_HEADS):
            for r in range(B_REP):
                h = g * B_REP + r
                qh = q_ref[:, h * HEAD_DIM:(h + 1) * HEAD_DIM].astype(F32) * scale
                qs_ref[g, r * tq:(r + 1) * tq, :] = qh.astype(qs_ref.dtype)
        m_ref[...] = jnp.full(m_ref.shape, MASK_VALUE, F32)
        accl_ref[...] = jnp.zeros(accl_ref.shape, F32)

    biasf_ref[...] = bias_ref[...].astype(F32)
    ones = jnp.ones((ts, LANES), BF16)
    for g in range(B_KV_HEADS):
        vo_ref[g, :, :HEAD_DIM] = v_ref[:, g * HEAD_DIM:(g + 1) * HEAD_DIM]
        vo_ref[g, :, HEAD_DIM:] = ones

    unit_rows = min(B_UNIT, rows)
    units = [(g, r0) for g in range(B_KV_HEADS) for r0 in range(0, rows, unit_rows)]

    def logits(u):
        g, r0 = units[u]
        kg = k_ref[:, g * HEAD_DIM:(g + 1) * HEAD_DIM]
        s_ref[u % 2] = lax.dot_general(qs_ref[g, r0:r0 + unit_rows, :], kg, NT_DIMS,
                                       preferred_element_type=F32)

    logits(0)
    for u, (g, r0) in enumerate(units):
        if u + 1 < len(units):
            logits(u + 1)
        par = u % 2
        for c in range(0, unit_rows, B_CHUNK):
            rs = slice(c, c + B_CHUNK)
            gs = slice(r0 + c, r0 + c + B_CHUNK)
            qrow = (r0 + c) % tq
            s = s_ref[par, rs, :] + biasf_ref[qrow:qrow + B_CHUNK, :]
            m_prev = m_ref[g, gs, :]
            m_new = jnp.maximum(m_prev, jnp.max(s, axis=1, keepdims=True))
            p_ref[par, rs, :] = jnp.exp2(s - _lanes(m_new, ts)).astype(BF16)
            alpha_ref[par, rs, :] = jnp.exp2(m_prev - m_new)
            m_ref[g, gs, :] = m_new
        pv = jnp.dot(p_ref[par], vo_ref[g], preferred_element_type=F32)
        alpha = alpha_ref[par]
        us = slice(r0, r0 + unit_rows)
        accl_ref[g, us, :] = jnp.concatenate([alpha, alpha], axis=1) * accl_ref[g, us, :] + pv

    @pl.when(j == last)
    def _():
        for g in range(B_KV_HEADS):
            out = accl_ref[g, :, :HEAD_DIM] / accl_ref[g, :, HEAD_DIM:]
            for r in range(B_REP):
                h = g * B_REP + r
                o_ref[:, h * HEAD_DIM:(h + 1) * HEAD_DIM] = out[r * tq:(r + 1) * tq].astype(o_ref.dtype)


def sparse_attention(qkv, bias, batch, seq):
    tq, ts = _tile(seq, B_TQ), _tile(seq, B_TS)
    view = qkv.reshape(batch, seq, B_QKV_WIDTH)
    k_tile = B_Q_WIDTH // B_KV_WIDTH
    pairs = [(i, j) for i in range(seq // tq) for j in range(((i + 1) * tq - 1) // ts + 1)]
    qblk = jnp.asarray([p[0] for p in pairs], jnp.int32)
    kblk = jnp.asarray([p[1] for p in pairs], jnp.int32)
    rows = B_REP * tq
    unit_rows = min(B_UNIT, rows)
    grid_spec = pltpu.PrefetchScalarGridSpec(
        num_scalar_prefetch=2,
        grid=(batch, len(pairs)),
        in_specs=[pl.BlockSpec((None, tq, B_Q_WIDTH), lambda b, s, qb, kb: (b, qb[s], 0)),
                  pl.BlockSpec((None, ts, B_KV_WIDTH), lambda b, s, qb, kb: (b, kb[s], k_tile)),
                  pl.BlockSpec((None, ts, B_KV_WIDTH), lambda b, s, qb, kb: (b, kb[s], k_tile + 1)),
                  pl.BlockSpec((None, tq, ts), lambda b, s, qb, kb: (b, qb[s], kb[s]))],
        out_specs=pl.BlockSpec((None, tq, B_Q_WIDTH), lambda b, s, qb, kb: (b, qb[s], 0)),
        scratch_shapes=[pltpu.VMEM((B_KV_HEADS, rows, HEAD_DIM), BF16),
                        pltpu.VMEM((B_KV_HEADS, ts, 2 * HEAD_DIM), BF16),
                        pltpu.VMEM((tq, ts), F32),
                        pltpu.VMEM((2, unit_rows, ts), F32),
                        pltpu.VMEM((2, unit_rows, ts), BF16),
                        pltpu.VMEM((2, unit_rows, LANES), F32),
                        pltpu.VMEM((B_KV_HEADS, rows, LANES), F32),
                        pltpu.VMEM((B_KV_HEADS, rows, 2 * HEAD_DIM), F32)])
    out = pl.pallas_call(
        _sparse_attn_kernel,
        out_shape=jax.ShapeDtypeStruct((batch, seq, B_Q_WIDTH), BF16),
        grid_spec=grid_spec,
        compiler_params=_params(("parallel", "arbitrary")),
        name="sparse_attention",
    )(qblk, kblk, view, view, view, bias)
    return out.reshape(batch * seq, B_Q_WIDTH)


def _rope_tables(positions):
    inv_freq = ROPE_THETA ** (-jnp.arange(0, HEAD_DIM, 2, dtype=F32) / HEAD_DIM)
    ang = positions.astype(F32).reshape(-1, 1) * inv_freq
    cos, sin = jnp.cos(ang), jnp.sin(ang)
    return jnp.concatenate([cos, cos], axis=-1), jnp.concatenate([-sin, sin], axis=-1)


def _mlp(f_in, w_up, w_down, layer):
    u = matmul(f_in, cast_weight(w_up, layer), act="relu2")
    return matmul(u, cast_weight(w_down, layer))


def _mixer_dilated(a, cos2, sin2, w_in, w_out, layer, batch, seq):
    tn = 1024
    per_part = A_GROUP_WIDTH // tn
    rope_pred = lambda j: (j // per_part) % 3 != 2
    a, cos2, sin2 = (_to_residue_major(t, batch, seq) for t in (a, cos2, sin2))
    qkv = matmul(a, cast_weight(w_in, layer), rope=(cos2, sin2, rope_pred), tn=tn)
    outs, lses = zip(*[dilated_attention(qkv, batch, seq, g) for g in range(N_GROUPS)])
    o = _from_residue_major(merge_groups(outs, lses), batch, seq)
    return matmul(o, cast_weight(w_out, layer))


def _mixer_sparse(a, cos2, sin2, w_in, idx_q_norm, w_idx_up, idx_k_w, idx_k_b, w_out, layer, batch, seq):
    tn = 1024
    n_rope = (B_Q_WIDTH + B_KV_WIDTH) // tn
    idx_width = IDX_Q_RANK + IDX_DIM + LANES
    w_all = cast_weight(w_in, layer, n_pad=B_QKV_WIDTH + idx_width)
    qkv = matmul(a, w_all, cols=(0, B_QKV_WIDTH), rope=(cos2, sin2, lambda j: j < n_rope), tn=tn)
    lat = matmul(a, w_all, cols=(B_QKV_WIDTH, idx_width), out_dtype=F32, tn=256)
    iq_n, ik, iw = indexer_prep(lat, idx_q_norm[layer], idx_k_w[layer], idx_k_b[layer], cos2, sin2)
    iq = matmul(iq_n, cast_weight(w_idx_up, layer), rope=(cos2, sin2, lambda j: j >= 0), tn=tn)
    bias = indexer_mask(iq, ik, iw, batch, seq)
    o = sparse_attention(qkv, bias, batch, seq)
    return matmul(o, cast_weight(w_out, layer))


def kernel(x, positions, attn_pre_norm, attn_post_norm, mlp_pre_norm, mlp_post_norm, w_in_a, w_out_a,
           w_in_b, idx_q_norm, w_idx_up, idx_k_norm_w, idx_k_norm_b, w_out_b, w_up, w_down):
    batch, seq, d_model = x.shape
    depth = attn_pre_norm.shape[0]
    cos2, sin2 = _rope_tables(positions)
    h = x.reshape(batch * seq, d_model)
    a = rms_norm_cast(h, attn_pre_norm[0])
    for i in range(depth):
        j = i // 2
        if i % 2 == 0:
            m = _mixer_dilated(a, cos2, sin2, w_in_a, w_out_a, j, batch, seq)
        else:
            m = _mixer_sparse(a, cos2, sin2, w_in_b, idx_q_norm, w_idx_up,
                              idx_k_norm_w, idx_k_norm_b, w_out_b, j, batch, seq)
        h, f_in = residual_norm(h, m, attn_post_norm[i], mlp_pre_norm[i])
        f = _mlp(f_in, w_up, w_down, i)
        h, a = residual_norm(h, f, mlp_post_norm[i], attn_pre_norm[i + 1] if i + 1 < depth else None)
    return h.reshape(batch, seq, d_model)
```

```python
import functools
import math

import jax
import jax.numpy as jnp
from jax import lax
from jax.experimental import pallas as pl
from jax.experimental.pallas import tpu as pltpu

HEAD_DIM = 128
ROPE_THETA = 10000.0
NORM_EPS = 1e-6
DILATED_GROUPS = ((128, 1), (512, 4), (2048, 16))
N_GROUPS = 3
A_HEADS_PER_GROUP = 16
A_BLOCK = 128
A_GROUP_WIDTH = A_HEADS_PER_GROUP * HEAD_DIM
B_HEADS = 32
B_KV_HEADS = 8
IDX_HEADS = 32
IDX_DIM = 128
IDX_Q_RANK = 1024
TOPK_MAX = 256
B_Q_WIDTH = B_HEADS * HEAD_DIM
B_KV_WIDTH = B_KV_HEADS * HEAD_DIM
B_QKV_WIDTH = B_Q_WIDTH + 2 * B_KV_WIDTH

LANES = 128
VMEM_LIMIT_BYTES = 56 * 2**20
MASK_VALUE = -1e30
INT_MIN = -2**31
LOG2E = math.log2(math.e)
NT_DIMS = (((1,), (1,)), ((), ()))

F32 = jnp.float32
BF16 = jnp.bfloat16


def _tile(dim, pref):
    t = min(dim, pref)
    if dim % t:
        t = 1 << (t.bit_length() - 1)
    while dim % t:
        t //= 2
    return t


def _params(semantics):
    return pltpu.CompilerParams(dimension_semantics=semantics, vmem_limit_bytes=VMEM_LIMIT_BYTES)


def _lanes(x, width):
    return jnp.concatenate([x] * (width // LANES), axis=1)


def _rms(x, g):
    return x * lax.rsqrt(jnp.mean(x * x, axis=-1, keepdims=True) + NORM_EPS) * g


def _norm_kernel(x_ref, g_ref, a_ref):
    a_ref[...] = _rms(x_ref[...], g_ref[...]).astype(a_ref.dtype)


def rms_norm_cast(x, g):
    m, d = x.shape
    tm = _tile(m, 256)
    return pl.pallas_call(
        _norm_kernel,
        out_shape=jax.ShapeDtypeStruct((m, d), BF16),
        grid=(m // tm,),
        in_specs=[pl.BlockSpec((tm, d), lambda i: (i, 0)), pl.BlockSpec((1, d), lambda i: (0, 0))],
        out_specs=pl.BlockSpec((tm, d), lambda i: (i, 0)),
        compiler_params=_params(("parallel",)),
        name="rms_norm_cast",
    )(x, g.reshape(1, d))


def _residual_kernel(h_ref, m_ref, gpost_ref, *rest, with_next):
    h = h_ref[...] + _rms(m_ref[...].astype(F32), gpost_ref[...])
    if with_next:
        gpre_ref, h_out_ref, a_ref = rest
        a_ref[...] = _rms(h, gpre_ref[...]).astype(a_ref.dtype)
    else:
        (h_out_ref,) = rest
    h_out_ref[...] = h


def residual_norm(h, m, g_post, g_pre_next=None):
    rows, d = h.shape
    tm = _tile(rows, 256)
    with_next = g_pre_next is not None
    row_spec = pl.BlockSpec((tm, d), lambda i: (i, 0))
    g_spec = pl.BlockSpec((1, d), lambda i: (0, 0))
    args = [h, m, g_post.reshape(1, d)]
    in_specs = [row_spec, row_spec, g_spec]
    out_shape = [jax.ShapeDtypeStruct((rows, d), F32)]
    out_specs = [row_spec]
    if with_next:
        args.append(g_pre_next.reshape(1, d))
        in_specs.append(g_spec)
        out_shape.append(jax.ShapeDtypeStruct((rows, d), BF16))
        out_specs.append(row_spec)
    out = pl.pallas_call(
        functools.partial(_residual_kernel, with_next=with_next),
        out_shape=out_shape,
        grid=(rows // tm,),
        in_specs=in_specs,
        out_specs=out_specs,
        compiler_params=_params(("parallel",)),
        name="residual_norm",
    )(*args)
    return (out[0], out[1]) if with_next else (out[0], None)


CAST_BLOCK_BYTES = 8 * 2**20


def _cast_kernel(w_ref, o_ref):
    n = w_ref.shape[1]
    o_ref[:, :n] = w_ref[...].astype(o_ref.dtype)
    if o_ref.shape[1] > n:
        o_ref[:, n:] = jnp.zeros((o_ref.shape[0], o_ref.shape[1] - n), o_ref.dtype)


def cast_weight(w, layer, n_pad=None):
    _, kdim, n = w.shape
    n_pad = n if n_pad is None else n_pad
    tk = _tile(kdim, max(8, CAST_BLOCK_BYTES // (4 * n)))
    return pl.pallas_call(
        _cast_kernel,
        out_shape=jax.ShapeDtypeStruct((kdim, n_pad), BF16),
        grid=(kdim // tk,),
        in_specs=[pl.BlockSpec((None, tk, n), lambda i: (layer, i, 0))],
        out_specs=pl.BlockSpec((tk, n_pad), lambda i: (i, 0)),
        compiler_params=_params(("parallel",)),
        name="cast_weight",
    )(w)


def _rope(x, cos2, sin2):
    return x * cos2 + pltpu.roll(x, HEAD_DIM // 2, 1) * sin2


def _matmul_kernel(a_ref, w_ref, *rest, nk, act, rope_pred):
    if rope_pred is not None:
        cos_ref, sin_ref, *rest = rest
    o_ref, *scratch = rest

    def finish(acc):
        if act == "relu2":
            r = jnp.maximum(acc, 0.0)
            acc = r * r
        if rope_pred is None:
            o_ref[...] = acc.astype(o_ref.dtype)
            return
        do_rope = rope_pred(pl.program_id(1))

        @pl.when(do_rope)
        def _():
            cos2 = cos_ref[...]
            sin2 = sin_ref[...]
            for c in range(o_ref.shape[1] // HEAD_DIM):
                sl = slice(c * HEAD_DIM, (c + 1) * HEAD_DIM)
                o_ref[:, sl] = _rope(acc[:, sl], cos2, sin2).astype(o_ref.dtype)

        @pl.when(jnp.logical_not(do_rope))
        def _():
            o_ref[...] = acc.astype(o_ref.dtype)

    prod = jnp.dot(a_ref[...], w_ref[...], preferred_element_type=F32)
    if nk == 1:
        finish(prod)
        return
    (acc_ref,) = scratch
    k = pl.program_id(2)

    @pl.when(k == 0)
    def _():
        acc_ref[...] = prod

    @pl.when(jnp.logical_and(k > 0, k < nk - 1))
    def _():
        acc_ref[...] += prod

    @pl.when(k == nk - 1)
    def _():
        finish(acc_ref[...] + prod)


def matmul(a, w, *, cols=None, out_dtype=BF16, act=None, rope=None, tm=1024, tn=1024, tk=4096):
    m, kdim = a.shape
    col0, n = (0, w.shape[1]) if cols is None else cols
    tm, tn, tk = _tile(m, tm), _tile(n, tn), _tile(kdim, tk)
    assert col0 % tn == 0
    j0 = col0 // tn
    nk = kdim // tk
    in_specs = [pl.BlockSpec((tm, tk), lambda i, j, k: (i, k)),
                pl.BlockSpec((tk, tn), lambda i, j, k: (k, j + j0))]
    args = [a, w]
    rope_pred = None
    if rope is not None:
        cos2, sin2, rope_pred = rope
        tab_spec = pl.BlockSpec((tm, HEAD_DIM), lambda i, j, k: (i, 0))
        in_specs += [tab_spec, tab_spec]
        args += [cos2, sin2]
    scratch = [pltpu.VMEM((tm, tn), F32)] if nk > 1 else []
    return pl.pallas_call(
        functools.partial(_matmul_kernel, nk=nk, act=act, rope_pred=rope_pred),
        out_shape=jax.ShapeDtypeStruct((m, n), out_dtype),
        grid=(m // tm, n // tn, nk),
        in_specs=in_specs,
        out_specs=pl.BlockSpec((tm, tn), lambda i, j, k: (i, j)),
        scratch_shapes=scratch,
        compiler_params=_params(("parallel", "parallel", "arbitrary")),
        name="matmul",
    )(*args)


A_PERIOD = 16


def _to_residue_major(x, batch, seq):
    u = seq // A_PERIOD
    return x.reshape(batch, u, 4, 4, -1).transpose(0, 3, 2, 1, 4).reshape(batch * seq, -1)


def _from_residue_major(x, batch, seq):
    u = seq // A_PERIOD
    return x.reshape(batch, 4, 4, u, -1).transpose(0, 3, 2, 1, 4).reshape(batch * seq, -1)


def _dilated_kernel(q_ref, kp_ref, kc_ref, vp_ref, vc_ref, o_ref, lse_ref, sp_ref, sc_ref,
                    *, w, rows, local_index, blk_axis):
    has_prev = pl.program_id(blk_axis) > 0
    qi = local_index(lax.broadcasted_iota(jnp.int32, (rows, rows), 0))
    kj = local_index(lax.broadcasted_iota(jnp.int32, (rows, rows), 1))
    mask_prev = jnp.logical_and(qi + rows - kj <= w, has_prev)
    mask_cur = jnp.logical_and(kj <= qi, qi - kj <= w)
    scale = HEAD_DIM ** -0.5 * LOG2E
    lead = (slice(None),) * (len(q_ref.shape) - 1)
    out_shape = o_ref.shape[:-1] + (HEAD_DIM,)

    def head(ref, h):
        return ref[lead + (slice(h * HEAD_DIM, (h + 1) * HEAD_DIM),)].reshape(rows, HEAD_DIM)

    def stage(h):
        q = head(q_ref, h)
        sp_ref[h % 2] = lax.dot_general(q, head(kp_ref, h), NT_DIMS, preferred_element_type=F32)
        sc_ref[h % 2] = lax.dot_general(q, head(kc_ref, h), NT_DIMS, preferred_element_type=F32)

    stage(0)
    for h in range(A_HEADS_PER_GROUP):
        if h + 1 < A_HEADS_PER_GROUP:
            stage(h + 1)
        s_prev = jnp.where(mask_prev, sp_ref[h % 2] * scale, MASK_VALUE)
        s_cur = jnp.where(mask_cur, sc_ref[h % 2] * scale, MASK_VALUE)
        m = jnp.maximum(jnp.max(s_prev, axis=1, keepdims=True), jnp.max(s_cur, axis=1, keepdims=True))
        p_prev = jnp.exp2(s_prev - m)
        p_cur = jnp.exp2(s_cur - m)
        l = jnp.sum(p_prev, axis=1, keepdims=True) + jnp.sum(p_cur, axis=1, keepdims=True)
        o = jnp.dot(p_prev.astype(BF16), head(vp_ref, h), preferred_element_type=F32)
        o += jnp.dot(p_cur.astype(BF16), head(vc_ref, h), preferred_element_type=F32)
        idx = lead + (slice(h * HEAD_DIM, (h + 1) * HEAD_DIM),)
        o_ref[idx] = (o / l).astype(o_ref.dtype).reshape(out_shape)
        lse_ref[idx] = jnp.broadcast_to(m * (1.0 / LOG2E) + jnp.log(l), (rows, HEAD_DIM)).reshape(out_shape)


def dilated_attention(qkv, batch, seq, group):
    window, r = DILATED_GROUPS[group]
    w = window // r
    u = seq // A_PERIOD
    width = qkv.shape[1]
    view = qkv.reshape(batch, 4, 4, u, width)
    if r == 16:
        rows, n_blk = A_BLOCK, u // A_BLOCK
        block = (None, None, None, rows, A_GROUP_WIDTH)
        grid = (batch, 4, 4, n_blk)
        place = lambda g, blk: (g[0], g[1], g[2], blk)
        local_index = lambda rho: rho
    elif r == 4:
        rows, n_blk = A_BLOCK, u // (A_BLOCK // 4)
        block = (None, None, 4, rows // 4, A_GROUP_WIDTH)
        grid = (batch, 4, n_blk)
        place = lambda g, blk: (g[0], g[1], 0, blk)
        local_index = lambda rho: 4 * (rho % (rows // 4)) + rho // (rows // 4)
    else:
        assert r == 1
        rows, n_blk = 2 * A_BLOCK, u // (2 * A_BLOCK // A_PERIOD)
        per = rows // A_PERIOD
        block = (None, 4, 4, per, A_GROUP_WIDTH)
        grid = (batch, n_blk)
        place = lambda g, blk: (g[0], 0, 0, blk)
        local_index = lambda rho: A_PERIOD * (rho % per) + 4 * ((rho // per) % 4) + rho // (4 * per)
    assert w <= rows and n_blk * rows * r == seq

    def spec(which, prev):
        def index(*g):
            blk = jnp.maximum(g[-1] - 1, 0) if prev else g[-1]
            return place(g, blk) + (group * 3 + which,)
        return pl.BlockSpec(block, index)

    out_spec = pl.BlockSpec(block, lambda *g: place(g, g[-1]) + (0,))
    o, lse = pl.pallas_call(
        functools.partial(_dilated_kernel, w=w, rows=rows, local_index=local_index, blk_axis=len(grid) - 1),
        out_shape=[jax.ShapeDtypeStruct((batch, 4, 4, u, A_GROUP_WIDTH), BF16),
                   jax.ShapeDtypeStruct((batch, 4, 4, u, A_GROUP_WIDTH), F32)],
        grid=grid,
        in_specs=[spec(0, False), spec(1, True), spec(1, False), spec(2, True), spec(2, False)],
        out_specs=[out_spec, out_spec],
        scratch_shapes=[pltpu.VMEM((2, rows, rows), F32),
                        pltpu.VMEM((2, rows, rows), F32)],
        compiler_params=_params(("parallel",) * (len(grid) - 1) + ("arbitrary",)),
        name="dilated_attention",
    )(view, view, view, view, view)
    return o.reshape(batch * seq, A_GROUP_WIDTH), lse.reshape(batch * seq, A_GROUP_WIDTH)


def _merge_kernel(o0_ref, o1_ref, o2_ref, l0_ref, l1_ref, l2_ref, out_ref):
    l0, l1, l2 = l0_ref[...], l1_ref[...], l2_ref[...]
    m = jnp.maximum(jnp.maximum(l0, l1), l2)
    e0, e1, e2 = jnp.exp(l0 - m), jnp.exp(l1 - m), jnp.exp(l2 - m)
    num = e0 * o0_ref[...].astype(F32) + e1 * o1_ref[...].astype(F32) + e2 * o2_ref[...].astype(F32)
    out_ref[...] = (num / (e0 + e1 + e2)).astype(out_ref.dtype)


def merge_groups(outs, lses):
    rows, width = outs[0].shape
    tm = _tile(rows, 256)
    spec = pl.BlockSpec((tm, width), lambda i: (i, 0))
    return pl.pallas_call(
        _merge_kernel,
        out_shape=jax.ShapeDtypeStruct((rows, width), BF16),
        grid=(rows // tm,),
        in_specs=[spec] * 6,
        out_specs=spec,
        compiler_params=_params(("parallel",)),
        name="merge_groups",
    )(*outs, *lses)


def _indexer_prep_kernel(lat_ref, gq_ref, lnw_ref, lnb_ref, cos_ref, sin_ref, iqn_ref, ik_ref, iw_ref):
    iqn_ref[...] = _rms(lat_ref[:, :IDX_Q_RANK], gq_ref[...]).astype(iqn_ref.dtype)
    k = lat_ref[:, IDX_Q_RANK:IDX_Q_RANK + IDX_DIM]
    kc = k - jnp.mean(k, axis=-1, keepdims=True)
    kn = kc * lax.rsqrt(jnp.mean(kc * kc, axis=-1, keepdims=True) + NORM_EPS) * lnw_ref[...] + lnb_ref[...]
    ik_ref[...] = _rope(kn, cos_ref[...], sin_ref[...]).astype(ik_ref.dtype)
    iw_ref[...] = lat_ref[:, IDX_Q_RANK + IDX_DIM:] * (IDX_HEADS ** -0.5 * IDX_DIM ** -0.5)


def indexer_prep(lat, gq, lnw, lnb, cos2, sin2):
    rows, width = lat.shape
    tm = _tile(rows, 512)
    row = lambda wd: pl.BlockSpec((tm, wd), lambda i: (i, 0))
    vec = lambda wd: pl.BlockSpec((1, wd), lambda i: (0, 0))
    return pl.pallas_call(
        _indexer_prep_kernel,
        out_shape=[jax.ShapeDtypeStruct((rows, IDX_Q_RANK), BF16),
                   jax.ShapeDtypeStruct((rows, IDX_DIM), BF16),
                   jax.ShapeDtypeStruct((rows, LANES), F32)],
        grid=(rows // tm,),
        in_specs=[row(width), vec(IDX_Q_RANK), vec(IDX_DIM), vec(IDX_DIM), row(HEAD_DIM), row(HEAD_DIM)],
        out_specs=[row(IDX_Q_RANK), row(IDX_DIM), row(LANES)],
        compiler_params=_params(("parallel",)),
        name="indexer_prep",
    )(lat, gq.reshape(1, -1), lnw.reshape(1, -1), lnb.reshape(1, -1), cos2, sin2)


IDX_TQ = 128
IDX_TS = 512
IDX_HEAD_BATCH = 8


def _indexer_kernel(iq_ref, ik_ref, iw_ref, bias_ref, key_ref, qs_ref, wb_ref, *, n_sel):
    tq, seq = key_ref.shape
    i = pl.program_id(1)
    n_chunks = ((i + 1) * tq + IDX_TS - 1) // IDX_TS
    key_ref[...] = jnp.full((tq, seq), INT_MIN, jnp.int32)
    iw = iw_ref[...]
    for h in range(IDX_HEADS):
        qs_ref[h * tq:(h + 1) * tq, :] = iq_ref[:, h * IDX_DIM:(h + 1) * IDX_DIM]
        wb_ref[h] = jnp.broadcast_to(iw[:, h:h + 1], (tq, LANES))
    t_pos = i * tq + lax.broadcasted_iota(jnp.int32, (tq, IDX_TS), 0)
    lane = lax.broadcasted_iota(jnp.int32, (tq, IDX_TS), 1)
    hb_rows = IDX_HEAD_BATCH * tq

    def score_chunk(c, carry):
        start = pl.multiple_of(c * IDX_TS, IDX_TS)
        keys = ik_ref[pl.ds(start, IDX_TS), :]
        acc = jnp.zeros((tq, IDX_TS), F32)
        for hb in range(IDX_HEADS // IDX_HEAD_BATCH):
            rel = lax.dot_general(qs_ref[hb * hb_rows:(hb + 1) * hb_rows, :], keys, NT_DIMS,
                                  preferred_element_type=F32)
            for hh in range(IDX_HEAD_BATCH):
                wb = _lanes(wb_ref[hb * IDX_HEAD_BATCH + hh], IDX_TS)
                acc += jnp.maximum(rel[hh * tq:(hh + 1) * tq], 0.0) * wb
        bits = lax.bitcast_convert_type(acc, jnp.int32)
        ordered = jnp.where(bits >= 0, bits, bits ^ jnp.int32(0x7FFFFFFF))
        key_ref[:, pl.ds(start, IDX_TS)] = jnp.where(start + lane <= t_pos, ordered, INT_MIN)
        return carry

    lax.fori_loop(0, n_chunks, score_chunk, 0)

    def count_ge(cand):
        cand_w = _lanes(cand, IDX_TS)

        def body(c, cnt):
            start = pl.multiple_of(c * IDX_TS, IDX_TS)
            ge = jnp.where(key_ref[:, pl.ds(start, IDX_TS)] >= cand_w, 1.0, 0.0)
            for b in range(IDX_TS // LANES):
                cnt = cnt + ge[:, b * LANES:(b + 1) * LANES]
            return cnt

        cnt = lax.fori_loop(0, n_chunks, body, jnp.zeros((tq, LANES), F32))
        return jnp.sum(cnt, axis=1, keepdims=True)

    thr0 = jnp.full((tq, LANES), INT_MIN + 1, jnp.int32)
    open0 = jnp.where(count_ge(thr0) > n_sel, 1.0, 0.0)

    def bit_cond(state):
        step, _, n_open = state
        return jnp.logical_and(step < 32, n_open > 0.0)

    def bit_step(state):
        step, (thr, still_open), _ = state
        base = jnp.where(step == 0, INT_MIN, thr)
        cand = base + lax.shift_left(jnp.int32(1), 31 - step)
        cnt = count_ge(cand)
        take = jnp.logical_and(cnt >= n_sel, still_open > 0.0)
        thr = jnp.where(take, cand, thr)
        still_open = jnp.where(cnt == n_sel, 0.0, still_open)
        return step + 1, (thr, still_open), jnp.max(still_open)

    _, (thr, _), _ = lax.while_loop(bit_cond, bit_step, (jnp.int32(0), (thr0, open0), jnp.max(open0)))
    thr_w = _lanes(thr, IDX_TS)
    for c in range(seq // IDX_TS):
        sl = slice(c * IDX_TS, (c + 1) * IDX_TS)
        bias_ref[:, sl] = jnp.where(key_ref[:, sl] >= thr_w, 0.0, MASK_VALUE).astype(bias_ref.dtype)


def indexer_mask(iq, ik, iw, batch, seq):
    n_sel = min(TOPK_MAX, seq // 4)
    tq = IDX_TQ
    assert seq % IDX_TS == 0 and seq % tq == 0
    return pl.pallas_call(
        functools.partial(_indexer_kernel, n_sel=n_sel),
        out_shape=jax.ShapeDtypeStruct((batch, seq, seq), BF16),
        grid=(batch, seq // tq),
        in_specs=[pl.BlockSpec((None, tq, IDX_HEADS * IDX_DIM), lambda b, i: (b, i, 0)),
                  pl.BlockSpec((None, seq, IDX_DIM), lambda b, i: (b, 0, 0)),
                  pl.BlockSpec((None, tq, LANES), lambda b, i: (b, i, 0))],
        out_specs=pl.BlockSpec((None, tq, seq), lambda b, i: (b, i, 0)),
        scratch_shapes=[pltpu.VMEM((tq, seq), jnp.int32),
                        pltpu.VMEM((IDX_HEADS * tq, IDX_DIM), BF16),
                        pltpu.VMEM((IDX_HEADS, tq, LANES), F32)],
        compiler_params=_params(("parallel", "arbitrary")),
        name="indexer_mask",
    )(iq.reshape(batch, seq, -1), ik.reshape(batch, seq, -1), iw.reshape(batch, seq, -1))


B_TQ = 256
B_TS = 512
B_REP = B_HEADS // B_KV_HEADS
B_CHUNK = 128
B_UNIT = 512


def _sparse_attn_kernel(qblk_ref, kblk_ref, q_ref, k_ref, v_ref, bias_ref, o_ref,
                        qs_ref, vo_ref, biasf_ref, s_ref, p_ref, alpha_ref, m_ref, accl_ref):
    tq, ts = bias_ref.shape
    rows = B_REP * tq
    step = pl.program_id(1)
    i, j = qblk_ref[step], kblk_ref[step]
    last = ((i + 1) * tq - 1) // ts

    @pl.when(j == 0)
    def _():
        scale = HEAD_DIM ** -0.5 * LOG2E
        for g in range(B_KV_HEADS):
            for r in range(B_REP):
                h = g * B_REP + r
                qh = q_ref[:, h * HEAD_DIM:(h + 1) * HEAD_DIM].astype(F32) * scale
                qs_ref[g, r * tq:(r + 1) * tq, :] = qh.astype(qs_ref.dtype)
        m_ref[...] = jnp.full(m_ref.shape, MASK_VALUE, F32)
        accl_ref[...] = jnp.zeros(accl_ref.shape, F32)

    biasf_ref[...] = bias_ref[...].astype(F32)
    ones = jnp.ones((ts, LANES), BF16)
    for g in range(B_KV_HEADS):
        vo_ref[g, :, :HEAD_DIM] = v_ref[:, g * HEAD_DIM:(g + 1) * HEAD_DIM]
        vo_ref[g, :, HEAD_DIM:] = ones

    unit_rows = min(B_UNIT, rows)
    units = [(g, r0) for g in range(B_KV_HEADS) for r0 in range(0, rows, unit_rows)]

    def logits(u):
        g, r0 = units[u]
        kg = k_ref[:, g * HEAD_DIM:(g + 1) * HEAD_DIM]
        s_ref[u % 2] = lax.dot_general(qs_ref[g, r0:r0 + unit_rows, :], kg, NT_DIMS,
                                       preferred_element_type=F32)

    logits(0)
    for u, (g, r0) in enumerate(units):
        if u + 1 < len(units):
            logits(u + 1)
        par = u % 2
        for c in range(0, unit_rows, B_CHUNK):
            rs = slice(c, c + B_CHUNK)
            gs = slice(r0 + c, r0 + c + B_CHUNK)
            qrow = (r0 + c) % tq
            s = s_ref[par, rs, :] + biasf_ref[qrow:qrow + B_CHUNK, :]
            m_prev = m_ref[g, gs, :]
            m_new = jnp.maximum(m_prev, jnp.max(s, axis=1, keepdims=True))
            p_ref[par, rs, :] = jnp.exp2(s - _lanes(m_new, ts)).astype(BF16)
            alpha_ref[par, rs, :] = jnp.exp2(m_prev - m_new)
            m_ref[g, gs, :] = m_new
        pv = jnp.dot(p_ref[par], vo_ref[g], preferred_element_type=F32)
        alpha = alpha_ref[par]
        us = slice(r0, r0 + unit_rows)
        accl_ref[g, us, :] = jnp.concatenate([alpha, alpha], axis=1) * accl_ref[g, us, :] + pv

    @pl.when(j == last)
    def _():
        for g in range(B_KV_HEADS):
            out = accl_ref[g, :, :HEAD_DIM] / accl_ref[g, :, HEAD_DIM:]
            for r in range(B_REP):
                h = g * B_REP + r
                o_ref[:, h * HEAD_DIM:(h + 1) * HEAD_DIM] = out[r * tq:(r + 1) * tq].astype(o_ref.dtype)


def sparse_attention(qkv, bias, batch, seq):
    tq, ts = _tile(seq, B_TQ), _tile(seq, B_TS)
    view = qkv.reshape(batch, seq, B_QKV_WIDTH)
    k_tile = B_Q_WIDTH // B_KV_WIDTH
    pairs = [(i, j) for i in range(seq // tq) for j in range(((i + 1) * tq - 1) // ts + 1)]
    qblk = jnp.asarray([p[0] for p in pairs], jnp.int32)
    kblk = jnp.asarray([p[1] for p in pairs], jnp.int32)
    rows = B_REP * tq
    unit_rows = min(B_UNIT, rows)
    grid_spec = pltpu.PrefetchScalarGridSpec(
        num_scalar_prefetch=2,
        grid=(batch, len(pairs)),
        in_specs=[pl.BlockSpec((None, tq, B_Q_WIDTH), lambda b, s, qb, kb: (b, qb[s], 0)),
                  pl.BlockSpec((None, ts, B_KV_WIDTH), lambda b, s, qb, kb: (b, kb[s], k_tile)),
                  pl.BlockSpec((None, ts, B_KV_WIDTH), lambda b, s, qb, kb: (b, kb[s], k_tile + 1)),
                  pl.BlockSpec((None, tq, ts), lambda b, s, qb, kb: (b, qb[s], kb[s]))],
        out_specs=pl.BlockSpec((None, tq, B_Q_WIDTH), lambda b, s, qb, kb: (b, qb[s], 0)),
        scratch_shapes=[pltpu.VMEM((B_KV_HEADS, rows, HEAD_DIM), BF16),
                        pltpu.VMEM((B_KV_HEADS, ts, 2 * HEAD_DIM), BF16),
                        pltpu.VMEM((tq, ts), F32),
                        pltpu.VMEM((2, unit_rows, ts), F32),
                        pltpu.VMEM((2, unit_rows, ts), BF16),
                        pltpu.VMEM((2, unit_rows, LANES), F32),
                        pltpu.VMEM((B_KV_HEADS, rows, LANES), F32),
                        pltpu.VMEM((B_KV_HEADS, rows, 2 * HEAD_DIM), F32)])
    out = pl.pallas_call(
        _sparse_attn_kernel,
        out_shape=jax.ShapeDtypeStruct((batch, seq, B_Q_WIDTH), BF16),
        grid_spec=grid_spec,
        compiler_params=_params(("parallel", "arbitrary")),
        name="sparse_attention",
    )(qblk, kblk, view, view, view, bias)
    return out.reshape(batch * seq, B_Q_WIDTH)


def _rope_tables(positions):
    inv_freq = ROPE_THETA ** (-jnp.arange(0, HEAD_DIM, 2, dtype=F32) / HEAD_DIM)
    ang = positions.astype(F32).reshape(-1, 1) * inv_freq
    cos, sin = jnp.cos(ang), jnp.sin(ang)
    return jnp.concatenate([cos, cos], axis=-1), jnp.concatenate([-sin, sin], axis=-1)


def _mlp(f_in, w_up, w_down, layer):
    u = matmul(f_in, cast_weight(w_up, layer), act="relu2")
    return matmul(u, cast_weight(w_down, layer))


def _mixer_dilated(a, cos2, sin2, w_in, w_out, layer, batch, seq):
    tn = 1024
    per_part = A_GROUP_WIDTH // tn
    rope_pred = lambda j: (j // per_part) % 3 != 2
    a, cos2, sin2 = (_to_residue_major(t, batch, seq) for t in (a, cos2, sin2))
    qkv = matmul(a, cast_weight(w_in, layer), rope=(cos2, sin2, rope_pred), tn=tn)
    outs, lses = zip(*[dilated_attention(qkv, batch, seq, g) for g in range(N_GROUPS)])
    o = _from_residue_major(merge_groups(outs, lses), batch, seq)
    return matmul(o, cast_weight(w_out, layer))


def _mixer_sparse(a, cos2, sin2, w_in, idx_q_norm, w_idx_up, idx_k_w, idx_k_b, w_out, layer, batch, seq):
    tn = 1024
    n_rope = (B_Q_WIDTH + B_KV_WIDTH) // tn
    idx_width = IDX_Q_RANK + IDX_DIM + LANES
    w_all = cast_weight(w_in, layer, n_pad=B_QKV_WIDTH + idx_width)
    qkv = matmul(a, w_all, cols=(0, B_QKV_WIDTH), rope=(cos2, sin2, lambda j: j < n_rope), tn=tn)
    lat = matmul(a, w_all, cols=(B_QKV_WIDTH, idx_width), out_dtype=F32, tn=256)
    iq_n, ik, iw = indexer_prep(lat, idx_q_norm[layer], idx_k_w[layer], idx_k_b[layer], cos2, sin2)
    iq = matmul(iq_n, cast_weight(w_idx_up, layer), rope=(cos2, sin2, lambda j: j >= 0), tn=tn)
    bias = indexer_mask(iq, ik, iw, batch, seq)
    o = sparse_attention(qkv, bias, batch, seq)
    return matmul(o, cast_weight(w_out, layer))


def kernel(x, positions, attn_pre_norm, attn_post_norm, mlp_pre_norm, mlp_post_norm, w_in_a, w_out_a,
           w_in_b, idx_q_norm, w_idx_up, idx_k_norm_w, idx_k_norm_b, w_out_b, w_up, w_down):
    batch, seq, d_model = x.shape
    depth = attn_pre_norm.shape[0]
    cos2, sin2 = _rope_tables(positions)
    h = x.reshape(batch * seq, d_model)
    a = rms_norm_cast(h, attn_pre_norm[0])
    for i in range(depth):
        j = i // 2
        if i % 2 == 0:
            m = _mixer_dilated(a, cos2, sin2, w_in_a, w_out_a, j, batch, seq)
        else:
            m = _mixer_sparse(a, cos2, sin2, w_in_b, idx_q_norm, w_idx_up,
                              idx_k_norm_w, idx_k_norm_b, w_out_b, j, batch, seq)
        h, f_in = residual_norm(h, m, attn_post_norm[i], mlp_pre_norm[i])
        f = _mlp(f_in, w_up, w_down, i)
        h, a = residual_norm(h, f, mlp_post_norm[i], attn_pre_norm[i + 1] if i + 1 < depth else None)
    return h.reshape(batch, seq, d_model)
```

```python
import functools
import math

import jax
import jax.numpy as jnp
from jax import lax
from jax.experimental import pallas as pl
from jax.experimental.pallas import tpu as pltpu

HEAD_DIM = 128
ROPE_THETA = 10000.0
NORM_EPS = 1e-6
DILATED_GROUPS = ((128, 1), (512, 4), (2048, 16))
N_GROUPS = 3
A_HEADS_PER_GROUP = 16
A_BLOCK = 128
A_GROUP_WIDTH = A_HEADS_PER_GROUP * HEAD_DIM
B_HEADS = 32
B_KV_HEADS = 8
IDX_HEADS = 32
IDX_DIM = 128
IDX_Q_RANK = 1024
TOPK_MAX = 256
B_Q_WIDTH = B_HEADS * HEAD_DIM
B_KV_WIDTH = B_KV_HEADS * HEAD_DIM
B_QKV_WIDTH = B_Q_WIDTH + 2 * B_KV_WIDTH

LANES = 128
VMEM_LIMIT_BYTES = 56 * 2**20
MASK_VALUE = -1e30
INT_MIN = -2**31
LOG2E = math.log2(math.e)
NT_DIMS = (((1,), (1,)), ((), ()))

F32 = jnp.float32
BF16 = jnp.bfloat16


def _tile(dim, pref):
    t = min(dim, pref)
    if dim % t:
        t = 1 << (t.bit_length() - 1)
    while dim % t:
        t //= 2
    return t


def _params(semantics):
    return pltpu.CompilerParams(dimension_semantics=semantics, vmem_limit_bytes=VMEM_LIMIT_BYTES)


def _lanes(x, width):
    return jnp.concatenate([x] * (width // LANES), axis=1)


def _rms(x, g):
    return x * lax.rsqrt(jnp.mean(x * x, axis=-1, keepdims=True) + NORM_EPS) * g


def _norm_kernel(x_ref, g_ref, a_ref):
    a_ref[...] = _rms(x_ref[...], g_ref[...]).astype(a_ref.dtype)


def rms_norm_cast(x, g):
    m, d = x.shape
    tm = _tile(m, 256)
    return pl.pallas_call(
        _norm_kernel,
        out_shape=jax.ShapeDtypeStruct((m, d), BF16),
        grid=(m // tm,),
        in_specs=[pl.BlockSpec((tm, d), lambda i: (i, 0)), pl.BlockSpec((1, d), lambda i: (0, 0))],
        out_specs=pl.BlockSpec((tm, d), lambda i: (i, 0)),
        compiler_params=_params(("parallel",)),
        name="rms_norm_cast",
    )(x, g.reshape(1, d))


def _residual_kernel(h_ref, m_ref, gpost_ref, *rest, with_next):
    h = h_ref[...] + _rms(m_ref[...].astype(F32), gpost_ref[...])
    if with_next:
        gpre_ref, h_out_ref, a_ref = rest
        a_ref[...] = _rms(h, gpre_ref[...]).astype(a_ref.dtype)
    else:
        (h_out_ref,) = rest
    h_out_ref[...] = h


def residual_norm(h, m, g_post, g_pre_next=None):
    rows, d = h.shape
    tm = _tile(rows, 256)
    with_next = g_pre_next is not None
    row_spec = pl.BlockSpec((tm, d), lambda i: (i, 0))
    g_spec = pl.BlockSpec((1, d), lambda i: (0, 0))
    args = [h, m, g_post.reshape(1, d)]
    in_specs = [row_spec, row_spec, g_spec]
    out_shape = [jax.ShapeDtypeStruct((rows, d), F32)]
    out_specs = [row_spec]
    if with_next:
        args.append(g_pre_next.reshape(1, d))
        in_specs.append(g_spec)
        out_shape.append(jax.ShapeDtypeStruct((rows, d), BF16))
        out_specs.append(row_spec)
    out = pl.pallas_call(
        functools.partial(_residual_kernel, with_next=with_next),
        out_shape=out_shape,
        grid=(rows // tm,),
        in_specs=in_specs,
        out_specs=out_specs,
        compiler_params=_params(("parallel",)),
        name="residual_norm",
    )(*args)
    return (out[0], out[1]) if with_next else (out[0], None)


CAST_BLOCK_BYTES = 8 * 2**20


def _cast_kernel(w_ref, o_ref):
    n = w_ref.shape[1]
    o_ref[:, :n] = w_ref[...].astype(o_ref.dtype)
    if o_ref.shape[1] > n:
        o_ref[:, n:] = jnp.zeros((o_ref.shape[0], o_ref.shape[1] - n), o_ref.dtype)


def cast_weight(w, layer, n_pad=None):
    _, kdim, n = w.shape
    n_pad = n if n_pad is None else n_pad
    tk = _tile(kdim, max(8, CAST_BLOCK_BYTES // (4 * n)))
    return pl.pallas_call(
        _cast_kernel,
        out_shape=jax.ShapeDtypeStruct((kdim, n_pad), BF16),
        grid=(kdim // tk,),
        in_specs=[pl.BlockSpec((None, tk, n), lambda i: (layer, i, 0))],
        out_specs=pl.BlockSpec((tk, n_pad), lambda i: (i, 0)),
        compiler_params=_params(("parallel",)),
        name="cast_weight",
    )(w)


def _rope(x, cos2, sin2):
    return x * cos2 + pltpu.roll(x, HEAD_DIM // 2, 1) * sin2


def _matmul_kernel(a_ref, w_ref, *rest, nk, act, rope_pred):
    if rope_pred is not None:
        cos_ref, sin_ref, *rest = rest
    o_ref, *scratch = rest

    def finish(acc):
        if act == "relu2":
            r = jnp.maximum(acc, 0.0)
            acc = r * r
        if rope_pred is None:
            o_ref[...] = acc.astype(o_ref.dtype)
            return
        do_rope = rope_pred(pl.program_id(1))

        @pl.when(do_rope)
        def _():
            cos2 = cos_ref[...]
            sin2 = sin_ref[...]
            for c in range(o_ref.shape[1] // HEAD_DIM):
                sl = slice(c * HEAD_DIM, (c + 1) * HEAD_DIM)
                o_ref[:, sl] = _rope(acc[:, sl], cos2, sin2).astype(o_ref.dtype)

        @pl.when(jnp.logical_not(do_rope))
        def _():
            o_ref[...] = acc.astype(o_ref.dtype)

    prod = jnp.dot(a_ref[...], w_ref[...], preferred_element_type=F32)
    if nk == 1:
        finish(prod)
        return
    (acc_ref,) = scratch
    k = pl.program_id(2)

    @pl.when(k == 0)
    def _():
        acc_ref[...] = prod

    @pl.when(jnp.logical_and(k > 0, k < nk - 1))
    def _():
        acc_ref[...] += prod

    @pl.when(k == nk - 1)
    def _():
        finish(acc_ref[...] + prod)


def matmul(a, w, *, cols=None, out_dtype=BF16, act=None, rope=None, tm=1024, tn=1024, tk=4096):
    m, kdim = a.shape
    col0, n = (0, w.shape[1]) if cols is None else cols
    tm, tn, tk = _tile(m, tm), _tile(n, tn), _tile(kdim, tk)
    assert col0 % tn == 0
    j0 = col0 // tn
    nk = kdim // tk
    in_specs = [pl.BlockSpec((tm, tk), lambda i, j, k: (i, k)),
                pl.BlockSpec((tk, tn), lambda i, j, k: (k, j + j0))]
    args = [a, w]
    rope_pred = None
    if rope is not None:
        cos2, sin2, rope_pred = rope
        tab_spec = pl.BlockSpec((tm, HEAD_DIM), lambda i, j, k: (i, 0))
        in_specs += [tab_spec, tab_spec]
        args += [cos2, sin2]
    scratch = [pltpu.VMEM((tm, tn), F32)] if nk > 1 else []
    return pl.pallas_call(
        functools.partial(_matmul_kernel, nk=nk, act=act, rope_pred=rope_pred),
        out_shape=jax.ShapeDtypeStruct((m, n), out_dtype),
        grid=(m // tm, n // tn, nk),
        in_specs=in_specs,
        out_specs=pl.BlockSpec((tm, tn), lambda i, j, k: (i, j)),
        scratch_shapes=scratch,
        compiler_params=_params(("parallel", "parallel", "arbitrary")),
        name="matmul",
    )(*args)


A_PERIOD = 16


def _to_residue_major(x, batch, seq):
    u = seq // A_PERIOD
    return x.reshape(batch, u, 4, 4, -1).transpose(0, 3, 2, 1, 4).reshape(batch * seq, -1)


def _from_residue_major(x, batch, seq):
    u = seq // A_PERIOD
    return x.reshape(batch, 4, 4, u, -1).transpose(0, 3, 2, 1, 4).reshape(batch * seq, -1)


def _dilated_kernel(q_ref, kp_ref, kc_ref, vp_ref, vc_ref, o_ref, lse_ref, sp_ref, sc_ref,
                    *, w, rows, local_index, blk_axis):
    has_prev = pl.program_id(blk_axis) > 0
    qi = local_index(lax.broadcasted_iota(jnp.int32, (rows, rows), 0))
    kj = local_index(lax.broadcasted_iota(jnp.int32, (rows, rows), 1))
    mask_prev = jnp.logical_and(qi + rows - kj <= w, has_prev)
    mask_cur = jnp.logical_and(kj <= qi, qi - kj <= w)
    scale = HEAD_DIM ** -0.5 * LOG2E
    lead = (slice(None),) * (len(q_ref.shape) - 1)
    out_shape = o_ref.shape[:-1] + (HEAD_DIM,)

    def head(ref, h):
        return ref[lead + (slice(h * HEAD_DIM, (h + 1) * HEAD_DIM),)].reshape(rows, HEAD_DIM)

    def stage(h):
        q = head(q_ref, h)
        sp_ref[h % 2] = lax.dot_general(q, head(kp_ref, h), NT_DIMS, preferred_element_type=F32)
        sc_ref[h % 2] = lax.dot_general(q, head(kc_ref, h), NT_DIMS, preferred_element_type=F32)

    stage(0)
    for h in range(A_HEADS_PER_GROUP):
        if h + 1 < A_HEADS_PER_GROUP:
            stage(h + 1)
        s_prev = jnp.where(mask_prev, sp_ref[h % 2] * scale, MASK_VALUE)
        s_cur = jnp.where(mask_cur, sc_ref[h % 2] * scale, MASK_VALUE)
        m = jnp.maximum(jnp.max(s_prev, axis=1, keepdims=True), jnp.max(s_cur, axis=1, keepdims=True))
        p_prev = jnp.exp2(s_prev - m)
        p_cur = jnp.exp2(s_cur - m)
        l = jnp.sum(p_prev, axis=1, keepdims=True) + jnp.sum(p_cur, axis=1, keepdims=True)
        o = jnp.dot(p_prev.astype(BF16), head(vp_ref, h), preferred_element_type=F32)
        o += jnp.dot(p_cur.astype(BF16), head(vc_ref, h), preferred_element_type=F32)
        idx = lead + (slice(h * HEAD_DIM, (h + 1) * HEAD_DIM),)
        o_ref[idx] = (o / l).astype(o_ref.dtype).reshape(out_shape)
        lse_ref[idx] = jnp.broadcast_to(m * (1.0 / LOG2E) + jnp.log(l), (rows, HEAD_DIM)).reshape(out_shape)


def dilated_attention(qkv, batch, seq, group):
    window, r = DILATED_GROUPS[group]
    w = window // r
    u = seq // A_PERIOD
    width = qkv.shape[1]
    view = qkv.reshape(batch, 4, 4, u, width)
    if r == 16:
        rows, n_blk = A_BLOCK, u // A_BLOCK
        block = (None, None, None, rows, A_GROUP_WIDTH)
        grid = (batch, 4, 4, n_blk)
        place = lambda g, blk: (g[0], g[1], g[2], blk)
        local_index = lambda rho: rho
    elif r == 4:
        rows, n_blk = A_BLOCK, u // (A_BLOCK // 4)
        block = (None, None, 4, rows // 4, A_GROUP_WIDTH)
        grid = (batch, 4, n_blk)
        place = lambda g, blk: (g[0], g[1], 0, blk)
        local_index = lambda rho: 4 * (rho % (rows // 4)) + rho // (rows // 4)
    else:
        assert r == 1
        rows, n_blk = 2 * A_BLOCK, u // (2 * A_BLOCK // A_PERIOD)
        per = rows // A_PERIOD
        block = (None, 4, 4, per, A_GROUP_WIDTH)
        grid = (batch, n_blk)
        place = lambda g, blk: (g[0], 0, 0, blk)
        local_index = lambda rho: A_PERIOD * (rho % per) + 4 * ((rho // per) % 4) + rho // (4 * per)
    assert w <= rows and n_blk * rows * r == seq

    def spec(which, prev):
        def index(*g):
            blk = jnp.maximum(g[-1] - 1, 0) if prev else g[-1]
            return place(g, blk) + (group * 3 + which,)
        return pl.BlockSpec(block, index)

    out_spec = pl.BlockSpec(block, lambda *g: place(g, g[-1]) + (0,))
    o, lse = pl.pallas_call(
        functools.partial(_dilated_kernel, w=w, rows=rows, local_index=local_index, blk_axis=len(grid) - 1),
        out_shape=[jax.ShapeDtypeStruct((batch, 4, 4, u, A_GROUP_WIDTH), BF16),
                   jax.ShapeDtypeStruct((batch, 4, 4, u, A_GROUP_WIDTH), F32)],
        grid=grid,
        in_specs=[spec(0, False), spec(1, True), spec(1, False), spec(2, True), spec(2, False)],
        out_specs=[out_spec, out_spec],
        scratch_shapes=[pltpu.VMEM((2, rows, rows), F32),
                        pltpu.VMEM((2, rows, rows), F32)],
        compiler_params=_params(("parallel",) * (len(grid) - 1) + ("arbitrary",)),
        name="dilated_attention",
    )(view, view, view, view, view)
    return o.reshape(batch * seq, A_GROUP_WIDTH), lse.reshape(batch * seq, A_GROUP_WIDTH)


def _merge_kernel(o0_ref, o1_ref, o2_ref, l0_ref, l1_ref, l2_ref, out_ref):
    l0, l1, l2 = l0_ref[...], l1_ref[...], l2_ref[...]
    m = jnp.maximum(jnp.maximum(l0, l1), l2)
    e0, e1, e2 = jnp.exp(l0 - m), jnp.exp(l1 - m), jnp.exp(l2 - m)
    num = e0 * o0_ref[...].astype(F32) + e1 * o1_ref[...].astype(F32) + e2 * o2_ref[...].astype(F32)
    out_ref[...] = (num / (e0 + e1 + e2)).astype(out_ref.dtype)


def merge_groups(outs, lses):
    rows, width = outs[0].shape
    tm = _tile(rows, 256)
    spec = pl.BlockSpec((tm, width), lambda i: (i, 0))
    return pl.pallas_call(
        _merge_kernel,
        out_shape=jax.ShapeDtypeStruct((rows, width), BF16),
        grid=(rows // tm,),
        in_specs=[spec] * 6,
        out_specs=spec,
        compiler_params=_params(("parallel",)),
        name="merge_groups",
    )(*outs, *lses)


def _indexer_prep_kernel(lat_ref, gq_ref, lnw_ref, lnb_ref, cos_ref, sin_ref, iqn_ref, ik_ref, iw_ref):
    iqn_ref[...] = _rms(lat_ref[:, :IDX_Q_RANK], gq_ref[...]).astype(iqn_ref.dtype)
    k = lat_ref[:, IDX_Q_RANK:IDX_Q_RANK + IDX_DIM]
    kc = k - jnp.mean(k, axis=-1, keepdims=True)
    kn = kc * lax.rsqrt(jnp.mean(kc * kc, axis=-1, keepdims=True) + NORM_EPS) * lnw_ref[...] + lnb_ref[...]
    ik_ref[...] = _rope(kn, cos_ref[...], sin_ref[...]).astype(ik_ref.dtype)
    iw_ref[...] = lat_ref[:, IDX_Q_RANK + IDX_DIM:] * (IDX_HEADS ** -0.5 * IDX_DIM ** -0.5)


def indexer_prep(lat, gq, lnw, lnb, cos2, sin2):
    rows, width = lat.shape
    tm = _tile(rows, 512)
    row = lambda wd: pl.BlockSpec((tm, wd), lambda i: (i, 0))
    vec = lambda wd: pl.BlockSpec((1, wd), lambda i: (0, 0))
    return pl.pallas_call(
        _indexer_prep_kernel,
        out_shape=[jax.ShapeDtypeStruct((rows, IDX_Q_RANK), BF16),
                   jax.ShapeDtypeStruct((rows, IDX_DIM), BF16),
                   jax.ShapeDtypeStruct((rows, LANES), F32)],
        grid=(rows // tm,),
        in_specs=[row(width), vec(IDX_Q_RANK), vec(IDX_DIM), vec(IDX_DIM), row(HEAD_DIM), row(HEAD_DIM)],
        out_specs=[row(IDX_Q_RANK), row(IDX_DIM), row(LANES)],
        compiler_params=_params(("parallel",)),
        name="indexer_prep",
    )(lat, gq.reshape(1, -1), lnw.reshape(1, -1), lnb.reshape(1, -1), cos2, sin2)


def _ordered_bits(x):
    bits = lax.bitcast_convert_type(x, jnp.int32)
    return jnp.where(bits >= 0, bits, bits ^ jnp.int32(0x7FFFFFFF))


def _score_of(key):
    return lax.bitcast_convert_type(jnp.where(key >= 0, key, key ^ jnp.int32(0x7FFFFFFF)), F32)


IDX_TQ = 128
IDX_TS = 512
IDX_HEAD_BATCH = 8
IDX_SCORE_STEPS = 24


def _indexer_kernel(iq_ref, ik_ref, iw_ref, bias_ref, key_ref, qs_ref, wb_ref, *, n_sel):
    tq, seq = key_ref.shape
    i = pl.program_id(1)
    n_chunks = ((i + 1) * tq + IDX_TS - 1) // IDX_TS
    key_ref[...] = jnp.full((tq, seq), INT_MIN, jnp.int32)
    iw = iw_ref[...]
    for h in range(IDX_HEADS):
        qs_ref[h * tq:(h + 1) * tq, :] = iq_ref[:, h * IDX_DIM:(h + 1) * IDX_DIM]
        wb_ref[h] = jnp.broadcast_to(iw[:, h:h + 1], (tq, LANES))
    t_pos = i * tq + lax.broadcasted_iota(jnp.int32, (tq, IDX_TS), 0)
    lane = lax.broadcasted_iota(jnp.int32, (tq, IDX_TS), 1)
    hb_rows = IDX_HEAD_BATCH * tq

    def score_chunk(c, carry):
        start = pl.multiple_of(c * IDX_TS, IDX_TS)
        keys = ik_ref[pl.ds(start, IDX_TS), :]
        acc = jnp.zeros((tq, IDX_TS), F32)
        for hb in range(IDX_HEADS // IDX_HEAD_BATCH):
            rel = lax.dot_general(qs_ref[hb * hb_rows:(hb + 1) * hb_rows, :], keys, NT_DIMS,
                                  preferred_element_type=F32)
            for hh in range(IDX_HEAD_BATCH):
                wb = _lanes(wb_ref[hb * IDX_HEAD_BATCH + hh], IDX_TS)
                acc += jnp.maximum(rel[hh * tq:(hh + 1) * tq], 0.0) * wb
        causal = start + lane <= t_pos
        key_ref[:, pl.ds(start, IDX_TS)] = jnp.where(causal, _ordered_bits(acc), INT_MIN)
        hi_part = jnp.where(causal, acc, -jnp.inf)
        lo_part = jnp.where(causal, acc, jnp.inf)
        smax, smin = carry
        for b in range(IDX_TS // LANES):
            smax = jnp.maximum(smax, hi_part[:, b * LANES:(b + 1) * LANES])
            smin = jnp.minimum(smin, lo_part[:, b * LANES:(b + 1) * LANES])
        return smax, smin

    smax, smin = lax.fori_loop(0, n_chunks, score_chunk,
                               (jnp.full((tq, LANES), -jnp.inf, F32), jnp.full((tq, LANES), jnp.inf, F32)))
    smax = jnp.max(smax, axis=1, keepdims=True)
    smin = jnp.min(smin, axis=1, keepdims=True)

    def count_ge(cand):
        cand_w = _lanes(cand, IDX_TS)

        def body(c, cnt):
            start = pl.multiple_of(c * IDX_TS, IDX_TS)
            ge = jnp.where(key_ref[:, pl.ds(start, IDX_TS)] >= cand_w, 1.0, 0.0)
            for b in range(IDX_TS // LANES):
                cnt = cnt + ge[:, b * LANES:(b + 1) * LANES]
            return cnt

        cnt = lax.fori_loop(0, n_chunks, body, jnp.zeros((tq, LANES), F32))
        return jnp.sum(cnt, axis=1, keepdims=True)

    ones_i = jnp.ones((tq, LANES), jnp.int32)
    n_causal = i * tq + 1 + lax.broadcasted_iota(jnp.int32, (tq, LANES), 0)
    lo0 = _ordered_bits(smin) * ones_i
    hi0 = _ordered_bits(smax) * ones_i + 1
    open0 = jnp.where(n_causal > n_sel, 1.0, 0.0)
    thr0 = jnp.full((tq, LANES), INT_MIN + 1, jnp.int32)

    def search_cond(state):
        step, _, n_open = state
        return jnp.logical_and(step < IDX_SCORE_STEPS + 33, n_open > 0.0)

    def search_step(state):
        step, (lo, hi, thr, still_open), _ = state
        score_mid = _ordered_bits(0.5 * _score_of(lo) + 0.5 * _score_of(hi))
        key_mid = (lo >> 1) + (hi >> 1) + (lo & hi & 1)
        cand = jnp.where(step < IDX_SCORE_STEPS, score_mid, key_mid)
        cand = jnp.maximum(jnp.minimum(cand, hi - 1), lo + 1)
        cnt = count_ge(cand)
        enough = cnt >= n_sel
        lo = jnp.where(enough, cand, lo)
        hi = jnp.where(enough, hi, cand)
        settled = jnp.logical_or(cnt == n_sel, hi <= lo + 1)
        thr = jnp.where(jnp.logical_and(still_open > 0.0, settled), lo, thr)
        still_open = jnp.where(settled, 0.0, still_open)
        return step + 1, (lo, hi, thr, still_open), jnp.max(still_open)

    init = (lo0, hi0, thr0, open0)
    _, (_, _, thr, _), _ = lax.while_loop(search_cond, search_step, (jnp.int32(0), init, jnp.max(open0)))
    thr_w = _lanes(thr, IDX_TS)
    for c in range(seq // IDX_TS):
        sl = slice(c * IDX_TS, (c + 1) * IDX_TS)
        bias_ref[:, sl] = jnp.where(key_ref[:, sl] >= thr_w, 0.0, MASK_VALUE).astype(bias_ref.dtype)


def indexer_mask(iq, ik, iw, batch, seq):
    n_sel = min(TOPK_MAX, seq // 4)
    tq = IDX_TQ
    assert seq % IDX_TS == 0 and seq % tq == 0
    return pl.pallas_call(
        functools.partial(_indexer_kernel, n_sel=n_sel),
        out_shape=jax.ShapeDtypeStruct((batch, seq, seq), BF16),
        grid=(batch, seq // tq),
        in_specs=[pl.BlockSpec((None, tq, IDX_HEADS * IDX_DIM), lambda b, i: (b, i, 0)),
                  pl.BlockSpec((None, seq, IDX_DIM), lambda b, i: (b, 0, 0)),
                  pl.BlockSpec((None, tq, LANES), lambda b, i: (b, i, 0))],
        out_specs=pl.BlockSpec((None, tq, seq), lambda b, i: (b, i, 0)),
        scratch_shapes=[pltpu.VMEM((tq, seq), jnp.int32),
                        pltpu.VMEM((IDX_HEADS * tq, IDX_DIM), BF16),
                        pltpu.VMEM((IDX_HEADS, tq, LANES), F32)],
        compiler_params=_params(("parallel", "arbitrary")),
        name="indexer_mask",
    )(iq.reshape(batch, seq, -1), ik.reshape(batch, seq, -1), iw.reshape(batch, seq, -1))


B_TQ = 256
B_TS = 512
B_REP = B_HEADS // B_KV_HEADS
B_CHUNK = 128
B_UNIT = 512


def _sparse_attn_kernel(qblk_ref, kblk_ref, q_ref, k_ref, v_ref, bias_ref, o_ref,
                        qs_ref, vo_ref, biasf_ref, s_ref, p_ref, alpha_ref, m_ref, accl_ref):
    tq, ts = bias_ref.shape
    rows = B_REP * tq
    step = pl.program_id(1)
    i, j = qblk_ref[step], kblk_ref[step]
    last = ((i + 1) * tq - 1) // ts

    @pl.when(j == 0)
    def _():
        scale = HEAD_DIM ** -0.5 * LOG2E
        for g in range(B_KV_HEADS):
            for r in range(B_REP):
                h = g * B_REP + r
                qh = q_ref[:, h * HEAD_DIM:(h + 1) * HEAD_DIM].astype(F32) * scale
                qs_ref[g, r * tq:(r + 1) * tq, :] = qh.astype(qs_ref.dtype)
        m_ref[...] = jnp.full(m_ref.shape, MASK_VALUE, F32)
        accl_ref[...] = jnp.zeros(accl_ref.shape, F32)

    biasf_ref[...] = bias_ref[...].astype(F32)
    ones = jnp.ones((ts, LANES), BF16)
    for g in range(B_KV_HEADS):
        vo_ref[g, :, :HEAD_DIM] = v_ref[:, g * HEAD_DIM:(g + 1) * HEAD_DIM]
        vo_ref[g, :, HEAD_DIM:] = ones

    unit_rows = min(B_UNIT, rows)
    units = [(g, r0) for g in range(B_KV_HEADS) for r0 in range(0, rows, unit_rows)]

    def logits(u):
        g, r0 = units[u]
        kg = k_ref[:, g * HEAD_DIM:(g + 1) * HEAD_DIM]
        s_ref[u % 2] = lax.dot_general(qs_ref[g, r0:r0 + unit_rows, :], kg, NT_DIMS,
                                       preferred_element_type=F32)

    logits(0)
    for u, (g, r0) in enumerate(units):
        if u + 1 < len(units):
            logits(u + 1)
        par = u % 2
        for c in range(0, unit_rows, B_CHUNK):
            rs = slice(c, c + B_CHUNK)
            gs = slice(r0 + c, r0 + c + B_CHUNK)
            qrow = (r0 + c) % tq
            s = s_ref[par, rs, :] + biasf_ref[qrow:qrow + B_CHUNK, :]
            m_prev = m_ref[g, gs, :]
            m_new = jnp.maximum(m_prev, jnp.max(s, axis=1, keepdims=True))
            p_ref[par, rs, :] = jnp.exp2(s - _lanes(m_new, ts)).astype(BF16)
            alpha_ref[par, rs, :] = jnp.exp2(m_prev - m_new)
            m_ref[g, gs, :] = m_new
        pv = jnp.dot(p_ref[par], vo_ref[g], preferred_element_type=F32)
        alpha = alpha_ref[par]
        us = slice(r0, r0 + unit_rows)
        accl_ref[g, us, :] = jnp.concatenate([alpha, alpha], axis=1) * accl_ref[g, us, :] + pv

    @pl.when(j == last)
    def _():
        for g in range(B_KV_HEADS):
            out = accl_ref[g, :, :HEAD_DIM] / accl_ref[g, :, HEAD_DIM:]
            for r in range(B_REP):
                h = g * B_REP + r
                o_ref[:, h * HEAD_DIM:(h + 1) * HEAD_DIM] = out[r * tq:(r + 1) * tq].astype(o_ref.dtype)


def sparse_attention(qkv, bias, batch, seq):
    tq, ts = _tile(seq, B_TQ), _tile(seq, B_TS)
    view = qkv.reshape(batch, seq, B_QKV_WIDTH)
    k_tile = B_Q_WIDTH // B_KV_WIDTH
    pairs = [(i, j) for i in range(seq // tq) for j in range(((i + 1) * tq - 1) // ts + 1)]
    qblk = jnp.asarray([p[0] for p in pairs], jnp.int32)
    kblk = jnp.asarray([p[1] for p in pairs], jnp.int32)
    rows = B_REP * tq
    unit_rows = min(B_UNIT, rows)
    grid_spec = pltpu.PrefetchScalarGridSpec(
        num_scalar_prefetch=2,
        grid=(batch, len(pairs)),
        in_specs=[pl.BlockSpec((None, tq, B_Q_WIDTH), lambda b, s, qb, kb: (b, qb[s], 0)),
                  pl.BlockSpec((None, ts, B_KV_WIDTH), lambda b, s, qb, kb: (b, kb[s], k_tile)),
                  pl.BlockSpec((None, ts, B_KV_WIDTH), lambda b, s, qb, kb: (b, kb[s], k_tile + 1)),
                  pl.BlockSpec((None, tq, ts), lambda b, s, qb, kb: (b, qb[s], kb[s]))],
        out_specs=pl.BlockSpec((None, tq, B_Q_WIDTH), lambda b, s, qb, kb: (b, qb[s], 0)),
        scratch_shapes=[pltpu.VMEM((B_KV_HEADS, rows, HEAD_DIM), BF16),
                        pltpu.VMEM((B_KV_HEADS, ts, 2 * HEAD_DIM), BF16),
                        pltpu.VMEM((tq, ts), F32),
                        pltpu.VMEM((2, unit_rows, ts), F32),
                        pltpu.VMEM((2, unit_rows, ts), BF16),
                        pltpu.VMEM((2, unit_rows, LANES), F32),
                        pltpu.VMEM((B_KV_HEADS, rows, LANES), F32),
                        pltpu.VMEM((B_KV_HEADS, rows, 2 * HEAD_DIM), F32)])
    out = pl.pallas_call(
        _sparse_attn_kernel,
        out_shape=jax.ShapeDtypeStruct((batch, seq, B_Q_WIDTH), BF16),
        grid_spec=grid_spec,
        compiler_params=_params(("parallel", "arbitrary")),
        name="sparse_attention",
    )(qblk, kblk, view, view, view, bias)
    return out.reshape(batch * seq, B_Q_WIDTH)


def _rope_tables(positions):
    inv_freq = ROPE_THETA ** (-jnp.arange(0, HEAD_DIM, 2, dtype=F32) / HEAD_DIM)
    ang = positions.astype(F32).reshape(-1, 1) * inv_freq
    cos, sin = jnp.cos(ang), jnp.sin(ang)
    return jnp.concatenate([cos, cos], axis=-1), jnp.concatenate([-sin, sin], axis=-1)


def _mlp(f_in, w_up, w_down, layer):
    u = matmul(f_in, cast_weight(w_up, layer), act="relu2")
    return matmul(u, cast_weight(w_down, layer))


def _mixer_dilated(a, cos2, sin2, w_in, w_out, layer, batch, seq):
    tn = 1024
    per_part = A_GROUP_WIDTH // tn
    rope_pred = lambda j: (j // per_part) % 3 != 2
    a, cos2, sin2 = (_to_residue_major(t, batch, seq) for t in (a, cos2, sin2))
    qkv = matmul(a, cast_weight(w_in, layer), rope=(cos2, sin2, rope_pred), tn=tn)
    outs, lses = zip(*[dilated_attention(qkv, batch, seq, g) for g in range(N_GROUPS)])
    o = _from_residue_major(merge_groups(outs, lses), batch, seq)
    return matmul(o, cast_weight(w_out, layer))


def _mixer_sparse(a, cos2, sin2, w_in, idx_q_norm, w_idx_up, idx_k_w, idx_k_b, w_out, layer, batch, seq):
    tn = 1024
    n_rope = (B_Q_WIDTH + B_KV_WIDTH) // tn
    idx_width = IDX_Q_RANK + IDX_DIM + LANES
    w_all = cast_weight(w_in, layer, n_pad=B_QKV_WIDTH + idx_width)
    qkv = matmul(a, w_all, cols=(0, B_QKV_WIDTH), rope=(cos2, sin2, lambda j: j < n_rope), tn=tn)
    lat = matmul(a, w_all, cols=(B_QKV_WIDTH, idx_width), out_dtype=F32, tn=256)
    iq_n, ik, iw = indexer_prep(lat, idx_q_norm[layer], idx_k_w[layer], idx_k_b[layer], cos2, sin2)
    iq = matmul(iq_n, cast_weight(w_idx_up, layer), rope=(cos2, sin2, lambda j: j >= 0), tn=tn)
    bias = indexer_mask(iq, ik, iw, batch, seq)
    o = sparse_attention(qkv, bias, batch, seq)
    return matmul(o, cast_weight(w_out, layer))


def kernel(x, positions, attn_pre_norm, attn_post_norm, mlp_pre_norm, mlp_post_norm, w_in_a, w_out_a,
           w_in_b, idx_q_norm, w_idx_up, idx_k_norm_w, idx_k_norm_b, w_out_b, w_up, w_down):
    batch, seq, d_model = x.shape
    depth = attn_pre_norm.shape[0]
    cos2, sin2 = _rope_tables(positions)
    h = x.reshape(batch * seq, d_model)
    a = rms_norm_cast(h, attn_pre_norm[0])
    for i in range(depth):
        j = i // 2
        if i % 2 == 0:
            m = _mixer_dilated(a, cos2, sin2, w_in_a, w_out_a, j, batch, seq)
        else:
            m = _mixer_sparse(a, cos2, sin2, w_in_b, idx_q_norm, w_idx_up,
                              idx_k_norm_w, idx_k_norm_b, w_out_b, j, batch, seq)
        h, f_in = residual_norm(h, m, attn_post_norm[i], mlp_pre_norm[i])
        f = _mlp(f_in, w_up, w_down, i)
        h, a = residual_norm(h, f, mlp_post_norm[i], attn_pre_norm[i + 1] if i + 1 < depth else None)
    return h.reshape(batch, seq, d_model)
```

```python
import functools
import math

import jax
import jax.numpy as jnp
from jax import lax
from jax.experimental import pallas as pl
from jax.experimental.pallas import tpu as pltpu

HEAD_DIM = 128
ROPE_THETA = 10000.0
NORM_EPS = 1e-6
DILATED_GROUPS = ((128, 1), (512, 4), (2048, 16))
N_GROUPS = 3
A_HEADS_PER_GROUP = 16
A_BLOCK = 128
A_GROUP_WIDTH = A_HEADS_PER_GROUP * HEAD_DIM
B_HEADS = 32
B_KV_HEADS = 8
IDX_HEADS = 32
IDX_DIM = 128
IDX_Q_RANK = 1024
TOPK_MAX = 256
B_Q_WIDTH = B_HEADS * HEAD_DIM
B_KV_WIDTH = B_KV_HEADS * HEAD_DIM
B_QKV_WIDTH = B_Q_WIDTH + 2 * B_KV_WIDTH

LANES = 128
VMEM_LIMIT_BYTES = 56 * 2**20
MASK_VALUE = -1e30
INT_MIN = -2**31
LOG2E = math.log2(math.e)
NT_DIMS = (((1,), (1,)), ((), ()))

F32 = jnp.float32
BF16 = jnp.bfloat16


def _tile(dim, pref):
    t = min(dim, pref)
    if dim % t:
        t = 1 << (t.bit_length() - 1)
    while dim % t:
        t //= 2
    return t


def _params(semantics):
    return pltpu.CompilerParams(dimension_semantics=semantics, vmem_limit_bytes=VMEM_LIMIT_BYTES)


def _lanes(x, width):
    return jnp.concatenate([x] * (width // LANES), axis=1)


def _rms(x, g):
    return x * lax.rsqrt(jnp.mean(x * x, axis=-1, keepdims=True) + NORM_EPS) * g


def _norm_kernel(x_ref, g_ref, a_ref):
    a_ref[...] = _rms(x_ref[...], g_ref[...]).astype(a_ref.dtype)


def rms_norm_cast(x, g):
    m, d = x.shape
    tm = _tile(m, 256)
    return pl.pallas_call(
        _norm_kernel,
        out_shape=jax.ShapeDtypeStruct((m, d), BF16),
        grid=(m // tm,),
        in_specs=[pl.BlockSpec((tm, d), lambda i: (i, 0)), pl.BlockSpec((1, d), lambda i: (0, 0))],
        out_specs=pl.BlockSpec((tm, d), lambda i: (i, 0)),
        compiler_params=_params(("parallel",)),
        name="rms_norm_cast",
    )(x, g.reshape(1, d))


def _residual_kernel(h_ref, m_ref, gpost_ref, *rest, with_next):
    h = h_ref[...] + _rms(m_ref[...].astype(F32), gpost_ref[...])
    if with_next:
        gpre_ref, h_out_ref, a_ref = rest
        a_ref[...] = _rms(h, gpre_ref[...]).astype(a_ref.dtype)
    else:
        (h_out_ref,) = rest
    h_out_ref[...] = h


def residual_norm(h, m, g_post, g_pre_next=None):
    rows, d = h.shape
    tm = _tile(rows, 256)
    with_next = g_pre_next is not None
    row_spec = pl.BlockSpec((tm, d), lambda i: (i, 0))
    g_spec = pl.BlockSpec((1, d), lambda i: (0, 0))
    args = [h, m, g_post.reshape(1, d)]
    in_specs = [row_spec, row_spec, g_spec]
    out_shape = [jax.ShapeDtypeStruct((rows, d), F32)]
    out_specs = [row_spec]
    if with_next:
        args.append(g_pre_next.reshape(1, d))
        in_specs.append(g_spec)
        out_shape.append(jax.ShapeDtypeStruct((rows, d), BF16))
        out_specs.append(row_spec)
    out = pl.pallas_call(
        functools.partial(_residual_kernel, with_next=with_next),
        out_shape=out_shape,
        grid=(rows // tm,),
        in_specs=in_specs,
        out_specs=out_specs,
        compiler_params=_params(("parallel",)),
        name="residual_norm",
    )(*args)
    return (out[0], out[1]) if with_next else (out[0], None)


CAST_BLOCK_BYTES = 8 * 2**20


def _cast_kernel(w_ref, o_ref):
    n = w_ref.shape[1]
    o_ref[:, :n] = w_ref[...].astype(o_ref.dtype)
    if o_ref.shape[1] > n:
        o_ref[:, n:] = jnp.zeros((o_ref.shape[0], o_ref.shape[1] - n), o_ref.dtype)


def cast_weight(w, layer, n_pad=None):
    _, kdim, n = w.shape
    n_pad = n if n_pad is None else n_pad
    tk = _tile(kdim, max(8, CAST_BLOCK_BYTES // (4 * n)))
    return pl.pallas_call(
        _cast_kernel,
        out_shape=jax.ShapeDtypeStruct((kdim, n_pad), BF16),
        grid=(kdim // tk,),
        in_specs=[pl.BlockSpec((None, tk, n), lambda i: (layer, i, 0))],
        out_specs=pl.BlockSpec((tk, n_pad), lambda i: (i, 0)),
        compiler_params=_params(("parallel",)),
        name="cast_weight",
    )(w)


MM_CHUNK = 256


def _rope(x, cos2, sin2):
    return x * cos2 + pltpu.roll(x, HEAD_DIM // 2, 1) * sin2


def _matmul_kernel(a_ref, w_ref, *rest, nk, act, rope_pred, n_split):
    if rope_pred is not None:
        cos_ref, sin_ref, *rest = rest
        do_rope = rope_pred(pl.program_id(1))
        cos2 = jnp.where(do_rope, cos_ref[...], 1.0)
        sin2 = jnp.where(do_rope, sin_ref[...], 0.0)
    o_ref, *scratch = rest
    sub = o_ref.shape[1] // n_split
    if nk > 1:
        (acc_ref,) = scratch

        @pl.when(pl.program_id(2) == 0)
        def _():
            acc_ref[...] = jnp.zeros(acc_ref.shape, F32)

    for c in range(n_split):
        acc = jnp.dot(a_ref[...], w_ref[:, c * sub:(c + 1) * sub], preferred_element_type=F32)
        if nk > 1:
            acc = acc_ref[c] + acc
            acc_ref[c] = acc
        if act == "relu2":
            r = jnp.maximum(acc, 0.0)
            acc = r * r
        if rope_pred is None:
            o_ref[:, c * sub:(c + 1) * sub] = acc.astype(o_ref.dtype)
        else:
            for hh in range(sub // HEAD_DIM):
                x = acc[:, hh * HEAD_DIM:(hh + 1) * HEAD_DIM]
                col = c * sub + hh * HEAD_DIM
                o_ref[:, col:col + HEAD_DIM] = _rope(x, cos2, sin2).astype(o_ref.dtype)


def matmul(a, w, *, cols=None, out_dtype=BF16, act=None, rope=None, tm=1024, tn=1024, tk=4096):
    m, kdim = a.shape
    col0, n = (0, w.shape[1]) if cols is None else cols
    tm, tn, tk = _tile(m, tm), _tile(n, tn), _tile(kdim, tk)
    assert col0 % tn == 0
    j0 = col0 // tn
    nk = kdim // tk
    in_specs = [pl.BlockSpec((tm, tk), lambda i, j, k: (i, k)),
                pl.BlockSpec((tk, tn), lambda i, j, k: (k, j + j0))]
    args = [a, w]
    rope_pred = None
    if rope is not None:
        cos2, sin2, rope_pred = rope
        tab_spec = pl.BlockSpec((tm, HEAD_DIM), lambda i, j, k: (i, 0))
        in_specs += [tab_spec, tab_spec]
        args += [cos2, sin2]
    n_split = max(1, tn // MM_CHUNK)
    scratch = [pltpu.VMEM((n_split, tm, tn // n_split), F32)] if nk > 1 else []
    return pl.pallas_call(
        functools.partial(_matmul_kernel, nk=nk, act=act, rope_pred=rope_pred, n_split=n_split),
        out_shape=jax.ShapeDtypeStruct((m, n), out_dtype),
        grid=(m // tm, n // tn, nk),
        in_specs=in_specs,
        out_specs=pl.BlockSpec((tm, tn), lambda i, j, k: (i, j)),
        scratch_shapes=scratch,
        compiler_params=_params(("parallel", "parallel", "arbitrary")),
        name="matmul",
    )(*args)


A_PERIOD = 16


def _to_residue_major(x, batch, seq):
    u = seq // A_PERIOD
    return x.reshape(batch, u, 4, 4, -1).transpose(0, 3, 2, 1, 4).reshape(batch * seq, -1)


def _from_residue_major(x, batch, seq):
    u = seq // A_PERIOD
    return x.reshape(batch, 4, 4, u, -1).transpose(0, 3, 2, 1, 4).reshape(batch * seq, -1)


def _dilated_kernel(q_ref, kp_ref, kc_ref, vp_ref, vc_ref, o_ref, lse_ref, sp_ref, sc_ref,
                    *, w, rows, local_index, blk_axis):
    has_prev = pl.program_id(blk_axis) > 0
    qi = local_index(lax.broadcasted_iota(jnp.int32, (rows, rows), 0))
    kj = local_index(lax.broadcasted_iota(jnp.int32, (rows, rows), 1))
    mask_prev = jnp.logical_and(qi + rows - kj <= w, has_prev)
    mask_cur = jnp.logical_and(kj <= qi, qi - kj <= w)
    scale = HEAD_DIM ** -0.5 * LOG2E
    lead = (slice(None),) * (len(q_ref.shape) - 1)
    out_shape = o_ref.shape[:-1] + (HEAD_DIM,)

    def head(ref, h):
        return ref[lead + (slice(h * HEAD_DIM, (h + 1) * HEAD_DIM),)].reshape(rows, HEAD_DIM)

    def stage(h):
        q = head(q_ref, h)
        sp_ref[h % 2] = lax.dot_general(q, head(kp_ref, h), NT_DIMS, preferred_element_type=F32)
        sc_ref[h % 2] = lax.dot_general(q, head(kc_ref, h), NT_DIMS, preferred_element_type=F32)

    stage(0)
    for h in range(A_HEADS_PER_GROUP):
        if h + 1 < A_HEADS_PER_GROUP:
            stage(h + 1)
        s_prev = jnp.where(mask_prev, sp_ref[h % 2] * scale, MASK_VALUE)
        s_cur = jnp.where(mask_cur, sc_ref[h % 2] * scale, MASK_VALUE)
        m = jnp.maximum(jnp.max(s_prev, axis=1, keepdims=True), jnp.max(s_cur, axis=1, keepdims=True))
        p_prev = jnp.exp2(s_prev - m)
        p_cur = jnp.exp2(s_cur - m)
        l = jnp.sum(p_prev, axis=1, keepdims=True) + jnp.sum(p_cur, axis=1, keepdims=True)
        o = jnp.dot(p_prev.astype(BF16), head(vp_ref, h), preferred_element_type=F32)
        o += jnp.dot(p_cur.astype(BF16), head(vc_ref, h), preferred_element_type=F32)
        idx = lead + (slice(h * HEAD_DIM, (h + 1) * HEAD_DIM),)
        o_ref[idx] = (o / l).astype(o_ref.dtype).reshape(out_shape)
        lse_ref[idx] = jnp.broadcast_to(m * (1.0 / LOG2E) + jnp.log(l), (rows, HEAD_DIM)).reshape(out_shape)


def dilated_attention(qkv, batch, seq, group):
    window, r = DILATED_GROUPS[group]
    w = window // r
    u = seq // A_PERIOD
    width = qkv.shape[1]
    view = qkv.reshape(batch, 4, 4, u, width)
    if r == 16:
        rows, n_blk = A_BLOCK, u // A_BLOCK
        block = (None, None, None, rows, A_GROUP_WIDTH)
        grid = (batch, 4, 4, n_blk)
        place = lambda g, blk: (g[0], g[1], g[2], blk)
        local_index = lambda rho: rho
    elif r == 4:
        rows, n_blk = A_BLOCK, u // (A_BLOCK // 4)
        block = (None, None, 4, rows // 4, A_GROUP_WIDTH)
        grid = (batch, 4, n_blk)
        place = lambda g, blk: (g[0], g[1], 0, blk)
        local_index = lambda rho: 4 * (rho % (rows // 4)) + rho // (rows // 4)
    else:
        assert r == 1
        rows, n_blk = 2 * A_BLOCK, u // (2 * A_BLOCK // A_PERIOD)
        per = rows // A_PERIOD
        block = (None, 4, 4, per, A_GROUP_WIDTH)
        grid = (batch, n_blk)
        place = lambda g, blk: (g[0], 0, 0, blk)
        local_index = lambda rho: A_PERIOD * (rho % per) + 4 * ((rho // per) % 4) + rho // (4 * per)
    assert w <= rows and n_blk * rows * r == seq

    def spec(which, prev):
        def index(*g):
            blk = jnp.maximum(g[-1] - 1, 0) if prev else g[-1]
            return place(g, blk) + (group * 3 + which,)
        return pl.BlockSpec(block, index)

    out_spec = pl.BlockSpec(block, lambda *g: place(g, g[-1]) + (0,))
    o, lse = pl.pallas_call(
        functools.partial(_dilated_kernel, w=w, rows=rows, local_index=local_index, blk_axis=len(grid) - 1),
        out_shape=[jax.ShapeDtypeStruct((batch, 4, 4, u, A_GROUP_WIDTH), BF16),
                   jax.ShapeDtypeStruct((batch, 4, 4, u, A_GROUP_WIDTH), F32)],
        grid=grid,
        in_specs=[spec(0, False), spec(1, True), spec(1, False), spec(2, True), spec(2, False)],
        out_specs=[out_spec, out_spec],
        scratch_shapes=[pltpu.VMEM((2, rows, rows), F32),
                        pltpu.VMEM((2, rows, rows), F32)],
        compiler_params=_params(("parallel",) * (len(grid) - 1) + ("arbitrary",)),
        name="dilated_attention",
    )(view, view, view, view, view)
    return o.reshape(batch * seq, A_GROUP_WIDTH), lse.reshape(batch * seq, A_GROUP_WIDTH)


def _merge_kernel(o0_ref, o1_ref, o2_ref, l0_ref, l1_ref, l2_ref, out_ref):
    l0, l1, l2 = l0_ref[...], l1_ref[...], l2_ref[...]
    m = jnp.maximum(jnp.maximum(l0, l1), l2)
    e0, e1, e2 = jnp.exp(l0 - m), jnp.exp(l1 - m), jnp.exp(l2 - m)
    num = e0 * o0_ref[...].astype(F32) + e1 * o1_ref[...].astype(F32) + e2 * o2_ref[...].astype(F32)
    out_ref[...] = (num / (e0 + e1 + e2)).astype(out_ref.dtype)


def merge_groups(outs, lses):
    rows, width = outs[0].shape
    tm = _tile(rows, 256)
    spec = pl.BlockSpec((tm, width), lambda i: (i, 0))
    return pl.pallas_call(
        _merge_kernel,
        out_shape=jax.ShapeDtypeStruct((rows, width), BF16),
        grid=(rows // tm,),
        in_specs=[spec] * 6,
        out_specs=spec,
        compiler_params=_params(("parallel",)),
        name="merge_groups",
    )(*outs, *lses)


def _indexer_prep_kernel(lat_ref, gq_ref, lnw_ref, lnb_ref, cos_ref, sin_ref, iqn_ref, ik_ref, iw_ref):
    iqn_ref[...] = _rms(lat_ref[:, :IDX_Q_RANK], gq_ref[...]).astype(iqn_ref.dtype)
    k = lat_ref[:, IDX_Q_RANK:IDX_Q_RANK + IDX_DIM]
    kc = k - jnp.mean(k, axis=-1, keepdims=True)
    kn = kc * lax.rsqrt(jnp.mean(kc * kc, axis=-1, keepdims=True) + NORM_EPS) * lnw_ref[...] + lnb_ref[...]
    ik_ref[...] = _rope(kn, cos_ref[...], sin_ref[...]).astype(ik_ref.dtype)
    iw_ref[...] = lat_ref[:, IDX_Q_RANK + IDX_DIM:] * (IDX_HEADS ** -0.5 * IDX_DIM ** -0.5)


def indexer_prep(lat, gq, lnw, lnb, cos2, sin2):
    rows, width = lat.shape
    tm = _tile(rows, 512)
    row = lambda wd: pl.BlockSpec((tm, wd), lambda i: (i, 0))
    vec = lambda wd: pl.BlockSpec((1, wd), lambda i: (0, 0))
    return pl.pallas_call(
        _indexer_prep_kernel,
        out_shape=[jax.ShapeDtypeStruct((rows, IDX_Q_RANK), BF16),
                   jax.ShapeDtypeStruct((rows, IDX_DIM), BF16),
                   jax.ShapeDtypeStruct((rows, LANES), F32)],
        grid=(rows // tm,),
        in_specs=[row(width), vec(IDX_Q_RANK), vec(IDX_DIM), vec(IDX_DIM), row(HEAD_DIM), row(HEAD_DIM)],
        out_specs=[row(IDX_Q_RANK), row(IDX_DIM), row(LANES)],
        compiler_params=_params(("parallel",)),
        name="indexer_prep",
    )(lat, gq.reshape(1, -1), lnw.reshape(1, -1), lnb.reshape(1, -1), cos2, sin2)


def _ordered_bits(x):
    bits = lax.bitcast_convert_type(x, jnp.int32)
    return jnp.where(bits >= 0, bits, bits ^ jnp.int32(0x7FFFFFFF))


def _score_of(key):
    return lax.bitcast_convert_type(jnp.where(key >= 0, key, key ^ jnp.int32(0x7FFFFFFF)), F32)


IDX_TQ = 128
IDX_TS = 512
IDX_HEAD_BATCH = 8
IDX_SCORE_STEPS = 24


def _indexer_kernel(iq_ref, ik_ref, iw_ref, bias_ref, key_ref, qs_ref, wb_ref, *, n_sel):
    tq, seq = key_ref.shape
    i = pl.program_id(1)
    n_chunks = ((i + 1) * tq + IDX_TS - 1) // IDX_TS
    key_ref[...] = jnp.full((tq, seq), INT_MIN, jnp.int32)
    iw = iw_ref[...]
    for h in range(IDX_HEADS):
        qs_ref[h * tq:(h + 1) * tq, :] = iq_ref[:, h * IDX_DIM:(h + 1) * IDX_DIM]
        wb_ref[h] = jnp.broadcast_to(iw[:, h:h + 1], (tq, LANES))
    t_pos = i * tq + lax.broadcasted_iota(jnp.int32, (tq, IDX_TS), 0)
    lane = lax.broadcasted_iota(jnp.int32, (tq, IDX_TS), 1)
    hb_rows = IDX_HEAD_BATCH * tq

    def score_chunk(c, carry):
        start = pl.multiple_of(c * IDX_TS, IDX_TS)
        keys = ik_ref[pl.ds(start, IDX_TS), :]
        acc = jnp.zeros((tq, IDX_TS), F32)
        for hb in range(IDX_HEADS // IDX_HEAD_BATCH):
            rel = lax.dot_general(qs_ref[hb * hb_rows:(hb + 1) * hb_rows, :], keys, NT_DIMS,
                                  preferred_element_type=F32)
            for hh in range(IDX_HEAD_BATCH):
                wb = _lanes(wb_ref[hb * IDX_HEAD_BATCH + hh], IDX_TS)
                acc += jnp.maximum(rel[hh * tq:(hh + 1) * tq], 0.0) * wb
        causal = start + lane <= t_pos
        key_ref[:, pl.ds(start, IDX_TS)] = jnp.where(causal, _ordered_bits(acc), INT_MIN)
        hi_part = jnp.where(causal, acc, -jnp.inf)
        lo_part = jnp.where(causal, acc, jnp.inf)
        smax, smin = carry
        for b in range(IDX_TS // LANES):
            smax = jnp.maximum(smax, hi_part[:, b * LANES:(b + 1) * LANES])
            smin = jnp.minimum(smin, lo_part[:, b * LANES:(b + 1) * LANES])
        return smax, smin

    smax, smin = lax.fori_loop(0, n_chunks, score_chunk,
                               (jnp.full((tq, LANES), -jnp.inf, F32), jnp.full((tq, LANES), jnp.inf, F32)))
    smax = jnp.max(smax, axis=1, keepdims=True)
    smin = jnp.min(smin, axis=1, keepdims=True)

    def count_ge(cand):
        cand_w = _lanes(cand, IDX_TS)

        def body(c, cnt):
            start = pl.multiple_of(c * IDX_TS, IDX_TS)
            ge = jnp.where(key_ref[:, pl.ds(start, IDX_TS)] >= cand_w, 1.0, 0.0)
            for b in range(IDX_TS // LANES):
                cnt = cnt + ge[:, b * LANES:(b + 1) * LANES]
            return cnt

        cnt = lax.fori_loop(0, n_chunks, body, jnp.zeros((tq, LANES), F32))
        return jnp.sum(cnt, axis=1, keepdims=True)

    ones_i = jnp.ones((tq, LANES), jnp.int32)
    n_causal = i * tq + 1 + lax.broadcasted_iota(jnp.int32, (tq, LANES), 0)
    lo0 = _ordered_bits(smin) * ones_i
    hi0 = _ordered_bits(smax) * ones_i + 1
    open0 = jnp.where(n_causal > n_sel, 1.0, 0.0)
    thr0 = jnp.full((tq, LANES), INT_MIN + 1, jnp.int32)

    def search_cond(state):
        step, _, n_open = state
        return jnp.logical_and(step < IDX_SCORE_STEPS + 33, n_open > 0.0)

    def search_step(state):
        step, (lo, hi, thr, still_open), _ = state
        score_mid = _ordered_bits(0.5 * _score_of(lo) + 0.5 * _score_of(hi))
        key_mid = (lo >> 1) + (hi >> 1) + (lo & hi & 1)
        cand = jnp.where(step < IDX_SCORE_STEPS, score_mid, key_mid)
        cand = jnp.maximum(jnp.minimum(cand, hi - 1), lo + 1)
        cnt = count_ge(cand)
        enough = cnt >= n_sel
        lo = jnp.where(enough, cand, lo)
        hi = jnp.where(enough, hi, cand)
        settled = jnp.logical_or(cnt == n_sel, hi <= lo + 1)
        thr = jnp.where(jnp.logical_and(still_open > 0.0, settled), lo, thr)
        still_open = jnp.where(settled, 0.0, still_open)
        return step + 1, (lo, hi, thr, still_open), jnp.max(still_open)

    init = (lo0, hi0, thr0, open0)
    _, (_, _, thr, _), _ = lax.while_loop(search_cond, search_step, (jnp.int32(0), init, jnp.max(open0)))
    thr_w = _lanes(thr, IDX_TS)
    for c in range(seq // IDX_TS):
        sl = slice(c * IDX_TS, (c + 1) * IDX_TS)
        bias_ref[:, sl] = jnp.where(key_ref[:, sl] >= thr_w, 0.0, MASK_VALUE).astype(bias_ref.dtype)


def indexer_mask(iq, ik, iw, batch, seq):
    n_sel = min(TOPK_MAX, seq // 4)
    tq = IDX_TQ
    assert seq % IDX_TS == 0 and seq % tq == 0
    return pl.pallas_call(
        functools.partial(_indexer_kernel, n_sel=n_sel),
        out_shape=jax.ShapeDtypeStruct((batch, seq, seq), BF16),
        grid=(batch, seq // tq),
        in_specs=[pl.BlockSpec((None, tq, IDX_HEADS * IDX_DIM), lambda b, i: (b, i, 0)),
                  pl.BlockSpec((None, seq, IDX_DIM), lambda b, i: (b, 0, 0)),
                  pl.BlockSpec((None, tq, LANES), lambda b, i: (b, i, 0))],
        out_specs=pl.BlockSpec((None, tq, seq), lambda b, i: (b, i, 0)),
        scratch_shapes=[pltpu.VMEM((tq, seq), jnp.int32),
                        pltpu.VMEM((IDX_HEADS * tq, IDX_DIM), BF16),
                        pltpu.VMEM((IDX_HEADS, tq, LANES), F32)],
        compiler_params=_params(("parallel", "arbitrary")),
        name="indexer_mask",
    )(iq.reshape(batch, seq, -1), ik.reshape(batch, seq, -1), iw.reshape(batch, seq, -1))


B_TQ = 256
B_TS = 512
B_REP = B_HEADS // B_KV_HEADS
B_CHUNK = 128
B_UNIT = 512


def _sparse_attn_kernel(qblk_ref, kblk_ref, q_ref, k_ref, v_ref, bias_ref, o_ref,
                        qs_ref, vo_ref, biasf_ref, s_ref, p_ref, alpha_ref, m_ref, accl_ref):
    tq, ts = bias_ref.shape
    rows = B_REP * tq
    step = pl.program_id(1)
    i, j = qblk_ref[step], kblk_ref[step]
    last = ((i + 1) * tq - 1) // ts

    @pl.when(j == 0)
    def _():
        scale = HEAD_DIM ** -0.5 * LOG2E
        for g in range(B_KV_HEADS):
            for r in range(B_REP):
                h = g * B_REP + r
                qh = q_ref[:, h * HEAD_DIM:(h + 1) * HEAD_DIM].astype(F32) * scale
                qs_ref[g, r * tq:(r + 1) * tq, :] = qh.astype(qs_ref.dtype)
        m_ref[...] = jnp.full(m_ref.shape, MASK_VALUE, F32)
        accl_ref[...] = jnp.zeros(accl_ref.shape, F32)

    biasf_ref[...] = bias_ref[...].astype(F32)
    ones = jnp.ones((ts, LANES), BF16)
    for g in range(B_KV_HEADS):
        vo_ref[g, :, :HEAD_DIM] = v_ref[:, g * HEAD_DIM:(g + 1) * HEAD_DIM]
        vo_ref[g, :, HEAD_DIM:] = ones

    unit_rows = min(B_UNIT, rows)
    units = [(g, r0) for g in range(B_KV_HEADS) for r0 in range(0, rows, unit_rows)]

    def logits(u):
        g, r0 = units[u]
        kg = k_ref[:, g * HEAD_DIM:(g + 1) * HEAD_DIM]
        s_ref[u % 2] = lax.dot_general(qs_ref[g, r0:r0 + unit_rows, :], kg, NT_DIMS,
                                       preferred_element_type=F32)

    logits(0)
    for u, (g, r0) in enumerate(units):
        if u + 1 < len(units):
            logits(u + 1)
        par = u % 2
        for c in range(0, unit_rows, B_CHUNK):
            rs = slice(c, c + B_CHUNK)
            gs = slice(r0 + c, r0 + c + B_CHUNK)
            qrow = (r0 + c) % tq
            s = s_ref[par, rs, :] + biasf_ref[qrow:qrow + B_CHUNK, :]
            m_prev = m_ref[g, gs, :]
            m_new = jnp.maximum(m_prev, jnp.max(s, axis=1, keepdims=True))
            p_ref[par, rs, :] = jnp.exp2(s - _lanes(m_new, ts)).astype(BF16)
            alpha_ref[par, rs, :] = jnp.exp2(m_prev - m_new)
            m_ref[g, gs, :] = m_new
        pv = jnp.dot(p_ref[par], vo_ref[g], preferred_element_type=F32)
        alpha = alpha_ref[par]
        us = slice(r0, r0 + unit_rows)
        accl_ref[g, us, :] = jnp.concatenate([alpha, alpha], axis=1) * accl_ref[g, us, :] + pv

    @pl.when(j == last)
    def _():
        for g in range(B_KV_HEADS):
            out = accl_ref[g, :, :HEAD_DIM] / accl_ref[g, :, HEAD_DIM:]
            for r in range(B_REP):
                h = g * B_REP + r
                o_ref[:, h * HEAD_DIM:(h + 1) * HEAD_DIM] = out[r * tq:(r + 1) * tq].astype(o_ref.dtype)


def sparse_attention(qkv, bias, batch, seq):
    tq, ts = _tile(seq, B_TQ), _tile(seq, B_TS)
    view = qkv.reshape(batch, seq, B_QKV_WIDTH)
    k_tile = B_Q_WIDTH // B_KV_WIDTH
    pairs = [(i, j) for i in range(seq // tq) for j in range(((i + 1) * tq - 1) // ts + 1)]
    qblk = jnp.asarray([p[0] for p in pairs], jnp.int32)
    kblk = jnp.asarray([p[1] for p in pairs], jnp.int32)
    rows = B_REP * tq
    unit_rows = min(B_UNIT, rows)
    grid_spec = pltpu.PrefetchScalarGridSpec(
        num_scalar_prefetch=2,
        grid=(batch, len(pairs)),
        in_specs=[pl.BlockSpec((None, tq, B_Q_WIDTH), lambda b, s, qb, kb: (b, qb[s], 0)),
                  pl.BlockSpec((None, ts, B_KV_WIDTH), lambda b, s, qb, kb: (b, kb[s], k_tile)),
                  pl.BlockSpec((None, ts, B_KV_WIDTH), lambda b, s, qb, kb: (b, kb[s], k_tile + 1)),
                  pl.BlockSpec((None, tq, ts), lambda b, s, qb, kb: (b, qb[s], kb[s]))],
        out_specs=pl.BlockSpec((None, tq, B_Q_WIDTH), lambda b, s, qb, kb: (b, qb[s], 0)),
        scratch_shapes=[pltpu.VMEM((B_KV_HEADS, rows, HEAD_DIM), BF16),
                        pltpu.VMEM((B_KV_HEADS, ts, 2 * HEAD_DIM), BF16),
                        pltpu.VMEM((tq, ts), F32),
                        pltpu.VMEM((2, unit_rows, ts), F32),
                        pltpu.VMEM((2, unit_rows, ts), BF16),
                        pltpu.VMEM((2, unit_rows, LANES), F32),
                        pltpu.VMEM((B_KV_HEADS, rows, LANES), F32),
                        pltpu.VMEM((B_KV_HEADS, rows, 2 * HEAD_DIM), F32)])
    out = pl.pallas_call(
        _sparse_attn_kernel,
        out_shape=jax.ShapeDtypeStruct((batch, seq, B_Q_WIDTH), BF16),
        grid_spec=grid_spec,
        compiler_params=_params(("parallel", "arbitrary")),
        name="sparse_attention",
    )(qblk, kblk, view, view, view, bias)
    return out.reshape(batch * seq, B_Q_WIDTH)


def _rope_tables(positions):
    inv_freq = ROPE_THETA ** (-jnp.arange(0, HEAD_DIM, 2, dtype=F32) / HEAD_DIM)
    ang = positions.astype(F32).reshape(-1, 1) * inv_freq
    cos, sin = jnp.cos(ang), jnp.sin(ang)
    return jnp.concatenate([cos, cos], axis=-1), jnp.concatenate([-sin, sin], axis=-1)


def _mlp(f_in, w_up, w_down, layer):
    u = matmul(f_in, cast_weight(w_up, layer), act="relu2")
    return matmul(u, cast_weight(w_down, layer))


def _mixer_dilated(a, cos2, sin2, w_in, w_out, layer, batch, seq):
    tn = 1024
    per_part = A_GROUP_WIDTH // tn
    rope_pred = lambda j: (j // per_part) % 3 != 2
    a, cos2, sin2 = (_to_residue_major(t, batch, seq) for t in (a, cos2, sin2))
    qkv = matmul(a, cast_weight(w_in, layer), rope=(cos2, sin2, rope_pred), tn=tn)
    outs, lses = zip(*[dilated_attention(qkv, batch, seq, g) for g in range(N_GROUPS)])
    o = _from_residue_major(merge_groups(outs, lses), batch, seq)
    return matmul(o, cast_weight(w_out, layer))


def _mixer_sparse(a, cos2, sin2, w_in, idx_q_norm, w_idx_up, idx_k_w, idx_k_b, w_out, layer, batch, seq):
    tn = 1024
    n_rope = (B_Q_WIDTH + B_KV_WIDTH) // tn
    idx_width = IDX_Q_RANK + IDX_DIM + LANES
    w_all = cast_weight(w_in, layer, n_pad=B_QKV_WIDTH + idx_width)
    qkv = matmul(a, w_all, cols=(0, B_QKV_WIDTH), rope=(cos2, sin2, lambda j: j < n_rope), tn=tn)
    lat = matmul(a, w_all, cols=(B_QKV_WIDTH, idx_width), out_dtype=F32, tn=256)
    iq_n, ik, iw = indexer_prep(lat, idx_q_norm[layer], idx_k_w[layer], idx_k_b[layer], cos2, sin2)
    iq = matmul(iq_n, cast_weight(w_idx_up, layer), rope=(cos2, sin2, lambda j: j >= 0), tn=tn)
    bias = indexer_mask(iq, ik, iw, batch, seq)
    o = sparse_attention(qkv, bias, batch, seq)
    return matmul(o, cast_weight(w_out, layer))


def kernel(x, positions, attn_pre_norm, attn_post_norm, mlp_pre_norm, mlp_post_norm, w_in_a, w_out_a,
           w_in_b, idx_q_norm, w_idx_up, idx_k_norm_w, idx_k_norm_b, w_out_b, w_up, w_down):
    batch, seq, d_model = x.shape
    depth = attn_pre_norm.shape[0]
    cos2, sin2 = _rope_tables(positions)
    h = x.reshape(batch * seq, d_model)
    a = rms_norm_cast(h, attn_pre_norm[0])
    for i in range(depth):
        j = i // 2
        if i % 2 == 0:
            m = _mixer_dilated(a, cos2, sin2, w_in_a, w_out_a, j, batch, seq)
        else:
            m = _mixer_sparse(a, cos2, sin2, w_in_b, idx_q_norm, w_idx_up,
                              idx_k_norm_w, idx_k_norm_b, w_out_b, j, batch, seq)
        h, f_in = residual_norm(h, m, attn_post_norm[i], mlp_pre_norm[i])
        f = _mlp(f_in, w_up, w_down, i)
        h, a = residual_norm(h, f, mlp_post_norm[i], attn_pre_norm[i + 1] if i + 1 < depth else None)
    return h.reshape(batch, seq, d_model)
```

```python
import functools
import math

import jax
import jax.numpy as jnp
from jax import lax
from jax.experimental import pallas as pl
from jax.experimental.pallas import tpu as pltpu

HEAD_DIM = 128
ROPE_THETA = 10000.0
NORM_EPS = 1e-6
DILATED_GROUPS = ((128, 1), (512, 4), (2048, 16))
N_GROUPS = 3
A_HEADS_PER_GROUP = 16
A_BLOCK = 128
A_GROUP_WIDTH = A_HEADS_PER_GROUP * HEAD_DIM
B_HEADS = 32
B_KV_HEADS = 8
IDX_HEADS = 32
IDX_DIM = 128
IDX_Q_RANK = 1024
TOPK_MAX = 256
B_Q_WIDTH = B_HEADS * HEAD_DIM
B_KV_WIDTH = B_KV_HEADS * HEAD_DIM
B_QKV_WIDTH = B_Q_WIDTH + 2 * B_KV_WIDTH

LANES = 128
VMEM_LIMIT_BYTES = 56 * 2**20
MASK_VALUE = -1e30
INT_MIN = -2**31
LOG2E = math.log2(math.e)
NT_DIMS = (((1,), (1,)), ((), ()))

F32 = jnp.float32
BF16 = jnp.bfloat16


def _tile(dim, pref):
    t = min(dim, pref)
    if dim % t:
        t = 1 << (t.bit_length() - 1)
    while dim % t:
        t //= 2
    return t


def _params(semantics):
    return pltpu.CompilerParams(dimension_semantics=semantics, vmem_limit_bytes=VMEM_LIMIT_BYTES)


def _lanes(x, width):
    return jnp.concatenate([x] * (width // LANES), axis=1)


def _rms(x, g):
    return x * lax.rsqrt(jnp.mean(x * x, axis=-1, keepdims=True) + NORM_EPS) * g


def _norm_kernel(x_ref, g_ref, a_ref):
    a_ref[...] = _rms(x_ref[...], g_ref[...]).astype(a_ref.dtype)


def rms_norm_cast(x, g):
    m, d = x.shape
    tm = _tile(m, 256)
    return pl.pallas_call(
        _norm_kernel,
        out_shape=jax.ShapeDtypeStruct((m, d), BF16),
        grid=(m // tm,),
        in_specs=[pl.BlockSpec((tm, d), lambda i: (i, 0)), pl.BlockSpec((1, d), lambda i: (0, 0))],
        out_specs=pl.BlockSpec((tm, d), lambda i: (i, 0)),
        compiler_params=_params(("parallel",)),
        name="rms_norm_cast",
    )(x, g.reshape(1, d))


def _residual_kernel(h_ref, m_ref, gpost_ref, *rest, with_next):
    h = h_ref[...] + _rms(m_ref[...].astype(F32), gpost_ref[...])
    if with_next:
        gpre_ref, h_out_ref, a_ref = rest
        a_ref[...] = _rms(h, gpre_ref[...]).astype(a_ref.dtype)
    else:
        (h_out_ref,) = rest
    h_out_ref[...] = h


def residual_norm(h, m, g_post, g_pre_next=None):
    rows, d = h.shape
    tm = _tile(rows, 256)
    with_next = g_pre_next is not None
    row_spec = pl.BlockSpec((tm, d), lambda i: (i, 0))
    g_spec = pl.BlockSpec((1, d), lambda i: (0, 0))
    args = [h, m, g_post.reshape(1, d)]
    in_specs = [row_spec, row_spec, g_spec]
    out_shape = [jax.ShapeDtypeStruct((rows, d), F32)]
    out_specs = [row_spec]
    if with_next:
        args.append(g_pre_next.reshape(1, d))
        in_specs.append(g_spec)
        out_shape.append(jax.ShapeDtypeStruct((rows, d), BF16))
        out_specs.append(row_spec)
    out = pl.pallas_call(
        functools.partial(_residual_kernel, with_next=with_next),
        out_shape=out_shape,
        grid=(rows // tm,),
        in_specs=in_specs,
        out_specs=out_specs,
        compiler_params=_params(("parallel",)),
        name="residual_norm",
    )(*args)
    return (out[0], out[1]) if with_next else (out[0], None)


CAST_BLOCK_BYTES = 8 * 2**20


def _cast_kernel(w_ref, o_ref):
    rows, n = w_ref.shape
    n_pad = o_ref.shape[1]
    n_full = n // LANES * LANES
    o_ref[:, :n_full] = w_ref[:, :n_full].astype(o_ref.dtype)
    if n_pad > n_full:
        tail = [w_ref[:, n_full:].astype(o_ref.dtype)] if n > n_full else []
        o_ref[:, n_full:] = jnp.concatenate(tail + [jnp.zeros((rows, n_pad - n), o_ref.dtype)], axis=1)


def cast_weight(w, layer, n_pad=None):
    _, kdim, n = w.shape
    n_pad = n if n_pad is None else n_pad
    tk = _tile(kdim, max(8, CAST_BLOCK_BYTES // (4 * n)))
    return pl.pallas_call(
        _cast_kernel,
        out_shape=jax.ShapeDtypeStruct((kdim, n_pad), BF16),
        grid=(kdim // tk,),
        in_specs=[pl.BlockSpec((None, tk, n), lambda i: (layer, i, 0))],
        out_specs=pl.BlockSpec((tk, n_pad), lambda i: (i, 0)),
        compiler_params=_params(("parallel",)),
        name="cast_weight",
    )(w)


MM_CHUNK = 256


def _rope(x, cos2, sin2):
    return x * cos2 + pltpu.roll(x, HEAD_DIM // 2, 1) * sin2


def _matmul_kernel(a_ref, w_ref, *rest, nk, act, rope_pred, n_split):
    if rope_pred is not None:
        cos_ref, sin_ref, *rest = rest
        do_rope = rope_pred(pl.program_id(1))
        cos2 = jnp.where(do_rope, cos_ref[...], 1.0)
        sin2 = jnp.where(do_rope, sin_ref[...], 0.0)
    o_ref, *scratch = rest
    sub = o_ref.shape[1] // n_split
    if nk > 1:
        (acc_ref,) = scratch

        @pl.when(pl.program_id(2) == 0)
        def _():
            acc_ref[...] = jnp.zeros(acc_ref.shape, F32)

    for c in range(n_split):
        acc = jnp.dot(a_ref[...], w_ref[:, c * sub:(c + 1) * sub], preferred_element_type=F32)
        if nk > 1:
            acc = acc_ref[c] + acc
            acc_ref[c] = acc
        if act == "relu2":
            r = jnp.maximum(acc, 0.0)
            acc = r * r
        if rope_pred is None:
            o_ref[:, c * sub:(c + 1) * sub] = acc.astype(o_ref.dtype)
        else:
            for hh in range(sub // HEAD_DIM):
                x = acc[:, hh * HEAD_DIM:(hh + 1) * HEAD_DIM]
                col = c * sub + hh * HEAD_DIM
                o_ref[:, col:col + HEAD_DIM] = _rope(x, cos2, sin2).astype(o_ref.dtype)


def matmul(a, w, *, cols=None, out_dtype=BF16, act=None, rope=None, tm=1024, tn=1024, tk=4096):
    m, kdim = a.shape
    col0, n = (0, w.shape[1]) if cols is None else cols
    tm, tn, tk = _tile(m, tm), _tile(n, tn), _tile(kdim, tk)
    assert col0 % tn == 0
    j0 = col0 // tn
    nk = kdim // tk
    in_specs = [pl.BlockSpec((tm, tk), lambda i, j, k: (i, k)),
                pl.BlockSpec((tk, tn), lambda i, j, k: (k, j + j0))]
    args = [a, w]
    rope_pred = None
    if rope is not None:
        cos2, sin2, rope_pred = rope
        tab_spec = pl.BlockSpec((tm, HEAD_DIM), lambda i, j, k: (i, 0))
        in_specs += [tab_spec, tab_spec]
        args += [cos2, sin2]
    n_split = max(1, tn // MM_CHUNK)
    scratch = [pltpu.VMEM((n_split, tm, tn // n_split), F32)] if nk > 1 else []
    return pl.pallas_call(
        functools.partial(_matmul_kernel, nk=nk, act=act, rope_pred=rope_pred, n_split=n_split),
        out_shape=jax.ShapeDtypeStruct((m, n), out_dtype),
        grid=(m // tm, n // tn, nk),
        in_specs=in_specs,
        out_specs=pl.BlockSpec((tm, tn), lambda i, j, k: (i, j)),
        scratch_shapes=scratch,
        compiler_params=_params(("parallel", "parallel", "arbitrary")),
        name="matmul",
    )(*args)


A_PERIOD = 16


def _to_residue_major(x, batch, seq):
    u = seq // A_PERIOD
    return x.reshape(batch, u, A_PERIOD, -1).transpose(0, 2, 1, 3).reshape(batch * seq, -1)


def _from_residue_major(x, batch, seq):
    u = seq // A_PERIOD
    return x.reshape(batch, A_PERIOD, u, -1).transpose(0, 2, 1, 3).reshape(batch * seq, -1)


def _dilated_kernel(q_ref, kp_ref, kc_ref, vp_ref, vc_ref, o_ref, lse_ref, sp_ref, sc_ref,
                    *, w, rows, local_index, blk_axis):
    has_prev = pl.program_id(blk_axis) > 0
    qi = local_index(lax.broadcasted_iota(jnp.int32, (rows, rows), 0))
    kj = local_index(lax.broadcasted_iota(jnp.int32, (rows, rows), 1))
    mask_prev = jnp.logical_and(qi + rows - kj <= w, has_prev)
    mask_cur = jnp.logical_and(kj <= qi, qi - kj <= w)
    scale = HEAD_DIM ** -0.5 * LOG2E
    lead = (slice(None),) * (len(q_ref.shape) - 1)
    out_shape = o_ref.shape[:-1] + (HEAD_DIM,)

    def head(ref, h):
        return ref[lead + (slice(h * HEAD_DIM, (h + 1) * HEAD_DIM),)].reshape(rows, HEAD_DIM)

    def stage(h):
        q = head(q_ref, h)
        sp_ref[h % 2] = lax.dot_general(q, head(kp_ref, h), NT_DIMS, preferred_element_type=F32)
        sc_ref[h % 2] = lax.dot_general(q, head(kc_ref, h), NT_DIMS, preferred_element_type=F32)

    stage(0)
    for h in range(A_HEADS_PER_GROUP):
        if h + 1 < A_HEADS_PER_GROUP:
            stage(h + 1)
        s_prev = jnp.where(mask_prev, sp_ref[h % 2] * scale, MASK_VALUE)
        s_cur = jnp.where(mask_cur, sc_ref[h % 2] * scale, MASK_VALUE)
        m = jnp.maximum(jnp.max(s_prev, axis=1, keepdims=True), jnp.max(s_cur, axis=1, keepdims=True))
        p_prev = jnp.exp2(s_prev - m)
        p_cur = jnp.exp2(s_cur - m)
        l = jnp.sum(p_prev, axis=1, keepdims=True) + jnp.sum(p_cur, axis=1, keepdims=True)
        o = jnp.dot(p_prev.astype(BF16), head(vp_ref, h), preferred_element_type=F32)
        o += jnp.dot(p_cur.astype(BF16), head(vc_ref, h), preferred_element_type=F32)
        idx = lead + (slice(h * HEAD_DIM, (h + 1) * HEAD_DIM),)
        o_ref[idx] = (o / l).astype(o_ref.dtype).reshape(out_shape)
        lse_ref[idx] = jnp.broadcast_to(m * (1.0 / LOG2E) + jnp.log(l), (rows, HEAD_DIM)).reshape(out_shape)


def dilated_attention(qkv, batch, seq, group):
    window, r = DILATED_GROUPS[group]
    w = window // r
    u = seq // A_PERIOD
    width = qkv.shape[1]
    view = qkv.reshape(batch, 4, 4, u, width)
    if r == 16:
        rows, n_blk = A_BLOCK, u // A_BLOCK
        block = (None, None, None, rows, A_GROUP_WIDTH)
        grid = (batch, 4, 4, n_blk)
        place = lambda g, blk: (g[0], g[1], g[2], blk)
        local_index = lambda rho: rho
    elif r == 4:
        rows, n_blk = A_BLOCK, u // (A_BLOCK // 4)
        block = (None, 4, None, rows // 4, A_GROUP_WIDTH)
        grid = (batch, 4, n_blk)
        place = lambda g, blk: (g[0], 0, g[1], blk)
        local_index = lambda rho: 4 * (rho % (rows // 4)) + rho // (rows // 4)
    else:
        assert r == 1
        rows, n_blk = 2 * A_BLOCK, u // (2 * A_BLOCK // A_PERIOD)
        per = rows // A_PERIOD
        block = (None, 4, 4, per, A_GROUP_WIDTH)
        grid = (batch, n_blk)
        place = lambda g, blk: (g[0], 0, 0, blk)
        local_index = lambda rho: A_PERIOD * (rho % per) + rho // per
    assert w <= rows and n_blk * rows * r == seq

    def spec(which, prev):
        def index(*g):
            blk = jnp.maximum(g[-1] - 1, 0) if prev else g[-1]
            return place(g, blk) + (group * 3 + which,)
        return pl.BlockSpec(block, index)

    out_spec = pl.BlockSpec(block, lambda *g: place(g, g[-1]) + (0,))
    o, lse = pl.pallas_call(
        functools.partial(_dilated_kernel, w=w, rows=rows, local_index=local_index, blk_axis=len(grid) - 1),
        out_shape=[jax.ShapeDtypeStruct((batch, 4, 4, u, A_GROUP_WIDTH), BF16),
                   jax.ShapeDtypeStruct((batch, 4, 4, u, A_GROUP_WIDTH), F32)],
        grid=grid,
        in_specs=[spec(0, False), spec(1, True), spec(1, False), spec(2, True), spec(2, False)],
        out_specs=[out_spec, out_spec],
        scratch_shapes=[pltpu.VMEM((2, rows, rows), F32),
                        pltpu.VMEM((2, rows, rows), F32)],
        compiler_params=_params(("parallel",) * (len(grid) - 1) + ("arbitrary",)),
        name="dilated_attention",
    )(view, view, view, view, view)
    return o.reshape(batch * seq, A_GROUP_WIDTH), lse.reshape(batch * seq, A_GROUP_WIDTH)


def _merge_kernel(o0_ref, o1_ref, o2_ref, l0_ref, l1_ref, l2_ref, out_ref):
    l0, l1, l2 = l0_ref[...], l1_ref[...], l2_ref[...]
    m = jnp.maximum(jnp.maximum(l0, l1), l2)
    e0, e1, e2 = jnp.exp(l0 - m), jnp.exp(l1 - m), jnp.exp(l2 - m)
    num = e0 * o0_ref[...].astype(F32) + e1 * o1_ref[...].astype(F32) + e2 * o2_ref[...].astype(F32)
    out_ref[...] = (num / (e0 + e1 + e2)).astype(out_ref.dtype)


def merge_groups(outs, lses):
    rows, width = outs[0].shape
    tm = _tile(rows, 256)
    spec = pl.BlockSpec((tm, width), lambda i: (i, 0))
    return pl.pallas_call(
        _merge_kernel,
        out_shape=jax.ShapeDtypeStruct((rows, width), BF16),
        grid=(rows // tm,),
        in_specs=[spec] * 6,
        out_specs=spec,
        compiler_params=_params(("parallel",)),
        name="merge_groups",
    )(*outs, *lses)


def _indexer_prep_kernel(lat_ref, gq_ref, lnw_ref, lnb_ref, cos_ref, sin_ref, iqn_ref, ik_ref, iw_ref):
    iqn_ref[...] = _rms(lat_ref[:, :IDX_Q_RANK], gq_ref[...]).astype(iqn_ref.dtype)
    k = lat_ref[:, IDX_Q_RANK:IDX_Q_RANK + IDX_DIM]
    kc = k - jnp.mean(k, axis=-1, keepdims=True)
    kn = kc * lax.rsqrt(jnp.mean(kc * kc, axis=-1, keepdims=True) + NORM_EPS) * lnw_ref[...] + lnb_ref[...]
    ik_ref[...] = _rope(kn, cos_ref[...], sin_ref[...]).astype(ik_ref.dtype)
    iw_ref[...] = lat_ref[:, IDX_Q_RANK + IDX_DIM:] * (IDX_HEADS ** -0.5 * IDX_DIM ** -0.5)


def indexer_prep(lat, gq, lnw, lnb, cos2, sin2):
    rows, width = lat.shape
    tm = _tile(rows, 512)
    row = lambda wd: pl.BlockSpec((tm, wd), lambda i: (i, 0))
    vec = lambda wd: pl.BlockSpec((1, wd), lambda i: (0, 0))
    return pl.pallas_call(
        _indexer_prep_kernel,
        out_shape=[jax.ShapeDtypeStruct((rows, IDX_Q_RANK), BF16),
                   jax.ShapeDtypeStruct((rows, IDX_DIM), BF16),
                   jax.ShapeDtypeStruct((rows, LANES), F32)],
        grid=(rows // tm,),
        in_specs=[row(width), vec(IDX_Q_RANK), vec(IDX_DIM), vec(IDX_DIM), row(HEAD_DIM), row(HEAD_DIM)],
        out_specs=[row(IDX_Q_RANK), row(IDX_DIM), row(LANES)],
        compiler_params=_params(("parallel",)),
        name="indexer_prep",
    )(lat, gq.reshape(1, -1), lnw.reshape(1, -1), lnb.reshape(1, -1), cos2, sin2)


def _ordered_bits(x):
    bits = lax.bitcast_convert_type(x, jnp.int32)
    return jnp.where(bits >= 0, bits, bits ^ jnp.int32(0x7FFFFFFF))


def _score_of(key):
    return lax.bitcast_convert_type(jnp.where(key >= 0, key, key ^ jnp.int32(0x7FFFFFFF)), F32)


IDX_TQ = 128
IDX_TS = 512
IDX_HEAD_BATCH = 8
IDX_SCORE_STEPS = 24


def _indexer_kernel(iq_ref, ik_ref, iw_ref, bias_ref, key_ref, qs_ref, wb_ref, *, n_sel):
    tq, seq = key_ref.shape
    i = pl.program_id(1)
    n_chunks = ((i + 1) * tq + IDX_TS - 1) // IDX_TS
    key_ref[...] = jnp.full((tq, seq), INT_MIN, jnp.int32)
    iw = iw_ref[...]
    for h in range(IDX_HEADS):
        qs_ref[h * tq:(h + 1) * tq, :] = iq_ref[:, h * IDX_DIM:(h + 1) * IDX_DIM]
        wb_ref[h] = jnp.broadcast_to(iw[:, h:h + 1], (tq, LANES))
    t_pos = i * tq + lax.broadcasted_iota(jnp.int32, (tq, IDX_TS), 0)
    lane = lax.broadcasted_iota(jnp.int32, (tq, IDX_TS), 1)
    hb_rows = IDX_HEAD_BATCH * tq

    def score_chunk(c, carry):
        start = pl.multiple_of(c * IDX_TS, IDX_TS)
        keys = ik_ref[pl.ds(start, IDX_TS), :]
        acc = jnp.zeros((tq, IDX_TS), F32)
        for hb in range(IDX_HEADS // IDX_HEAD_BATCH):
            rel = lax.dot_general(qs_ref[hb * hb_rows:(hb + 1) * hb_rows, :], keys, NT_DIMS,
                                  preferred_element_type=F32)
            for hh in range(IDX_HEAD_BATCH):
                wb = _lanes(wb_ref[hb * IDX_HEAD_BATCH + hh], IDX_TS)
                acc += jnp.maximum(rel[hh * tq:(hh + 1) * tq], 0.0) * wb
        causal = start + lane <= t_pos
        key_ref[:, pl.ds(start, IDX_TS)] = jnp.where(causal, _ordered_bits(acc), INT_MIN)
        hi_part = jnp.where(causal, acc, -jnp.inf)
        lo_part = jnp.where(causal, acc, jnp.inf)
        smax, smin = carry
        for b in range(IDX_TS // LANES):
            smax = jnp.maximum(smax, hi_part[:, b * LANES:(b + 1) * LANES])
            smin = jnp.minimum(smin, lo_part[:, b * LANES:(b + 1) * LANES])
        return smax, smin

    smax, smin = lax.fori_loop(0, n_chunks, score_chunk,
                               (jnp.full((tq, LANES), -jnp.inf, F32), jnp.full((tq, LANES), jnp.inf, F32)))
    smax = jnp.max(smax, axis=1, keepdims=True)
    smin = jnp.min(smin, axis=1, keepdims=True)

    def count_ge(cand):
        cand_w = _lanes(cand, IDX_TS)

        def body(c, cnt):
            start = pl.multiple_of(c * IDX_TS, IDX_TS)
            ge = jnp.where(key_ref[:, pl.ds(start, IDX_TS)] >= cand_w, 1.0, 0.0)
            for b in range(IDX_TS // LANES):
                cnt = cnt + ge[:, b * LANES:(b + 1) * LANES]
            return cnt

        cnt = lax.fori_loop(0, n_chunks, body, jnp.zeros((tq, LANES), F32))
        return jnp.sum(cnt, axis=1, keepdims=True)

    ones_i = jnp.ones((tq, LANES), jnp.int32)
    n_causal = i * tq + 1 + lax.broadcasted_iota(jnp.int32, (tq, LANES), 0)
    lo0 = _ordered_bits(smin) * ones_i
    hi0 = _ordered_bits(smax) * ones_i + 1
    open0 = jnp.where(n_causal > n_sel, 1.0, 0.0)
    thr0 = jnp.full((tq, LANES), INT_MIN + 1, jnp.int32)

    def search_cond(state):
        step, _, n_open = state
        return jnp.logical_and(step < IDX_SCORE_STEPS + 33, n_open > 0.0)

    def search_step(state):
        step, (lo, hi, thr, still_open), _ = state
        score_mid = _ordered_bits(0.5 * _score_of(lo) + 0.5 * _score_of(hi))
        key_mid = (lo >> 1) + (hi >> 1) + (lo & hi & 1)
        cand = jnp.where(step < IDX_SCORE_STEPS, score_mid, key_mid)
        cand = jnp.maximum(jnp.minimum(cand, hi - 1), lo + 1)
        cnt = count_ge(cand)
        enough = cnt >= n_sel
        lo = jnp.where(enough, cand, lo)
        hi = jnp.where(enough, hi, cand)
        settled = jnp.logical_or(cnt == n_sel, hi <= lo + 1)
        thr = jnp.where(jnp.logical_and(still_open > 0.0, settled), lo, thr)
        still_open = jnp.where(settled, 0.0, still_open)
        return step + 1, (lo, hi, thr, still_open), jnp.max(still_open)

    init = (lo0, hi0, thr0, open0)
    _, (_, _, thr, _), _ = lax.while_loop(search_cond, search_step, (jnp.int32(0), init, jnp.max(open0)))
    thr_w = _lanes(thr, IDX_TS)
    for c in range(seq // IDX_TS):
        sl = slice(c * IDX_TS, (c + 1) * IDX_TS)
        bias_ref[:, sl] = jnp.where(key_ref[:, sl] >= thr_w, 0.0, MASK_VALUE).astype(bias_ref.dtype)


def indexer_mask(iq, ik, iw, batch, seq):
    n_sel = min(TOPK_MAX, seq // 4)
    tq = IDX_TQ
    assert seq % IDX_TS == 0 and seq % tq == 0
    return pl.pallas_call(
        functools.partial(_indexer_kernel, n_sel=n_sel),
        out_shape=jax.ShapeDtypeStruct((batch, seq, seq), BF16),
        grid=(batch, seq // tq),
        in_specs=[pl.BlockSpec((None, tq, IDX_HEADS * IDX_DIM), lambda b, i: (b, i, 0)),
                  pl.BlockSpec((None, seq, IDX_DIM), lambda b, i: (b, 0, 0)),
                  pl.BlockSpec((None, tq, LANES), lambda b, i: (b, i, 0))],
        out_specs=pl.BlockSpec((None, tq, seq), lambda b, i: (b, i, 0)),
        scratch_shapes=[pltpu.VMEM((tq, seq), jnp.int32),
                        pltpu.VMEM((IDX_HEADS * tq, IDX_DIM), BF16),
                        pltpu.VMEM((IDX_HEADS, tq, LANES), F32)],
        compiler_params=_params(("parallel", "arbitrary")),
        name="indexer_mask",
    )(iq.reshape(batch, seq, -1), ik.reshape(batch, seq, -1), iw.reshape(batch, seq, -1))


B_TQ = 256
B_TS = 512
B_REP = B_HEADS // B_KV_HEADS
B_CHUNK = 128
B_UNIT = 512


def _sparse_attn_kernel(qblk_ref, kblk_ref, q_ref, k_ref, v_ref, bias_ref, o_ref,
                        qs_ref, vo_ref, biasf_ref, s_ref, p_ref, alpha_ref, m_ref, accl_ref):
    tq, ts = bias_ref.shape
    rows = B_REP * tq
    step = pl.program_id(1)
    i, j = qblk_ref[step], kblk_ref[step]
    last = ((i + 1) * tq - 1) // ts

    @pl.when(j == 0)
    def _():
        scale = HEAD_DIM ** -0.5 * LOG2E
        for g in range(B_KV_HEADS):
            for r in range(B_REP):
                h = g * B_REP + r
                qh = q_ref[:, h * HEAD_DIM:(h + 1) * HEAD_DIM].astype(F32) * scale
                qs_ref[g, r * tq:(r + 1) * tq, :] = qh.astype(qs_ref.dtype)
        m_ref[...] = jnp.full(m_ref.shape, MASK_VALUE, F32)
        accl_ref[...] = jnp.zeros(accl_ref.shape, F32)

    biasf_ref[...] = bias_ref[...].astype(F32)
    ones = jnp.ones((ts, LANES), BF16)
    for g in range(B_KV_HEADS):
        vo_ref[g, :, :HEAD_DIM] = v_ref[:, g * HEAD_DIM:(g + 1) * HEAD_DIM]
        vo_ref[g, :, HEAD_DIM:] = ones

    unit_rows = min(B_UNIT, rows)
    units = [(g, r0) for g in range(B_KV_HEADS) for r0 in range(0, rows, unit_rows)]

    def logits(u):
        g, r0 = units[u]
        kg = k_ref[:, g * HEAD_DIM:(g + 1) * HEAD_DIM]
        s_ref[u % 2] = lax.dot_general(qs_ref[g, r0:r0 + unit_rows, :], kg, NT_DIMS,
                                       preferred_element_type=F32)

    logits(0)
    for u, (g, r0) in enumerate(units):
        if u + 1 < len(units):
            logits(u + 1)
        par = u % 2
        for c in range(0, unit_rows, B_CHUNK):
            rs = slice(c, c + B_CHUNK)
            gs = slice(r0 + c, r0 + c + B_CHUNK)
            qrow = (r0 + c) % tq
            s = s_ref[par, rs, :] + biasf_ref[qrow:qrow + B_CHUNK, :]
            m_prev = m_ref[g, gs, :]
            m_new = jnp.maximum(m_prev, jnp.max(s, axis=1, keepdims=True))
            p_ref[par, rs, :] = jnp.exp2(s - _lanes(m_new, ts)).astype(BF16)
            alpha_ref[par, rs, :] = jnp.exp2(m_prev - m_new)
            m_ref[g, gs, :] = m_new
        pv = jnp.dot(p_ref[par], vo_ref[g], preferred_element_type=F32)
        alpha = alpha_ref[par]
        us = slice(r0, r0 + unit_rows)
        accl_ref[g, us, :] = jnp.concatenate([alpha, alpha], axis=1) * accl_ref[g, us, :] + pv

    @pl.when(j == last)
    def _():
        for g in range(B_KV_HEADS):
            out = accl_ref[g, :, :HEAD_DIM] / accl_ref[g, :, HEAD_DIM:]
            for r in range(B_REP):
                h = g * B_REP + r
                o_ref[:, h * HEAD_DIM:(h + 1) * HEAD_DIM] = out[r * tq:(r + 1) * tq].astype(o_ref.dtype)


def sparse_attention(qkv, bias, batch, seq):
    tq, ts = _tile(seq, B_TQ), _tile(seq, B_TS)
    view = qkv.reshape(batch, seq, B_QKV_WIDTH)
    k_tile = B_Q_WIDTH // B_KV_WIDTH
    pairs = [(i, j) for i in range(seq // tq) for j in range(((i + 1) * tq - 1) // ts + 1)]
    qblk = jnp.asarray([p[0] for p in pairs], jnp.int32)
    kblk = jnp.asarray([p[1] for p in pairs], jnp.int32)
    rows = B_REP * tq
    unit_rows = min(B_UNIT, rows)
    grid_spec = pltpu.PrefetchScalarGridSpec(
        num_scalar_prefetch=2,
        grid=(batch, len(pairs)),
        in_specs=[pl.BlockSpec((None, tq, B_Q_WIDTH), lambda b, s, qb, kb: (b, qb[s], 0)),
                  pl.BlockSpec((None, ts, B_KV_WIDTH), lambda b, s, qb, kb: (b, kb[s], k_tile)),
                  pl.BlockSpec((None, ts, B_KV_WIDTH), lambda b, s, qb, kb: (b, kb[s], k_tile + 1)),
                  pl.BlockSpec((None, tq, ts), lambda b, s, qb, kb: (b, qb[s], kb[s]))],
        out_specs=pl.BlockSpec((None, tq, B_Q_WIDTH), lambda b, s, qb, kb: (b, qb[s], 0)),
        scratch_shapes=[pltpu.VMEM((B_KV_HEADS, rows, HEAD_DIM), BF16),
                        pltpu.VMEM((B_KV_HEADS, ts, 2 * HEAD_DIM), BF16),
                        pltpu.VMEM((tq, ts), F32),
                        pltpu.VMEM((2, unit_rows, ts), F32),
                        pltpu.VMEM((2, unit_rows, ts), BF16),
                        pltpu.VMEM((2, unit_rows, LANES), F32),
                        pltpu.VMEM((B_KV_HEADS, rows, LANES), F32),
                        pltpu.VMEM((B_KV_HEADS, rows, 2 * HEAD_DIM), F32)])
    out = pl.pallas_call(
        _sparse_attn_kernel,
        out_shape=jax.ShapeDtypeStruct((batch, seq, B_Q_WIDTH), BF16),
        grid_spec=grid_spec,
        compiler_params=_params(("parallel", "arbitrary")),
        name="sparse_attention",
    )(qblk, kblk, view, view, view, bias)
    return out.reshape(batch * seq, B_Q_WIDTH)


def _rope_tables(positions):
    inv_freq = ROPE_THETA ** (-jnp.arange(0, HEAD_DIM, 2, dtype=F32) / HEAD_DIM)
    ang = positions.astype(F32).reshape(-1, 1) * inv_freq
    cos, sin = jnp.cos(ang), jnp.sin(ang)
    return jnp.concatenate([cos, cos], axis=-1), jnp.concatenate([-sin, sin], axis=-1)


def _mlp(f_in, w_up, w_down, layer):
    u = matmul(f_in, cast_weight(w_up, layer), act="relu2")
    return matmul(u, cast_weight(w_down, layer))


def _mixer_dilated(a, cos2, sin2, w_in, w_out, layer, batch, seq):
    tn = 1024
    per_part = A_GROUP_WIDTH // tn
    rope_pred = lambda j: (j // per_part) % 3 != 2
    a, cos2, sin2 = (_to_residue_major(t, batch, seq) for t in (a, cos2, sin2))
    qkv = matmul(a, cast_weight(w_in, layer), rope=(cos2, sin2, rope_pred), tn=tn)
    outs, lses = zip(*[dilated_attention(qkv, batch, seq, g) for g in range(N_GROUPS)])
    o = _from_residue_major(merge_groups(outs, lses), batch, seq)
    return matmul(o, cast_weight(w_out, layer))


def _mixer_sparse(a, cos2, sin2, w_in, idx_q_norm, w_idx_up, idx_k_w, idx_k_b, w_out, layer, batch, seq):
    tn = 1024
    n_rope = (B_Q_WIDTH + B_KV_WIDTH) // tn
    idx_width = IDX_Q_RANK + IDX_DIM + LANES
    w_all = cast_weight(w_in, layer, n_pad=B_QKV_WIDTH + idx_width)
    qkv = matmul(a, w_all, cols=(0, B_QKV_WIDTH), rope=(cos2, sin2, lambda j: j < n_rope), tn=tn)
    lat = matmul(a, w_all, cols=(B_QKV_WIDTH, idx_width), out_dtype=F32, tn=256)
    iq_n, ik, iw = indexer_prep(lat, idx_q_norm[layer], idx_k_w[layer], idx_k_b[layer], cos2, sin2)
    iq = matmul(iq_n, cast_weight(w_idx_up, layer), rope=(cos2, sin2, lambda j: j >= 0), tn=tn)
    bias = indexer_mask(iq, ik, iw, batch, seq)
    o = sparse_attention(qkv, bias, batch, seq)
    return matmul(o, cast_weight(w_out, layer))


def kernel(x, positions, attn_pre_norm, attn_post_norm, mlp_pre_norm, mlp_post_norm, w_in_a, w_out_a,
           w_in_b, idx_q_norm, w_idx_up, idx_k_norm_w, idx_k_norm_b, w_out_b, w_up, w_down):
    batch, seq, d_model = x.shape
    depth = attn_pre_norm.shape[0]
    cos2, sin2 = _rope_tables(positions)
    h = x.reshape(batch * seq, d_model)
    a = rms_norm_cast(h, attn_pre_norm[0])
    for i in range(depth):
        j = i // 2
        if i % 2 == 0:
            m = _mixer_dilated(a, cos2, sin2, w_in_a, w_out_a, j, batch, seq)
        else:
            m = _mixer_sparse(a, cos2, sin2, w_in_b, idx_q_norm, w_idx_up,
                              idx_k_norm_w, idx_k_norm_b, w_out_b, j, batch, seq)
        h, f_in = residual_norm(h, m, attn_post_norm[i], mlp_pre_norm[i])
        f = _mlp(f_in, w_up, w_down, i)
        h, a = residual_norm(h, f, mlp_post_norm[i], attn_pre_norm[i + 1] if i + 1 < depth else None)
    return h.reshape(batch, seq, d_model)
```

```python
import functools
import math

import jax
import jax.numpy as jnp
from jax import lax
from jax.experimental import pallas as pl
from jax.experimental.pallas import tpu as pltpu

HEAD_DIM = 128
ROPE_THETA = 10000.0
NORM_EPS = 1e-6
DILATED_GROUPS = ((128, 1), (512, 4), (2048, 16))
N_GROUPS = 3
A_HEADS_PER_GROUP = 16
A_BLOCK = 128
A_GROUP_WIDTH = A_HEADS_PER_GROUP * HEAD_DIM
B_HEADS = 32
B_KV_HEADS = 8
IDX_HEADS = 32
IDX_DIM = 128
IDX_Q_RANK = 1024
TOPK_MAX = 256
B_Q_WIDTH = B_HEADS * HEAD_DIM
B_KV_WIDTH = B_KV_HEADS * HEAD_DIM
B_QKV_WIDTH = B_Q_WIDTH + 2 * B_KV_WIDTH

LANES = 128
VMEM_LIMIT_BYTES = 56 * 2**20
MASK_VALUE = -1e30
INT_MIN = -2**31
INT_MAX = 2**31 - 1
LOG2E = math.log2(math.e)
NT_DIMS = (((1,), (1,)), ((), ()))

F32 = jnp.float32
BF16 = jnp.bfloat16


def _tile(dim, pref):
    t = min(dim, pref)
    if dim % t:
        t = 1 << (t.bit_length() - 1)
    while dim % t:
        t //= 2
    return t


def _params(semantics):
    return pltpu.CompilerParams(dimension_semantics=semantics, vmem_limit_bytes=VMEM_LIMIT_BYTES)


def _lanes(x, width):
    return jnp.concatenate([x] * (width // LANES), axis=1)


def _rms(x, g):
    return x * lax.rsqrt(jnp.mean(x * x, axis=-1, keepdims=True) + NORM_EPS) * g


def _norm_kernel(x_ref, g_ref, a_ref):
    a_ref[...] = _rms(x_ref[...], g_ref[...]).astype(a_ref.dtype)


def rms_norm_cast(x, g):
    m, d = x.shape
    tm = _tile(m, 256)
    return pl.pallas_call(
        _norm_kernel,
        out_shape=jax.ShapeDtypeStruct((m, d), BF16),
        grid=(m // tm,),
        in_specs=[pl.BlockSpec((tm, d), lambda i: (i, 0)), pl.BlockSpec((1, d), lambda i: (0, 0))],
        out_specs=pl.BlockSpec((tm, d), lambda i: (i, 0)),
        compiler_params=_params(("parallel",)),
        name="rms_norm_cast",
    )(x, g.reshape(1, d))


def _residual_kernel(h_ref, m_ref, gpost_ref, *rest, with_next):
    h = h_ref[...] + _rms(m_ref[...].astype(F32), gpost_ref[...])
    if with_next:
        gpre_ref, h_out_ref, a_ref = rest
        a_ref[...] = _rms(h, gpre_ref[...]).astype(a_ref.dtype)
    else:
        (h_out_ref,) = rest
    h_out_ref[...] = h


def residual_norm(h, m, g_post, g_pre_next=None):
    rows, d = h.shape
    tm = _tile(rows, 256)
    with_next = g_pre_next is not None
    row_spec = pl.BlockSpec((tm, d), lambda i: (i, 0))
    g_spec = pl.BlockSpec((1, d), lambda i: (0, 0))
    args = [h, m, g_post.reshape(1, d)]
    in_specs = [row_spec, row_spec, g_spec]
    out_shape = [jax.ShapeDtypeStruct((rows, d), F32)]
    out_specs = [row_spec]
    if with_next:
        args.append(g_pre_next.reshape(1, d))
        in_specs.append(g_spec)
        out_shape.append(jax.ShapeDtypeStruct((rows, d), BF16))
        out_specs.append(row_spec)
    out = pl.pallas_call(
        functools.partial(_residual_kernel, with_next=with_next),
        out_shape=out_shape,
        grid=(rows // tm,),
        in_specs=in_specs,
        out_specs=out_specs,
        compiler_params=_params(("parallel",)),
        name="residual_norm",
    )(*args)
    return (out[0], out[1]) if with_next else (out[0], None)


CAST_BLOCK_BYTES = 8 * 2**20


def _cast_kernel(w_ref, o_ref):
    rows, n = w_ref.shape
    n_pad = o_ref.shape[1]
    n_full = n // LANES * LANES
    o_ref[:, :n_full] = w_ref[:, :n_full].astype(o_ref.dtype)
    if n_pad > n_full:
        tail = [w_ref[:, n_full:].astype(o_ref.dtype)] if n > n_full else []
        o_ref[:, n_full:] = jnp.concatenate(tail + [jnp.zeros((rows, n_pad - n), o_ref.dtype)], axis=1)


def cast_weight(w, layer, n_pad=None):
    _, kdim, n = w.shape
    n_pad = n if n_pad is None else n_pad
    tk = _tile(kdim, max(8, CAST_BLOCK_BYTES // (4 * n)))
    return pl.pallas_call(
        _cast_kernel,
        out_shape=jax.ShapeDtypeStruct((kdim, n_pad), BF16),
        grid=(kdim // tk,),
        in_specs=[pl.BlockSpec((None, tk, n), lambda i: (layer, i, 0))],
        out_specs=pl.BlockSpec((tk, n_pad), lambda i: (i, 0)),
        compiler_params=_params(("parallel",)),
        name="cast_weight",
    )(w)


MM_CHUNK = 256


def _rope(x, cos2, sin2):
    return x * cos2 + pltpu.roll(x, HEAD_DIM // 2, 1) * sin2


def _matmul_kernel(a_ref, w_ref, *rest, nk, act, rope_pred, n_split):
    if rope_pred is not None:
        cos_ref, sin_ref, *rest = rest
        do_rope = rope_pred(pl.program_id(1))
        cos2 = jnp.where(do_rope, cos_ref[...], 1.0)
        sin2 = jnp.where(do_rope, sin_ref[...], 0.0)
    o_ref, *scratch = rest
    sub = o_ref.shape[1] // n_split
    if nk > 1:
        (acc_ref,) = scratch

        @pl.when(pl.program_id(2) == 0)
        def _():
            acc_ref[...] = jnp.zeros(acc_ref.shape, F32)

    for c in range(n_split):
        acc = jnp.dot(a_ref[...], w_ref[:, c * sub:(c + 1) * sub], preferred_element_type=F32)
        if nk > 1:
            acc = acc_ref[c] + acc
            acc_ref[c] = acc
        if act == "relu2":
            r = jnp.maximum(acc, 0.0)
            acc = r * r
        if rope_pred is None:
            o_ref[:, c * sub:(c + 1) * sub] = acc.astype(o_ref.dtype)
        else:
            for hh in range(sub // HEAD_DIM):
                x = acc[:, hh * HEAD_DIM:(hh + 1) * HEAD_DIM]
                col = c * sub + hh * HEAD_DIM
                o_ref[:, col:col + HEAD_DIM] = _rope(x, cos2, sin2).astype(o_ref.dtype)


def matmul(a, w, *, cols=None, out_dtype=BF16, act=None, rope=None, tm=1024, tn=1024, tk=4096):
    m, kdim = a.shape
    col0, n = (0, w.shape[1]) if cols is None else cols
    tm, tn, tk = _tile(m, tm), _tile(n, tn), _tile(kdim, tk)
    assert col0 % tn == 0
    j0 = col0 // tn
    nk = kdim // tk
    in_specs = [pl.BlockSpec((tm, tk), lambda i, j, k: (i, k)),
                pl.BlockSpec((tk, tn), lambda i, j, k: (k, j + j0))]
    args = [a, w]
    rope_pred = None
    if rope is not None:
        cos2, sin2, rope_pred = rope
        tab_spec = pl.BlockSpec((tm, HEAD_DIM), lambda i, j, k: (i, 0))
        in_specs += [tab_spec, tab_spec]
        args += [cos2, sin2]
    n_split = max(1, tn // MM_CHUNK)
    scratch = [pltpu.VMEM((n_split, tm, tn // n_split), F32)] if nk > 1 else []
    return pl.pallas_call(
        functools.partial(_matmul_kernel, nk=nk, act=act, rope_pred=rope_pred, n_split=n_split),
        out_shape=jax.ShapeDtypeStruct((m, n), out_dtype),
        grid=(m // tm, n // tn, nk),
        in_specs=in_specs,
        out_specs=pl.BlockSpec((tm, tn), lambda i, j, k: (i, j)),
        scratch_shapes=scratch,
        compiler_params=_params(("parallel", "parallel", "arbitrary")),
        name="matmul",
    )(*args)


A_PERIOD = 16


def _to_residue_major(x, batch, seq):
    u = seq // A_PERIOD
    return x.reshape(batch, u, A_PERIOD, -1).transpose(0, 2, 1, 3).reshape(batch * seq, -1)


def _from_residue_major(x, batch, seq):
    u = seq // A_PERIOD
    return x.reshape(batch, A_PERIOD, u, -1).transpose(0, 2, 1, 3).reshape(batch * seq, -1)


def _dilated_kernel(q_ref, kp_ref, kc_ref, vp_ref, vc_ref, o_ref, lse_ref, sp_ref, sc_ref,
                    *, w, rows, local_index, blk_axis):
    has_prev = pl.program_id(blk_axis) > 0
    qi = local_index(lax.broadcasted_iota(jnp.int32, (rows, rows), 0))
    kj = local_index(lax.broadcasted_iota(jnp.int32, (rows, rows), 1))
    mask_prev = jnp.logical_and(qi + rows - kj <= w, has_prev)
    mask_cur = jnp.logical_and(kj <= qi, qi - kj <= w)
    scale = HEAD_DIM ** -0.5 * LOG2E
    lead = (slice(None),) * (len(q_ref.shape) - 1)
    out_shape = o_ref.shape[:-1] + (HEAD_DIM,)

    def head(ref, h):
        return ref[lead + (slice(h * HEAD_DIM, (h + 1) * HEAD_DIM),)].reshape(rows, HEAD_DIM)

    def stage(h):
        q = head(q_ref, h)
        sp_ref[h % 2] = lax.dot_general(q, head(kp_ref, h), NT_DIMS, preferred_element_type=F32)
        sc_ref[h % 2] = lax.dot_general(q, head(kc_ref, h), NT_DIMS, preferred_element_type=F32)

    stage(0)
    for h in range(A_HEADS_PER_GROUP):
        if h + 1 < A_HEADS_PER_GROUP:
            stage(h + 1)
        s_prev = jnp.where(mask_prev, sp_ref[h % 2] * scale, MASK_VALUE)
        s_cur = jnp.where(mask_cur, sc_ref[h % 2] * scale, MASK_VALUE)
        m = jnp.maximum(jnp.max(s_prev, axis=1, keepdims=True), jnp.max(s_cur, axis=1, keepdims=True))
        p_prev = jnp.exp2(s_prev - m)
        p_cur = jnp.exp2(s_cur - m)
        l = jnp.sum(p_prev, axis=1, keepdims=True) + jnp.sum(p_cur, axis=1, keepdims=True)
        o = jnp.dot(p_prev.astype(BF16), head(vp_ref, h), preferred_element_type=F32)
        o += jnp.dot(p_cur.astype(BF16), head(vc_ref, h), preferred_element_type=F32)
        idx = lead + (slice(h * HEAD_DIM, (h + 1) * HEAD_DIM),)
        o_ref[idx] = (o / l).astype(o_ref.dtype).reshape(out_shape)
        lse_ref[idx] = jnp.broadcast_to(m * (1.0 / LOG2E) + jnp.log(l), (rows, HEAD_DIM)).reshape(out_shape)


def dilated_attention(qkv, batch, seq, group):
    window, r = DILATED_GROUPS[group]
    w = window // r
    u = seq // A_PERIOD
    width = qkv.shape[1]
    view = qkv.reshape(batch, 4, 4, u, width)
    if r == 16:
        rows, n_blk = A_BLOCK, u // A_BLOCK
        block = (None, None, None, rows, A_GROUP_WIDTH)
        grid = (batch, 4, 4, n_blk)
        place = lambda g, blk: (g[0], g[1], g[2], blk)
        local_index = lambda rho: rho
    elif r == 4:
        rows, n_blk = A_BLOCK, u // (A_BLOCK // 4)
        block = (None, 4, None, rows // 4, A_GROUP_WIDTH)
        grid = (batch, 4, n_blk)
        place = lambda g, blk: (g[0], 0, g[1], blk)
        local_index = lambda rho: 4 * (rho % (rows // 4)) + rho // (rows // 4)
    else:
        assert r == 1
        rows, n_blk = 2 * A_BLOCK, u // (2 * A_BLOCK // A_PERIOD)
        per = rows // A_PERIOD
        block = (None, 4, 4, per, A_GROUP_WIDTH)
        grid = (batch, n_blk)
        place = lambda g, blk: (g[0], 0, 0, blk)
        local_index = lambda rho: A_PERIOD * (rho % per) + rho // per
    assert w <= rows and n_blk * rows * r == seq

    def spec(which, prev):
        def index(*g):
            blk = jnp.maximum(g[-1] - 1, 0) if prev else g[-1]
            return place(g, blk) + (group * 3 + which,)
        return pl.BlockSpec(block, index)

    out_spec = pl.BlockSpec(block, lambda *g: place(g, g[-1]) + (0,))
    o, lse = pl.pallas_call(
        functools.partial(_dilated_kernel, w=w, rows=rows, local_index=local_index, blk_axis=len(grid) - 1),
        out_shape=[jax.ShapeDtypeStruct((batch, 4, 4, u, A_GROUP_WIDTH), BF16),
                   jax.ShapeDtypeStruct((batch, 4, 4, u, A_GROUP_WIDTH), F32)],
        grid=grid,
        in_specs=[spec(0, False), spec(1, True), spec(1, False), spec(2, True), spec(2, False)],
        out_specs=[out_spec, out_spec],
        scratch_shapes=[pltpu.VMEM((2, rows, rows), F32),
                        pltpu.VMEM((2, rows, rows), F32)],
        compiler_params=_params(("parallel",) * (len(grid) - 1) + ("arbitrary",)),
        name="dilated_attention",
    )(view, view, view, view, view)
    return o.reshape(batch * seq, A_GROUP_WIDTH), lse.reshape(batch * seq, A_GROUP_WIDTH)


def _merge_kernel(o0_ref, o1_ref, o2_ref, l0_ref, l1_ref, l2_ref, out_ref):
    l0, l1, l2 = l0_ref[...], l1_ref[...], l2_ref[...]
    m = jnp.maximum(jnp.maximum(l0, l1), l2)
    e0, e1, e2 = jnp.exp(l0 - m), jnp.exp(l1 - m), jnp.exp(l2 - m)
    num = e0 * o0_ref[...].astype(F32) + e1 * o1_ref[...].astype(F32) + e2 * o2_ref[...].astype(F32)
    out_ref[...] = (num / (e0 + e1 + e2)).astype(out_ref.dtype)


def merge_groups(outs, lses):
    rows, width = outs[0].shape
    tm = _tile(rows, 256)
    spec = pl.BlockSpec((tm, width), lambda i: (i, 0))
    return pl.pallas_call(
        _merge_kernel,
        out_shape=jax.ShapeDtypeStruct((rows, width), BF16),
        grid=(rows // tm,),
        in_specs=[spec] * 6,
        out_specs=spec,
        compiler_params=_params(("parallel",)),
        name="merge_groups",
    )(*outs, *lses)


def _indexer_prep_kernel(lat_ref, gq_ref, lnw_ref, lnb_ref, cos_ref, sin_ref, iqn_ref, ik_ref, iw_ref):
    iqn_ref[...] = _rms(lat_ref[:, :IDX_Q_RANK], gq_ref[...]).astype(iqn_ref.dtype)
    k = lat_ref[:, IDX_Q_RANK:IDX_Q_RANK + IDX_DIM]
    kc = k - jnp.mean(k, axis=-1, keepdims=True)
    kn = kc * lax.rsqrt(jnp.mean(kc * kc, axis=-1, keepdims=True) + NORM_EPS) * lnw_ref[...] + lnb_ref[...]
    ik_ref[...] = _rope(kn, cos_ref[...], sin_ref[...]).astype(ik_ref.dtype)
    iw_ref[...] = lat_ref[:, IDX_Q_RANK + IDX_DIM:] * (IDX_HEADS ** -0.5 * IDX_DIM ** -0.5)


def indexer_prep(lat, gq, lnw, lnb, cos2, sin2):
    rows, width = lat.shape
    tm = _tile(rows, 512)
    row = lambda wd: pl.BlockSpec((tm, wd), lambda i: (i, 0))
    vec = lambda wd: pl.BlockSpec((1, wd), lambda i: (0, 0))
    return pl.pallas_call(
        _indexer_prep_kernel,
        out_shape=[jax.ShapeDtypeStruct((rows, IDX_Q_RANK), BF16),
                   jax.ShapeDtypeStruct((rows, IDX_DIM), BF16),
                   jax.ShapeDtypeStruct((rows, LANES), F32)],
        grid=(rows // tm,),
        in_specs=[row(width), vec(IDX_Q_RANK), vec(IDX_DIM), vec(IDX_DIM), row(HEAD_DIM), row(HEAD_DIM)],
        out_specs=[row(IDX_Q_RANK), row(IDX_DIM), row(LANES)],
        compiler_params=_params(("parallel",)),
        name="indexer_prep",
    )(lat, gq.reshape(1, -1), lnw.reshape(1, -1), lnb.reshape(1, -1), cos2, sin2)


def _ordered_bits(x):
    bits = lax.bitcast_convert_type(x, jnp.int32)
    key = jnp.where(bits >= 0, bits, bits ^ jnp.int32(0x7FFFFFFF))
    return jnp.where(key == -1, 0, key)


def _score_of(key):
    return lax.bitcast_convert_type(jnp.where(key >= 0, key, key ^ jnp.int32(0x7FFFFFFF)), F32)


IDX_TQ = 128
IDX_TS = 512
IDX_HEAD_BATCH = 8
IDX_SCORE_STEPS = 24


def _indexer_kernel(iq_ref, ik_ref, iw_ref, bias_ref, key_ref, qs_ref, wb_ref, tie_ref, *, n_sel):
    tq, seq = key_ref.shape
    i = pl.program_id(1)
    n_chunks = ((i + 1) * tq + IDX_TS - 1) // IDX_TS
    key_ref[...] = jnp.full((tq, seq), INT_MIN, jnp.int32)
    iw = iw_ref[...]
    for h in range(IDX_HEADS):
        qs_ref[h * tq:(h + 1) * tq, :] = iq_ref[:, h * IDX_DIM:(h + 1) * IDX_DIM]
        wb_ref[h] = jnp.broadcast_to(iw[:, h:h + 1], (tq, LANES))
    t_pos = i * tq + lax.broadcasted_iota(jnp.int32, (tq, IDX_TS), 0)
    lane = lax.broadcasted_iota(jnp.int32, (tq, IDX_TS), 1)
    hb_rows = IDX_HEAD_BATCH * tq

    def score_chunk(c, carry):
        start = pl.multiple_of(c * IDX_TS, IDX_TS)
        keys = ik_ref[pl.ds(start, IDX_TS), :]
        acc = jnp.zeros((tq, IDX_TS), F32)
        for hb in range(IDX_HEADS // IDX_HEAD_BATCH):
            rel = lax.dot_general(qs_ref[hb * hb_rows:(hb + 1) * hb_rows, :], keys, NT_DIMS,
                                  preferred_element_type=F32)
            for hh in range(IDX_HEAD_BATCH):
                wb = _lanes(wb_ref[hb * IDX_HEAD_BATCH + hh], IDX_TS)
                acc += jnp.maximum(rel[hh * tq:(hh + 1) * tq], 0.0) * wb
        causal = start + lane <= t_pos
        key_ref[:, pl.ds(start, IDX_TS)] = jnp.where(causal, _ordered_bits(acc), INT_MIN)
        hi_part = jnp.where(causal, acc, -jnp.inf)
        lo_part = jnp.where(causal, acc, jnp.inf)
        smax, smin = carry
        for b in range(IDX_TS // LANES):
            smax = jnp.maximum(smax, hi_part[:, b * LANES:(b + 1) * LANES])
            smin = jnp.minimum(smin, lo_part[:, b * LANES:(b + 1) * LANES])
        return smax, smin

    smax, smin = lax.fori_loop(0, n_chunks, score_chunk,
                               (jnp.full((tq, LANES), -jnp.inf, F32), jnp.full((tq, LANES), jnp.inf, F32)))
    smax = jnp.max(smax, axis=1, keepdims=True)
    smin = jnp.min(smin, axis=1, keepdims=True)

    def count_where(pred):
        def body(c, cnt):
            start = pl.multiple_of(c * IDX_TS, IDX_TS)
            hit = jnp.where(pred(key_ref[:, pl.ds(start, IDX_TS)], start + lane), 1.0, 0.0)
            for b in range(IDX_TS // LANES):
                cnt = cnt + hit[:, b * LANES:(b + 1) * LANES]
            return cnt

        cnt = lax.fori_loop(0, n_chunks, body, jnp.zeros((tq, LANES), F32))
        return jnp.sum(cnt, axis=1, keepdims=True)

    def count_ge(cand):
        cand_w = _lanes(cand, IDX_TS)
        return count_where(lambda keys, pos: keys >= cand_w)

    ones_i = jnp.ones((tq, LANES), jnp.int32)
    n_causal = i * tq + 1 + lax.broadcasted_iota(jnp.int32, (tq, LANES), 0)
    lo0 = _ordered_bits(smin) * ones_i
    hi0 = _ordered_bits(smax) * ones_i + 1
    open0 = jnp.where(n_causal > n_sel, 1.0, 0.0)
    thr0 = jnp.full((tq, LANES), INT_MIN + 1, jnp.int32)

    def search_cond(state):
        step, _, n_open = state
        return jnp.logical_and(step < IDX_SCORE_STEPS + 33, n_open > 0.0)

    def search_step(state):
        step, (lo, hi, thr, still_open), _ = state
        score_mid = _ordered_bits(0.5 * _score_of(lo) + 0.5 * _score_of(hi))
        key_mid = (lo >> 1) + (hi >> 1) + (lo & hi & 1)
        cand = jnp.where(step < IDX_SCORE_STEPS, score_mid, key_mid)
        cand = jnp.maximum(jnp.minimum(cand, hi - 1), lo + 1)
        cnt = count_ge(cand)
        enough = cnt >= n_sel
        lo = jnp.where(enough, cand, lo)
        hi = jnp.where(enough, hi, cand)
        settled = jnp.logical_or(cnt == n_sel, hi <= lo + 1)
        thr = jnp.where(jnp.logical_and(still_open > 0.0, settled), lo, thr)
        still_open = jnp.where(settled, 0.0, still_open)
        return step + 1, (lo, hi, thr, still_open), jnp.max(still_open)

    init = (lo0, hi0, thr0, open0)
    _, (_, _, thr, _), _ = lax.while_loop(search_cond, search_step, (jnp.int32(0), init, jnp.max(open0)))
    thr_w = _lanes(thr, IDX_TS)

    tied = jnp.logical_and(count_ge(thr) > n_sel, n_causal > n_sel)
    tie_ref[...] = jnp.full((tq, LANES), INT_MAX, jnp.int32)

    @pl.when(jnp.max(jnp.where(tied, 1.0, 0.0)) > 0.0)
    def _():
        need = n_sel - count_ge(thr + 1)

        def halve(_, bracket):
            below, upto = bracket
            mid_w = _lanes((below + upto) >> 1, IDX_TS)
            cnt = count_where(lambda keys, pos: jnp.logical_and(keys == thr_w, pos <= mid_w))
            ok = cnt >= need
            mid = (below + upto) >> 1
            return jnp.where(ok, below, mid), jnp.where(ok, mid, upto)

        _, upto = lax.fori_loop(0, seq.bit_length(), halve,
                                (jnp.full((tq, LANES), -1, jnp.int32), jnp.full((tq, LANES), seq - 1, jnp.int32)))
        tie_ref[...] = jnp.where(tied, upto, INT_MAX)

    last_tie_w = _lanes(tie_ref[...], IDX_TS)
    for c in range(seq // IDX_TS):
        sl = slice(c * IDX_TS, (c + 1) * IDX_TS)
        keys = key_ref[:, sl]
        keep = jnp.logical_or(keys > thr_w, jnp.logical_and(keys == thr_w, c * IDX_TS + lane <= last_tie_w))
        bias_ref[:, sl] = jnp.where(keep, 0.0, MASK_VALUE).astype(bias_ref.dtype)


def indexer_mask(iq, ik, iw, batch, seq):
    n_sel = min(TOPK_MAX, seq // 4)
    tq = IDX_TQ
    assert seq % IDX_TS == 0 and seq % tq == 0
    return pl.pallas_call(
        functools.partial(_indexer_kernel, n_sel=n_sel),
        out_shape=jax.ShapeDtypeStruct((batch, seq, seq), BF16),
        grid=(batch, seq // tq),
        in_specs=[pl.BlockSpec((None, tq, IDX_HEADS * IDX_DIM), lambda b, i: (b, i, 0)),
                  pl.BlockSpec((None, seq, IDX_DIM), lambda b, i: (b, 0, 0)),
                  pl.BlockSpec((None, tq, LANES), lambda b, i: (b, i, 0))],
        out_specs=pl.BlockSpec((None, tq, seq), lambda b, i: (b, i, 0)),
        scratch_shapes=[pltpu.VMEM((tq, seq), jnp.int32),
                        pltpu.VMEM((IDX_HEADS * tq, IDX_DIM), BF16),
                        pltpu.VMEM((IDX_HEADS, tq, LANES), F32),
                        pltpu.VMEM((tq, LANES), jnp.int32)],
        compiler_params=_params(("parallel", "arbitrary")),
        name="indexer_mask",
    )(iq.reshape(batch, seq, -1), ik.reshape(batch, seq, -1), iw.reshape(batch, seq, -1))


B_TQ = 256
B_TS = 512
B_REP = B_HEADS // B_KV_HEADS
B_CHUNK = 128
B_UNIT = 512


def _sparse_attn_kernel(qblk_ref, kblk_ref, q_ref, k_ref, v_ref, bias_ref, o_ref,
                        qs_ref, vo_ref, biasf_ref, s_ref, p_ref, alpha_ref, m_ref, accl_ref):
    tq, ts = bias_ref.shape
    rows = B_REP * tq
    step = pl.program_id(1)
    i, j = qblk_ref[step], kblk_ref[step]
    last = ((i + 1) * tq - 1) // ts

    @pl.when(j == 0)
    def _():
        scale = HEAD_DIM ** -0.5 * LOG2E
        for g in range(B_KV_HEADS):
            for r in range(B_REP):
                h = g * B_REP + r
                qh = q_ref[:, h * HEAD_DIM:(h + 1) * HEAD_DIM].astype(F32) * scale
                qs_ref[g, r * tq:(r + 1) * tq, :] = qh.astype(qs_ref.dtype)
        m_ref[...] = jnp.full(m_ref.shape, MASK_VALUE, F32)
        accl_ref[...] = jnp.zeros(accl_ref.shape, F32)

    biasf_ref[...] = bias_ref[...].astype(F32)
    ones = jnp.ones((ts, LANES), BF16)
    for g in range(B_KV_HEADS):
        vo_ref[g, :, :HEAD_DIM] = v_ref[:, g * HEAD_DIM:(g + 1) * HEAD_DIM]
        vo_ref[g, :, HEAD_DIM:] = ones

    unit_rows = min(B_UNIT, rows)
    units = [(g, r0) for g in range(B_KV_HEADS) for r0 in range(0, rows, unit_rows)]

    def logits(u):
        g, r0 = units[u]
        kg = k_ref[:, g * HEAD_DIM:(g + 1) * HEAD_DIM]
        s_ref[u % 2] = lax.dot_general(qs_ref[g, r0:r0 + unit_rows, :], kg, NT_DIMS,
                                       preferred_element_type=F32)

    logits(0)
    for u, (g, r0) in enumerate(units):
        if u + 1 < len(units):
            logits(u + 1)
        par = u % 2
        for c in range(0, unit_rows, B_CHUNK):
            rs = slice(c, c + B_CHUNK)
            gs = slice(r0 + c, r0 + c + B_CHUNK)
            qrow = (r0 + c) % tq
            s = s_ref[par, rs, :] + biasf_ref[qrow:qrow + B_CHUNK, :]
            m_prev = m_ref[g, gs, :]
            m_new = jnp.maximum(m_prev, jnp.max(s, axis=1, keepdims=True))
            p_ref[par, rs, :] = jnp.exp2(s - _lanes(m_new, ts)).astype(BF16)
            alpha_ref[par, rs, :] = jnp.exp2(m_prev - m_new)
            m_ref[g, gs, :] = m_new
        pv = jnp.dot(p_ref[par], vo_ref[g], preferred_element_type=F32)
        alpha = alpha_ref[par]
        us = slice(r0, r0 + unit_rows)
        accl_ref[g, us, :] = jnp.concatenate([alpha, alpha], axis=1) * accl_ref[g, us, :] + pv

    @pl.when(j == last)
    def _():
        for g in range(B_KV_HEADS):
            out = accl_ref[g, :, :HEAD_DIM] / accl_ref[g, :, HEAD_DIM:]
            for r in range(B_REP):
                h = g * B_REP + r
                o_ref[:, h * HEAD_DIM:(h + 1) * HEAD_DIM] = out[r * tq:(r + 1) * tq].astype(o_ref.dtype)


def sparse_attention(qkv, bias, batch, seq):
    tq, ts = _tile(seq, B_TQ), _tile(seq, B_TS)
    view = qkv.reshape(batch, seq, B_QKV_WIDTH)
    k_tile = B_Q_WIDTH // B_KV_WIDTH
    pairs = [(i, j) for i in range(seq // tq) for j in range(((i + 1) * tq - 1) // ts + 1)]
    qblk = jnp.asarray([p[0] for p in pairs], jnp.int32)
    kblk = jnp.asarray([p[1] for p in pairs], jnp.int32)
    rows = B_REP * tq
    unit_rows = min(B_UNIT, rows)
    grid_spec = pltpu.PrefetchScalarGridSpec(
        num_scalar_prefetch=2,
        grid=(batch, len(pairs)),
        in_specs=[pl.BlockSpec((None, tq, B_Q_WIDTH), lambda b, s, qb, kb: (b, qb[s], 0)),
                  pl.BlockSpec((None, ts, B_KV_WIDTH), lambda b, s, qb, kb: (b, kb[s], k_tile)),
                  pl.BlockSpec((None, ts, B_KV_WIDTH), lambda b, s, qb, kb: (b, kb[s], k_tile + 1)),
                  pl.BlockSpec((None, tq, ts), lambda b, s, qb, kb: (b, qb[s], kb[s]))],
        out_specs=pl.BlockSpec((None, tq, B_Q_WIDTH), lambda b, s, qb, kb: (b, qb[s], 0)),
        scratch_shapes=[pltpu.VMEM((B_KV_HEADS, rows, HEAD_DIM), BF16),
                        pltpu.VMEM((B_KV_HEADS, ts, 2 * HEAD_DIM), BF16),
                        pltpu.VMEM((tq, ts), F32),
                        pltpu.VMEM((2, unit_rows, ts), F32),
                        pltpu.VMEM((2, unit_rows, ts), BF16),
                        pltpu.VMEM((2, unit_rows, LANES), F32),
                        pltpu.VMEM((B_KV_HEADS, rows, LANES), F32),
                        pltpu.VMEM((B_KV_HEADS, rows, 2 * HEAD_DIM), F32)])
    out = pl.pallas_call(
        _sparse_attn_kernel,
        out_shape=jax.ShapeDtypeStruct((batch, seq, B_Q_WIDTH), BF16),
        grid_spec=grid_spec,
        compiler_params=_params(("parallel", "arbitrary")),
        name="sparse_attention",
    )(qblk, kblk, view, view, view, bias)
    return out.reshape(batch * seq, B_Q_WIDTH)


def _rope_tables(positions):
    inv_freq = ROPE_THETA ** (-jnp.arange(0, HEAD_DIM, 2, dtype=F32) / HEAD_DIM)
    ang = positions.astype(F32).reshape(-1, 1) * inv_freq
    cos, sin = jnp.cos(ang), jnp.sin(ang)
    return jnp.concatenate([cos, cos], axis=-1), jnp.concatenate([-sin, sin], axis=-1)


def _mlp(f_in, w_up, w_down, layer):
    u = matmul(f_in, cast_weight(w_up, layer), act="relu2")
    return matmul(u, cast_weight(w_down, layer))


def _mixer_dilated(a, cos2, sin2, w_in, w_out, layer, batch, seq):
    tn = 1024
    per_part = A_GROUP_WIDTH // tn
    rope_pred = lambda j: (j // per_part) % 3 != 2
    a, cos2, sin2 = (_to_residue_major(t, batch, seq) for t in (a, cos2, sin2))
    qkv = matmul(a, cast_weight(w_in, layer), rope=(cos2, sin2, rope_pred), tn=tn)
    outs, lses = zip(*[dilated_attention(qkv, batch, seq, g) for g in range(N_GROUPS)])
    o = _from_residue_major(merge_groups(outs, lses), batch, seq)
    return matmul(o, cast_weight(w_out, layer))


def _mixer_sparse(a, cos2, sin2, w_in, idx_q_norm, w_idx_up, idx_k_w, idx_k_b, w_out, layer, batch, seq):
    tn = 1024
    n_rope = (B_Q_WIDTH + B_KV_WIDTH) // tn
    idx_width = IDX_Q_RANK + IDX_DIM + LANES
    w_all = cast_weight(w_in, layer, n_pad=B_QKV_WIDTH + idx_width)
    qkv = matmul(a, w_all, cols=(0, B_QKV_WIDTH), rope=(cos2, sin2, lambda j: j < n_rope), tn=tn)
    lat = matmul(a, w_all, cols=(B_QKV_WIDTH, idx_width), out_dtype=F32, tn=256)
    iq_n, ik, iw = indexer_prep(lat, idx_q_norm[layer], idx_k_w[layer], idx_k_b[layer], cos2, sin2)
    iq = matmul(iq_n, cast_weight(w_idx_up, layer), rope=(cos2, sin2, lambda j: j >= 0), tn=tn)
    bias = indexer_mask(iq, ik, iw, batch, seq)
    o = sparse_attention(qkv, bias, batch, seq)
    return matmul(o, cast_weight(w_out, layer))


def kernel(x, positions, attn_pre_norm, attn_post_norm, mlp_pre_norm, mlp_post_norm, w_in_a, w_out_a,
           w_in_b, idx_q_norm, w_idx_up, idx_k_norm_w, idx_k_norm_b, w_out_b, w_up, w_down):
    batch, seq, d_model = x.shape
    depth = attn_pre_norm.shape[0]
    cos2, sin2 = _rope_tables(positions)
    h = x.reshape(batch * seq, d_model)
    a = rms_norm_cast(h, attn_pre_norm[0])
    for i in range(depth):
        j = i // 2
        if i % 2 == 0:
            m = _mixer_dilated(a, cos2, sin2, w_in_a, w_out_a, j, batch, seq)
        else:
            m = _mixer_sparse(a, cos2, sin2, w_in_b, idx_q_norm, w_idx_up,
                              idx_k_norm_w, idx_k_norm_b, w_out_b, j, batch, seq)
        h, f_in = residual_norm(h, m, attn_post_norm[i], mlp_pre_norm[i])
        f = _mlp(f_in, w_up, w_down, i)
        h, a = residual_norm(h, f, mlp_post_norm[i], attn_pre_norm[i + 1] if i + 1 < depth else None)
    return h.reshape(batch, seq, d_model)
```

```python
import functools
import math

import jax
import jax.numpy as jnp
from jax import lax
from jax.experimental import pallas as pl
from jax.experimental.pallas import tpu as pltpu

HEAD_DIM = 128
ROPE_THETA = 10000.0
NORM_EPS = 1e-6
DILATED_GROUPS = ((128, 1), (512, 4), (2048, 16))
N_GROUPS = 3
A_HEADS_PER_GROUP = 16
A_BLOCK = 128
A_GROUP_WIDTH = A_HEADS_PER_GROUP * HEAD_DIM
B_HEADS = 32
B_KV_HEADS = 8
IDX_HEADS = 32
IDX_DIM = 128
IDX_Q_RANK = 1024
TOPK_MAX = 256
B_Q_WIDTH = B_HEADS * HEAD_DIM
B_KV_WIDTH = B_KV_HEADS * HEAD_DIM
B_QKV_WIDTH = B_Q_WIDTH + 2 * B_KV_WIDTH

LANES = 128
VMEM_LIMIT_BYTES = 56 * 2**20
MASK_VALUE = -1e30
INT_MIN = -2**31
INT_MAX = 2**31 - 1
LOG2E = math.log2(math.e)
NT_DIMS = (((1,), (1,)), ((), ()))

F32 = jnp.float32
BF16 = jnp.bfloat16


def _tile(dim, pref):
    t = min(dim, pref)
    if dim % t:
        t = 1 << (t.bit_length() - 1)
    while dim % t:
        t //= 2
    return t


def _params(semantics):
    return pltpu.CompilerParams(dimension_semantics=semantics, vmem_limit_bytes=VMEM_LIMIT_BYTES)


def _lanes(x, width):
    return jnp.concatenate([x] * (width // LANES), axis=1)


def _rms(x, g):
    return x * lax.rsqrt(jnp.mean(x * x, axis=-1, keepdims=True) + NORM_EPS) * g


def _norm_kernel(x_ref, g_ref, a_ref):
    a_ref[...] = _rms(x_ref[...], g_ref[...]).astype(a_ref.dtype)


def rms_norm_cast(x, g):
    m, d = x.shape
    tm = _tile(m, 256)
    return pl.pallas_call(
        _norm_kernel,
        out_shape=jax.ShapeDtypeStruct((m, d), BF16),
        grid=(m // tm,),
        in_specs=[pl.BlockSpec((tm, d), lambda i: (i, 0)), pl.BlockSpec((1, d), lambda i: (0, 0))],
        out_specs=pl.BlockSpec((tm, d), lambda i: (i, 0)),
        compiler_params=_params(("parallel",)),
        name="rms_norm_cast",
    )(x, g.reshape(1, d))


def _residual_kernel(h_ref, m_ref, gpost_ref, *rest, with_next):
    h = h_ref[...] + _rms(m_ref[...].astype(F32), gpost_ref[...])
    if with_next:
        gpre_ref, h_out_ref, a_ref = rest
        a_ref[...] = _rms(h, gpre_ref[...]).astype(a_ref.dtype)
    else:
        (h_out_ref,) = rest
    h_out_ref[...] = h


def residual_norm(h, m, g_post, g_pre_next=None):
    rows, d = h.shape
    tm = _tile(rows, 256)
    with_next = g_pre_next is not None
    row_spec = pl.BlockSpec((tm, d), lambda i: (i, 0))
    g_spec = pl.BlockSpec((1, d), lambda i: (0, 0))
    args = [h, m, g_post.reshape(1, d)]
    in_specs = [row_spec, row_spec, g_spec]
    out_shape = [jax.ShapeDtypeStruct((rows, d), F32)]
    out_specs = [row_spec]
    if with_next:
        args.append(g_pre_next.reshape(1, d))
        in_specs.append(g_spec)
        out_shape.append(jax.ShapeDtypeStruct((rows, d), BF16))
        out_specs.append(row_spec)
    out = pl.pallas_call(
        functools.partial(_residual_kernel, with_next=with_next),
        out_shape=out_shape,
        grid=(rows // tm,),
        in_specs=in_specs,
        out_specs=out_specs,
        compiler_params=_params(("parallel",)),
        name="residual_norm",
    )(*args)
    return (out[0], out[1]) if with_next else (out[0], None)


CAST_BLOCK_BYTES = 8 * 2**20


def _cast_kernel(w_ref, o_ref):
    rows, n = w_ref.shape
    n_pad = o_ref.shape[1]
    n_full = n // LANES * LANES
    o_ref[:, :n_full] = w_ref[:, :n_full].astype(o_ref.dtype)
    if n_pad > n_full:
        tail = [w_ref[:, n_full:].astype(o_ref.dtype)] if n > n_full else []
        o_ref[:, n_full:] = jnp.concatenate(tail + [jnp.zeros((rows, n_pad - n), o_ref.dtype)], axis=1)


def cast_weight(w, layer, n_pad=None):
    _, kdim, n = w.shape
    n_pad = n if n_pad is None else n_pad
    tk = _tile(kdim, max(8, CAST_BLOCK_BYTES // (4 * n)))
    return pl.pallas_call(
        _cast_kernel,
        out_shape=jax.ShapeDtypeStruct((kdim, n_pad), BF16),
        grid=(kdim // tk,),
        in_specs=[pl.BlockSpec((None, tk, n), lambda i: (layer, i, 0))],
        out_specs=pl.BlockSpec((tk, n_pad), lambda i: (i, 0)),
        compiler_params=_params(("parallel",)),
        name="cast_weight",
    )(w)


MM_CHUNK = 256


def _rope(x, cos2, sin2):
    return x * cos2 + pltpu.roll(x, HEAD_DIM // 2, 1) * sin2


def _matmul_kernel(a_ref, w_ref, *rest, nk, act, rope_pred, n_split):
    if rope_pred is not None:
        cos_ref, sin_ref, *rest = rest
        do_rope = rope_pred(pl.program_id(1))
        cos2 = jnp.where(do_rope, cos_ref[...], 1.0)
        sin2 = jnp.where(do_rope, sin_ref[...], 0.0)
    o_ref, *scratch = rest
    sub = o_ref.shape[1] // n_split
    if nk > 1:
        (acc_ref,) = scratch

        @pl.when(pl.program_id(2) == 0)
        def _():
            acc_ref[...] = jnp.zeros(acc_ref.shape, F32)

    for c in range(n_split):
        acc = jnp.dot(a_ref[...], w_ref[:, c * sub:(c + 1) * sub], preferred_element_type=F32)
        if nk > 1:
            acc = acc_ref[c] + acc
            acc_ref[c] = acc
        if act == "relu2":
            r = jnp.maximum(acc, 0.0)
            acc = r * r
        if rope_pred is None:
            o_ref[:, c * sub:(c + 1) * sub] = acc.astype(o_ref.dtype)
        else:
            for hh in range(sub // HEAD_DIM):
                x = acc[:, hh * HEAD_DIM:(hh + 1) * HEAD_DIM]
                col = c * sub + hh * HEAD_DIM
                o_ref[:, col:col + HEAD_DIM] = _rope(x, cos2, sin2).astype(o_ref.dtype)


def matmul(a, w, *, cols=None, out_dtype=BF16, act=None, rope=None, tm=1024, tn=1024, tk=4096):
    m, kdim = a.shape
    col0, n = (0, w.shape[1]) if cols is None else cols
    tm, tn, tk = _tile(m, tm), _tile(n, tn), _tile(kdim, tk)
    assert col0 % tn == 0
    j0 = col0 // tn
    nk = kdim // tk
    in_specs = [pl.BlockSpec((tm, tk), lambda i, j, k: (i, k)),
                pl.BlockSpec((tk, tn), lambda i, j, k: (k, j + j0))]
    args = [a, w]
    rope_pred = None
    if rope is not None:
        cos2, sin2, rope_pred = rope
        tab_spec = pl.BlockSpec((tm, HEAD_DIM), lambda i, j, k: (i, 0))
        in_specs += [tab_spec, tab_spec]
        args += [cos2, sin2]
    n_split = max(1, tn // MM_CHUNK)
    scratch = [pltpu.VMEM((n_split, tm, tn // n_split), F32)] if nk > 1 else []
    return pl.pallas_call(
        functools.partial(_matmul_kernel, nk=nk, act=act, rope_pred=rope_pred, n_split=n_split),
        out_shape=jax.ShapeDtypeStruct((m, n), out_dtype),
        grid=(m // tm, n // tn, nk),
        in_specs=in_specs,
        out_specs=pl.BlockSpec((tm, tn), lambda i, j, k: (i, j)),
        scratch_shapes=scratch,
        compiler_params=_params(("parallel", "parallel", "arbitrary")),
        name="matmul",
    )(*args)


A_PERIOD = 16


def _to_residue_major(x, batch, seq):
    u = seq // A_PERIOD
    return x.reshape(batch, u, A_PERIOD, -1).transpose(0, 2, 1, 3).reshape(batch * seq, -1)


def _from_residue_major(x, batch, seq):
    u = seq // A_PERIOD
    return x.reshape(batch, A_PERIOD, u, -1).transpose(0, 2, 1, 3).reshape(batch * seq, -1)


def _dilated_kernel(q_ref, kp_ref, kc_ref, vp_ref, vc_ref, o_ref, lse_ref, sp_ref, sc_ref,
                    *, w, rows, local_index, blk_axis):
    has_prev = pl.program_id(blk_axis) > 0
    qi = local_index(lax.broadcasted_iota(jnp.int32, (rows, rows), 0))
    kj = local_index(lax.broadcasted_iota(jnp.int32, (rows, rows), 1))
    mask_prev = jnp.logical_and(qi + rows - kj <= w, has_prev)
    mask_cur = jnp.logical_and(kj <= qi, qi - kj <= w)
    scale = HEAD_DIM ** -0.5 * LOG2E
    lead = (slice(None),) * (len(q_ref.shape) - 1)
    out_shape = o_ref.shape[:-1] + (HEAD_DIM,)

    def head(ref, h):
        return ref[lead + (slice(h * HEAD_DIM, (h + 1) * HEAD_DIM),)].reshape(rows, HEAD_DIM)

    def stage(h):
        q = head(q_ref, h)
        sp_ref[h % 2] = lax.dot_general(q, head(kp_ref, h), NT_DIMS, preferred_element_type=F32)
        sc_ref[h % 2] = lax.dot_general(q, head(kc_ref, h), NT_DIMS, preferred_element_type=F32)

    stage(0)
    for h in range(A_HEADS_PER_GROUP):
        if h + 1 < A_HEADS_PER_GROUP:
            stage(h + 1)
        s_prev = jnp.where(mask_prev, sp_ref[h % 2] * scale, MASK_VALUE)
        s_cur = jnp.where(mask_cur, sc_ref[h % 2] * scale, MASK_VALUE)
        m = jnp.maximum(jnp.max(s_prev, axis=1, keepdims=True), jnp.max(s_cur, axis=1, keepdims=True))
        p_prev = jnp.exp2(s_prev - m)
        p_cur = jnp.exp2(s_cur - m)
        l = jnp.sum(p_prev, axis=1, keepdims=True) + jnp.sum(p_cur, axis=1, keepdims=True)
        o = jnp.dot(p_prev.astype(BF16), head(vp_ref, h), preferred_element_type=F32)
        o += jnp.dot(p_cur.astype(BF16), head(vc_ref, h), preferred_element_type=F32)
        idx = lead + (slice(h * HEAD_DIM, (h + 1) * HEAD_DIM),)
        o_ref[idx] = (o / l).astype(o_ref.dtype).reshape(out_shape)
        lse_ref[idx] = jnp.broadcast_to(m * (1.0 / LOG2E) + jnp.log(l), (rows, HEAD_DIM)).reshape(out_shape)


def dilated_attention(qkv, batch, seq, group):
    window, r = DILATED_GROUPS[group]
    w = window // r
    u = seq // A_PERIOD
    width = qkv.shape[1]
    view = qkv.reshape(batch, 4, 4, u, width)
    if r == 16:
        rows, n_blk = A_BLOCK, u // A_BLOCK
        block = (None, None, None, rows, A_GROUP_WIDTH)
        grid = (batch, 4, 4, n_blk)
        place = lambda g, blk: (g[0], g[1], g[2], blk)
        local_index = lambda rho: rho
    elif r == 4:
        rows, n_blk = A_BLOCK, u // (A_BLOCK // 4)
        block = (None, 4, None, rows // 4, A_GROUP_WIDTH)
        grid = (batch, 4, n_blk)
        place = lambda g, blk: (g[0], 0, g[1], blk)
        local_index = lambda rho: 4 * (rho % (rows // 4)) + rho // (rows // 4)
    else:
        assert r == 1
        rows, n_blk = 2 * A_BLOCK, u // (2 * A_BLOCK // A_PERIOD)
        per = rows // A_PERIOD
        block = (None, 4, 4, per, A_GROUP_WIDTH)
        grid = (batch, n_blk)
        place = lambda g, blk: (g[0], 0, 0, blk)
        local_index = lambda rho: A_PERIOD * (rho % per) + rho // per
    assert w <= rows and n_blk * rows * r == seq

    def spec(which, prev):
        def index(*g):
            blk = jnp.maximum(g[-1] - 1, 0) if prev else g[-1]
            return place(g, blk) + (group * 3 + which,)
        return pl.BlockSpec(block, index)

    out_spec = pl.BlockSpec(block, lambda *g: place(g, g[-1]) + (0,))
    o, lse = pl.pallas_call(
        functools.partial(_dilated_kernel, w=w, rows=rows, local_index=local_index, blk_axis=len(grid) - 1),
        out_shape=[jax.ShapeDtypeStruct((batch, 4, 4, u, A_GROUP_WIDTH), BF16),
                   jax.ShapeDtypeStruct((batch, 4, 4, u, A_GROUP_WIDTH), F32)],
        grid=grid,
        in_specs=[spec(0, False), spec(1, True), spec(1, False), spec(2, True), spec(2, False)],
        out_specs=[out_spec, out_spec],
        scratch_shapes=[pltpu.VMEM((2, rows, rows), F32),
                        pltpu.VMEM((2, rows, rows), F32)],
        compiler_params=_params(("parallel",) * (len(grid) - 1) + ("arbitrary",)),
        name="dilated_attention",
    )(view, view, view, view, view)
    return o.reshape(batch * seq, A_GROUP_WIDTH), lse.reshape(batch * seq, A_GROUP_WIDTH)


def _merge_kernel(o0_ref, o1_ref, o2_ref, l0_ref, l1_ref, l2_ref, out_ref):
    l0, l1, l2 = l0_ref[...], l1_ref[...], l2_ref[...]
    m = jnp.maximum(jnp.maximum(l0, l1), l2)
    e0, e1, e2 = jnp.exp(l0 - m), jnp.exp(l1 - m), jnp.exp(l2 - m)
    num = e0 * o0_ref[...].astype(F32) + e1 * o1_ref[...].astype(F32) + e2 * o2_ref[...].astype(F32)
    out_ref[...] = (num / (e0 + e1 + e2)).astype(out_ref.dtype)


def merge_groups(outs, lses):
    rows, width = outs[0].shape
    tm = _tile(rows, 256)
    spec = pl.BlockSpec((tm, width), lambda i: (i, 0))
    return pl.pallas_call(
        _merge_kernel,
        out_shape=jax.ShapeDtypeStruct((rows, width), BF16),
        grid=(rows // tm,),
        in_specs=[spec] * 6,
        out_specs=spec,
        compiler_params=_params(("parallel",)),
        name="merge_groups",
    )(*outs, *lses)


def _indexer_prep_kernel(lat_ref, gq_ref, lnw_ref, lnb_ref, cos_ref, sin_ref, iqn_ref, ik_ref, iw_ref):
    iqn_ref[...] = _rms(lat_ref[:, :IDX_Q_RANK], gq_ref[...]).astype(iqn_ref.dtype)
    k = lat_ref[:, IDX_Q_RANK:IDX_Q_RANK + IDX_DIM]
    kc = k - jnp.mean(k, axis=-1, keepdims=True)
    kn = kc * lax.rsqrt(jnp.mean(kc * kc, axis=-1, keepdims=True) + NORM_EPS) * lnw_ref[...] + lnb_ref[...]
    ik_ref[...] = _rope(kn, cos_ref[...], sin_ref[...]).astype(ik_ref.dtype)
    iw_ref[...] = lat_ref[:, IDX_Q_RANK + IDX_DIM:] * (IDX_HEADS ** -0.5 * IDX_DIM ** -0.5)


def indexer_prep(lat, gq, lnw, lnb, cos2, sin2):
    rows, width = lat.shape
    tm = _tile(rows, 512)
    row = lambda wd: pl.BlockSpec((tm, wd), lambda i: (i, 0))
    vec = lambda wd: pl.BlockSpec((1, wd), lambda i: (0, 0))
    return pl.pallas_call(
        _indexer_prep_kernel,
        out_shape=[jax.ShapeDtypeStruct((rows, IDX_Q_RANK), BF16),
                   jax.ShapeDtypeStruct((rows, IDX_DIM), BF16),
                   jax.ShapeDtypeStruct((rows, LANES), F32)],
        grid=(rows // tm,),
        in_specs=[row(width), vec(IDX_Q_RANK), vec(IDX_DIM), vec(IDX_DIM), row(HEAD_DIM), row(HEAD_DIM)],
        out_specs=[row(IDX_Q_RANK), row(IDX_DIM), row(LANES)],
        compiler_params=_params(("parallel",)),
        name="indexer_prep",
    )(lat, gq.reshape(1, -1), lnw.reshape(1, -1), lnb.reshape(1, -1), cos2, sin2)


def _ordered_bits(x):
    bits = lax.bitcast_convert_type(x, jnp.int32)
    key = jnp.where(bits >= 0, bits, bits ^ jnp.int32(0x7FFFFFFF))
    return jnp.where(key == -1, 0, key)


def _score_of(key):
    return lax.bitcast_convert_type(jnp.where(key >= 0, key, key ^ jnp.int32(0x7FFFFFFF)), F32)


IDX_TQ = 128
IDX_TS = 512
IDX_HEAD_BATCH = 8
IDX_SCORE_STEPS = 24


def _indexer_kernel(iq_ref, ik_ref, iw_ref, bias_ref, key_ref, qs_ref, wb_ref, *, n_sel):
    tq, seq = key_ref.shape
    i = pl.program_id(1)
    n_chunks = ((i + 1) * tq + IDX_TS - 1) // IDX_TS
    key_ref[...] = jnp.full((tq, seq), INT_MIN, jnp.int32)
    iw = iw_ref[...]
    for h in range(IDX_HEADS):
        qs_ref[h * tq:(h + 1) * tq, :] = iq_ref[:, h * IDX_DIM:(h + 1) * IDX_DIM]
        wb_ref[h] = jnp.broadcast_to(iw[:, h:h + 1], (tq, LANES))
    t_pos = i * tq + lax.broadcasted_iota(jnp.int32, (tq, IDX_TS), 0)
    lane = lax.broadcasted_iota(jnp.int32, (tq, IDX_TS), 1)
    hb_rows = IDX_HEAD_BATCH * tq

    def score_chunk(c, carry):
        start = pl.multiple_of(c * IDX_TS, IDX_TS)
        keys = ik_ref[pl.ds(start, IDX_TS), :]
        acc = jnp.zeros((tq, IDX_TS), F32)
        for hb in range(IDX_HEADS // IDX_HEAD_BATCH):
            rel = lax.dot_general(qs_ref[hb * hb_rows:(hb + 1) * hb_rows, :], keys, NT_DIMS,
                                  preferred_element_type=F32)
            for hh in range(IDX_HEAD_BATCH):
                wb = _lanes(wb_ref[hb * IDX_HEAD_BATCH + hh], IDX_TS)
                acc += jnp.maximum(rel[hh * tq:(hh + 1) * tq], 0.0) * wb
        causal = start + lane <= t_pos
        key_ref[:, pl.ds(start, IDX_TS)] = jnp.where(causal, _ordered_bits(acc), INT_MIN)
        hi_part = jnp.where(causal, acc, -jnp.inf)
        lo_part = jnp.where(causal, acc, jnp.inf)
        smax, smin = carry
        for b in range(IDX_TS // LANES):
            smax = jnp.maximum(smax, hi_part[:, b * LANES:(b + 1) * LANES])
            smin = jnp.minimum(smin, lo_part[:, b * LANES:(b + 1) * LANES])
        return smax, smin

    smax, smin = lax.fori_loop(0, n_chunks, score_chunk,
                               (jnp.full((tq, LANES), -jnp.inf, F32), jnp.full((tq, LANES), jnp.inf, F32)))
    smax = jnp.max(smax, axis=1, keepdims=True)
    smin = jnp.min(smin, axis=1, keepdims=True)

    def count_where(pred):
        def body(c, cnt):
            start = pl.multiple_of(c * IDX_TS, IDX_TS)
            hit = jnp.where(pred(key_ref[:, pl.ds(start, IDX_TS)], start + lane), 1.0, 0.0)
            for b in range(IDX_TS // LANES):
                cnt = cnt + hit[:, b * LANES:(b + 1) * LANES]
            return cnt

        cnt = lax.fori_loop(0, n_chunks, body, jnp.zeros((tq, LANES), F32))
        return jnp.sum(cnt, axis=1, keepdims=True)

    def count_ge(cand):
        cand_w = _lanes(cand, IDX_TS)
        return count_where(lambda keys, pos: keys >= cand_w)

    ones_i = jnp.ones((tq, LANES), jnp.int32)
    n_causal = i * tq + 1 + lax.broadcasted_iota(jnp.int32, (tq, LANES), 0)
    lo0 = _ordered_bits(smin) * ones_i
    hi0 = _ordered_bits(smax) * ones_i + 1
    open0 = jnp.where(n_causal > n_sel, 1.0, 0.0)
    thr0 = jnp.full((tq, LANES), INT_MIN + 1, jnp.int32)

    def search_cond(state):
        step, _, n_open = state
        return jnp.logical_and(step < IDX_SCORE_STEPS + 33, n_open > 0.0)

    def search_step(state):
        step, (lo, hi, n_lo, thr, n_thr, still_open), _ = state
        score_mid = _ordered_bits(0.5 * _score_of(lo) + 0.5 * _score_of(hi))
        key_mid = (lo >> 1) + (hi >> 1) + (lo & hi & 1)
        cand = jnp.where(step < IDX_SCORE_STEPS, score_mid, key_mid)
        cand = jnp.maximum(jnp.minimum(cand, hi - 1), lo + 1)
        cnt = count_ge(cand)
        enough = cnt >= n_sel
        lo = jnp.where(enough, cand, lo)
        n_lo = jnp.where(enough, cnt, n_lo)
        hi = jnp.where(enough, hi, cand)
        settled = jnp.logical_and(still_open > 0.0, jnp.logical_or(cnt == n_sel, hi <= lo + 1))
        thr = jnp.where(settled, lo, thr)
        n_thr = jnp.where(settled, n_lo, n_thr)
        still_open = jnp.where(settled, 0.0, still_open)
        return step + 1, (lo, hi, n_lo, thr, n_thr, still_open), jnp.max(still_open)

    n_causal_f = n_causal.astype(F32)
    init = (lo0, hi0, n_causal_f, thr0, jnp.minimum(n_causal_f, n_sel), open0)
    _, (_, _, _, thr, n_thr, _), _ = lax.while_loop(search_cond, search_step,
                                                    (jnp.int32(0), init, jnp.max(open0)))
    thr_w = _lanes(thr, IDX_TS)

    tied = n_thr > n_sel
    any_tied = jnp.max(jnp.where(tied, 1.0, 0.0)) > 0.0

    @pl.when(jnp.logical_not(any_tied))
    def _():
        for c in range(seq // IDX_TS):
            sl = slice(c * IDX_TS, (c + 1) * IDX_TS)
            bias_ref[:, sl] = jnp.where(key_ref[:, sl] >= thr_w, 0.0, MASK_VALUE).astype(bias_ref.dtype)

    @pl.when(any_tied)
    def _():
        need = n_sel - count_ge(thr + 1)

        def halve(_, bracket):
            below, upto = bracket
            mid = (below + upto) >> 1
            mid_w = _lanes(mid, IDX_TS)
            cnt = count_where(lambda keys, pos: jnp.logical_and(keys == thr_w, pos <= mid_w))
            ok = cnt >= need
            return jnp.where(ok, below, mid), jnp.where(ok, mid, upto)

        _, upto = lax.fori_loop(0, seq.bit_length(), halve,
                                (jnp.full((tq, LANES), -1, jnp.int32), jnp.full((tq, LANES), seq - 1, jnp.int32)))
        last_tie_w = _lanes(jnp.where(tied, upto, INT_MAX), IDX_TS)
        for c in range(seq // IDX_TS):
            sl = slice(c * IDX_TS, (c + 1) * IDX_TS)
            keys = key_ref[:, sl]
            at_thr = jnp.logical_and(keys == thr_w, c * IDX_TS + lane <= last_tie_w)
            bias_ref[:, sl] = jnp.where(jnp.logical_or(keys > thr_w, at_thr), 0.0, MASK_VALUE).astype(bias_ref.dtype)


def indexer_mask(iq, ik, iw, batch, seq):
    n_sel = min(TOPK_MAX, seq // 4)
    tq = IDX_TQ
    assert seq % IDX_TS == 0 and seq % tq == 0
    return pl.pallas_call(
        functools.partial(_indexer_kernel, n_sel=n_sel),
        out_shape=jax.ShapeDtypeStruct((batch, seq, seq), BF16),
        grid=(batch, seq // tq),
        in_specs=[pl.BlockSpec((None, tq, IDX_HEADS * IDX_DIM), lambda b, i: (b, i, 0)),
                  pl.BlockSpec((None, seq, IDX_DIM), lambda b, i: (b, 0, 0)),
                  pl.BlockSpec((None, tq, LANES), lambda b, i: (b, i, 0))],
        out_specs=pl.BlockSpec((None, tq, seq), lambda b, i: (b, i, 0)),
        scratch_shapes=[pltpu.VMEM((tq, seq), jnp.int32),
                        pltpu.VMEM((IDX_HEADS * tq, IDX_DIM), BF16),
                        pltpu.VMEM((IDX_HEADS, tq, LANES), F32)],
        compiler_params=_params(("parallel", "arbitrary")),
        name="indexer_mask",
    )(iq.reshape(batch, seq, -1), ik.reshape(batch, seq, -1), iw.reshape(batch, seq, -1))


B_TQ = 256
B_TS = 512
B_REP = B_HEADS // B_KV_HEADS
B_CHUNK = 128
B_UNIT = 512


def _sparse_attn_kernel(qblk_ref, kblk_ref, q_ref, k_ref, v_ref, bias_ref, o_ref,
                        qs_ref, vo_ref, biasf_ref, s_ref, p_ref, alpha_ref, m_ref, accl_ref):
    tq, ts = bias_ref.shape
    rows = B_REP * tq
    step = pl.program_id(1)
    i, j = qblk_ref[step], kblk_ref[step]
    last = ((i + 1) * tq - 1) // ts

    @pl.when(j == 0)
    def _():
        scale = HEAD_DIM ** -0.5 * LOG2E
        for g in range(B_KV_HEADS):
            for r in range(B_REP):
                h = g * B_REP + r
                qh = q_ref[:, h * HEAD_DIM:(h + 1) * HEAD_DIM].astype(F32) * scale
                qs_ref[g, r * tq:(r + 1) * tq, :] = qh.astype(qs_ref.dtype)
        m_ref[...] = jnp.full(m_ref.shape, MASK_VALUE, F32)
        accl_ref[...] = jnp.zeros(accl_ref.shape, F32)

    biasf_ref[...] = bias_ref[...].astype(F32)
    ones = jnp.ones((ts, LANES), BF16)
    for g in range(B_KV_HEADS):
        vo_ref[g, :, :HEAD_DIM] = v_ref[:, g * HEAD_DIM:(g + 1) * HEAD_DIM]
        vo_ref[g, :, HEAD_DIM:] = ones

    unit_rows = min(B_UNIT, rows)
    units = [(g, r0) for g in range(B_KV_HEADS) for r0 in range(0, rows, unit_rows)]

    def logits(u):
        g, r0 = units[u]
        kg = k_ref[:, g * HEAD_DIM:(g + 1) * HEAD_DIM]
        s_ref[u % 2] = lax.dot_general(qs_ref[g, r0:r0 + unit_rows, :], kg, NT_DIMS,
                                       preferred_element_type=F32)

    logits(0)
    for u, (g, r0) in enumerate(units):
        if u + 1 < len(units):
            logits(u + 1)
        par = u % 2
        for c in range(0, unit_rows, B_CHUNK):
            rs = slice(c, c + B_CHUNK)
            gs = slice(r0 + c, r0 + c + B_CHUNK)
            qrow = (r0 + c) % tq
            s = s_ref[par, rs, :] + biasf_ref[qrow:qrow + B_CHUNK, :]
            m_prev = m_ref[g, gs, :]
            m_new = jnp.maximum(m_prev, jnp.max(s, axis=1, keepdims=True))
            p_ref[par, rs, :] = jnp.exp2(s - _lanes(m_new, ts)).astype(BF16)
            alpha_ref[par, rs, :] = jnp.exp2(m_prev - m_new)
            m_ref[g, gs, :] = m_new
        pv = jnp.dot(p_ref[par], vo_ref[g], preferred_element_type=F32)
        alpha = alpha_ref[par]
        us = slice(r0, r0 + unit_rows)
        accl_ref[g, us, :] = jnp.concatenate([alpha, alpha], axis=1) * accl_ref[g, us, :] + pv

    @pl.when(j == last)
    def _():
        for g in range(B_KV_HEADS):
            out = accl_ref[g, :, :HEAD_DIM] / accl_ref[g, :, HEAD_DIM:]
            for r in range(B_REP):
                h = g * B_REP + r
                o_ref[:, h * HEAD_DIM:(h + 1) * HEAD_DIM] = out[r * tq:(r + 1) * tq].astype(o_ref.dtype)


def sparse_attention(qkv, bias, batch, seq):
    tq, ts = _tile(seq, B_TQ), _tile(seq, B_TS)
    view = qkv.reshape(batch, seq, B_QKV_WIDTH)
    k_tile = B_Q_WIDTH // B_KV_WIDTH
    pairs = [(i, j) for i in range(seq // tq) for j in range(((i + 1) * tq - 1) // ts + 1)]
    qblk = jnp.asarray([p[0] for p in pairs], jnp.int32)
    kblk = jnp.asarray([p[1] for p in pairs], jnp.int32)
    rows = B_REP * tq
    unit_rows = min(B_UNIT, rows)
    grid_spec = pltpu.PrefetchScalarGridSpec(
        num_scalar_prefetch=2,
        grid=(batch, len(pairs)),
        in_specs=[pl.BlockSpec((None, tq, B_Q_WIDTH), lambda b, s, qb, kb: (b, qb[s], 0)),
                  pl.BlockSpec((None, ts, B_KV_WIDTH), lambda b, s, qb, kb: (b, kb[s], k_tile)),
                  pl.BlockSpec((None, ts, B_KV_WIDTH), lambda b, s, qb, kb: (b, kb[s], k_tile + 1)),
                  pl.BlockSpec((None, tq, ts), lambda b, s, qb, kb: (b, qb[s], kb[s]))],
        out_specs=pl.BlockSpec((None, tq, B_Q_WIDTH), lambda b, s, qb, kb: (b, qb[s], 0)),
        scratch_shapes=[pltpu.VMEM((B_KV_HEADS, rows, HEAD_DIM), BF16),
                        pltpu.VMEM((B_KV_HEADS, ts, 2 * HEAD_DIM), BF16),
                        pltpu.VMEM((tq, ts), F32),
                        pltpu.VMEM((2, unit_rows, ts), F32),
                        pltpu.VMEM((2, unit_rows, ts), BF16),
                        pltpu.VMEM((2, unit_rows, LANES), F32),
                        pltpu.VMEM((B_KV_HEADS, rows, LANES), F32),
                        pltpu.VMEM((B_KV_HEADS, rows, 2 * HEAD_DIM), F32)])
    out = pl.pallas_call(
        _sparse_attn_kernel,
        out_shape=jax.ShapeDtypeStruct((batch, seq, B_Q_WIDTH), BF16),
        grid_spec=grid_spec,
        compiler_params=_params(("parallel", "arbitrary")),
        name="sparse_attention",
    )(qblk, kblk, view, view, view, bias)
    return out.reshape(batch * seq, B_Q_WIDTH)


def _rope_tables(positions):
    inv_freq = ROPE_THETA ** (-jnp.arange(0, HEAD_DIM, 2, dtype=F32) / HEAD_DIM)
    ang = positions.astype(F32).reshape(-1, 1) * inv_freq
    cos, sin = jnp.cos(ang), jnp.sin(ang)
    return jnp.concatenate([cos, cos], axis=-1), jnp.concatenate([-sin, sin], axis=-1)


def _mlp(f_in, w_up, w_down, layer):
    u = matmul(f_in, cast_weight(w_up, layer), act="relu2")
    return matmul(u, cast_weight(w_down, layer))


def _mixer_dilated(a, cos2, sin2, w_in, w_out, layer, batch, seq):
    tn = 1024
    per_part = A_GROUP_WIDTH // tn
    rope_pred = lambda j: (j // per_part) % 3 != 2
    a, cos2, sin2 = (_to_residue_major(t, batch, seq) for t in (a, cos2, sin2))
    qkv = matmul(a, cast_weight(w_in, layer), rope=(cos2, sin2, rope_pred), tn=tn)
    outs, lses = zip(*[dilated_attention(qkv, batch, seq, g) for g in range(N_GROUPS)])
    o = _from_residue_major(merge_groups(outs, lses), batch, seq)
    return matmul(o, cast_weight(w_out, layer))


def _mixer_sparse(a, cos2, sin2, w_in, idx_q_norm, w_idx_up, idx_k_w, idx_k_b, w_out, layer, batch, seq):
    tn = 1024
    n_rope = (B_Q_WIDTH + B_KV_WIDTH) // tn
    idx_width = IDX_Q_RANK + IDX_DIM + LANES
    w_all = cast_weight(w_in, layer, n_pad=B_QKV_WIDTH + idx_width)
    qkv = matmul(a, w_all, cols=(0, B_QKV_WIDTH), rope=(cos2, sin2, lambda j: j < n_rope), tn=tn)
    lat = matmul(a, w_all, cols=(B_QKV_WIDTH, idx_width), out_dtype=F32, tn=256)
    iq_n, ik, iw = indexer_prep(lat, idx_q_norm[layer], idx_k_w[layer], idx_k_b[layer], cos2, sin2)
    iq = matmul(iq_n, cast_weight(w_idx_up, layer), rope=(cos2, sin2, lambda j: j >= 0), tn=tn)
    bias = indexer_mask(iq, ik, iw, batch, seq)
    o = sparse_attention(qkv, bias, batch, seq)
    return matmul(o, cast_weight(w_out, layer))


def kernel(x, positions, attn_pre_norm, attn_post_norm, mlp_pre_norm, mlp_post_norm, w_in_a, w_out_a,
           w_in_b, idx_q_norm, w_idx_up, idx_k_norm_w, idx_k_norm_b, w_out_b, w_up, w_down):
    batch, seq, d_model = x.shape
    depth = attn_pre_norm.shape[0]
    cos2, sin2 = _rope_tables(positions)
    h = x.reshape(batch * seq, d_model)
    a = rms_norm_cast(h, attn_pre_norm[0])
    for i in range(depth):
        j = i // 2
        if i % 2 == 0:
            m = _mixer_dilated(a, cos2, sin2, w_in_a, w_out_a, j, batch, seq)
        else:
            m = _mixer_sparse(a, cos2, sin2, w_in_b, idx_q_norm, w_idx_up,
                              idx_k_norm_w, idx_k_norm_b, w_out_b, j, batch, seq)
        h, f_in = residual_norm(h, m, attn_post_norm[i], mlp_pre_norm[i])
        f = _mlp(f_in, w_up, w_down, i)
        h, a = residual_norm(h, f, mlp_post_norm[i], attn_pre_norm[i + 1] if i + 1 < depth else None)
    return h.reshape(batch, seq, d_model)
```

```python
import functools
import math

import jax
import jax.numpy as jnp
from jax import lax
from jax.experimental import pallas as pl
from jax.experimental.pallas import tpu as pltpu

HEAD_DIM = 128
ROPE_THETA = 10000.0
NORM_EPS = 1e-6
DILATED_GROUPS = ((128, 1), (512, 4), (2048, 16))
N_GROUPS = 3
A_HEADS_PER_GROUP = 16
A_BLOCK = 128
A_GROUP_WIDTH = A_HEADS_PER_GROUP * HEAD_DIM
B_HEADS = 32
B_KV_HEADS = 8
IDX_HEADS = 32
IDX_DIM = 128
IDX_Q_RANK = 1024
TOPK_MAX = 256
B_Q_WIDTH = B_HEADS * HEAD_DIM
B_KV_WIDTH = B_KV_HEADS * HEAD_DIM
B_QKV_WIDTH = B_Q_WIDTH + 2 * B_KV_WIDTH

LANES = 128
VMEM_LIMIT_BYTES = 56 * 2**20
MASK_VALUE = -1e30
INT_MIN = -2**31
INT_MAX = 2**31 - 1
LOG2E = math.log2(math.e)
NT_DIMS = (((1,), (1,)), ((), ()))

F32 = jnp.float32
BF16 = jnp.bfloat16


def _tile(dim, pref):
    t = min(dim, pref)
    if dim % t:
        t = 1 << (t.bit_length() - 1)
    while dim % t:
        t //= 2
    return t


def _params(semantics):
    return pltpu.CompilerParams(dimension_semantics=semantics, vmem_limit_bytes=VMEM_LIMIT_BYTES)


def _lanes(x, width):
    return jnp.concatenate([x] * (width // LANES), axis=1)


def _rms(x, g):
    return x * lax.rsqrt(jnp.mean(x * x, axis=-1, keepdims=True) + NORM_EPS) * g


def _norm_kernel(x_ref, g_ref, a_ref):
    a_ref[...] = _rms(x_ref[...], g_ref[...]).astype(a_ref.dtype)


def rms_norm_cast(x, g):
    m, d = x.shape
    tm = _tile(m, 256)
    return pl.pallas_call(
        _norm_kernel,
        out_shape=jax.ShapeDtypeStruct((m, d), BF16),
        grid=(m // tm,),
        in_specs=[pl.BlockSpec((tm, d), lambda i: (i, 0)), pl.BlockSpec((1, d), lambda i: (0, 0))],
        out_specs=pl.BlockSpec((tm, d), lambda i: (i, 0)),
        compiler_params=_params(("parallel",)),
        name="rms_norm_cast",
    )(x, g.reshape(1, d))


def _residual_kernel(h_ref, m_ref, gpost_ref, *rest, with_next):
    h = h_ref[...] + _rms(m_ref[...].astype(F32), gpost_ref[...])
    if with_next:
        gpre_ref, h_out_ref, a_ref = rest
        a_ref[...] = _rms(h, gpre_ref[...]).astype(a_ref.dtype)
    else:
        (h_out_ref,) = rest
    h_out_ref[...] = h


def residual_norm(h, m, g_post, g_pre_next=None):
    rows, d = h.shape
    tm = _tile(rows, 256)
    with_next = g_pre_next is not None
    row_spec = pl.BlockSpec((tm, d), lambda i: (i, 0))
    g_spec = pl.BlockSpec((1, d), lambda i: (0, 0))
    args = [h, m, g_post.reshape(1, d)]
    in_specs = [row_spec, row_spec, g_spec]
    out_shape = [jax.ShapeDtypeStruct((rows, d), F32)]
    out_specs = [row_spec]
    if with_next:
        args.append(g_pre_next.reshape(1, d))
        in_specs.append(g_spec)
        out_shape.append(jax.ShapeDtypeStruct((rows, d), BF16))
        out_specs.append(row_spec)
    out = pl.pallas_call(
        functools.partial(_residual_kernel, with_next=with_next),
        out_shape=out_shape,
        grid=(rows // tm,),
        in_specs=in_specs,
        out_specs=out_specs,
        compiler_params=_params(("parallel",)),
        name="residual_norm",
    )(*args)
    return (out[0], out[1]) if with_next else (out[0], None)


CAST_BLOCK_BYTES = 8 * 2**20


def _cast_kernel(w_ref, o_ref):
    rows, n = w_ref.shape
    n_pad = o_ref.shape[1]
    n_full = n // LANES * LANES
    o_ref[:, :n_full] = w_ref[:, :n_full].astype(o_ref.dtype)
    if n_pad > n_full:
        tail = [w_ref[:, n_full:].astype(o_ref.dtype)] if n > n_full else []
        o_ref[:, n_full:] = jnp.concatenate(tail + [jnp.zeros((rows, n_pad - n), o_ref.dtype)], axis=1)


def cast_weight(w, layer, n_pad=None):
    _, kdim, n = w.shape
    n_pad = n if n_pad is None else n_pad
    tk = _tile(kdim, max(8, CAST_BLOCK_BYTES // (4 * n)))
    return pl.pallas_call(
        _cast_kernel,
        out_shape=jax.ShapeDtypeStruct((kdim, n_pad), BF16),
        grid=(kdim // tk,),
        in_specs=[pl.BlockSpec((None, tk, n), lambda i: (layer, i, 0))],
        out_specs=pl.BlockSpec((tk, n_pad), lambda i: (i, 0)),
        compiler_params=_params(("parallel",)),
        name="cast_weight",
    )(w)


MM_CHUNK = 256


def _rope(x, cos2, sin2):
    return x * cos2 + pltpu.roll(x, HEAD_DIM // 2, 1) * sin2


def _matmul_kernel(a_ref, w_ref, *rest, nk, act, rope_pred, n_split):
    if rope_pred is not None:
        cos_ref, sin_ref, *rest = rest
        do_rope = rope_pred(pl.program_id(1))
        cos2 = jnp.where(do_rope, cos_ref[...], 1.0)
        sin2 = jnp.where(do_rope, sin_ref[...], 0.0)
    o_ref, *scratch = rest
    sub = o_ref.shape[1] // n_split
    if nk > 1:
        (acc_ref,) = scratch

        @pl.when(pl.program_id(2) == 0)
        def _():
            acc_ref[...] = jnp.zeros(acc_ref.shape, F32)

    for c in range(n_split):
        acc = jnp.dot(a_ref[...], w_ref[:, c * sub:(c + 1) * sub], preferred_element_type=F32)
        if nk > 1:
            acc = acc_ref[c] + acc
            acc_ref[c] = acc
        if act == "relu2":
            r = jnp.maximum(acc, 0.0)
            acc = r * r
        if rope_pred is None:
            o_ref[:, c * sub:(c + 1) * sub] = acc.astype(o_ref.dtype)
        else:
            for hh in range(sub // HEAD_DIM):
                x = acc[:, hh * HEAD_DIM:(hh + 1) * HEAD_DIM]
                col = c * sub + hh * HEAD_DIM
                o_ref[:, col:col + HEAD_DIM] = _rope(x, cos2, sin2).astype(o_ref.dtype)


def matmul(a, w, *, cols=None, out_dtype=BF16, act=None, rope=None, tm=1024, tn=1024, tk=4096):
    m, kdim = a.shape
    col0, n = (0, w.shape[1]) if cols is None else cols
    tm, tn, tk = _tile(m, tm), _tile(n, tn), _tile(kdim, tk)
    assert col0 % tn == 0
    j0 = col0 // tn
    nk = kdim // tk
    in_specs = [pl.BlockSpec((tm, tk), lambda i, j, k: (i, k)),
                pl.BlockSpec((tk, tn), lambda i, j, k: (k, j + j0))]
    args = [a, w]
    rope_pred = None
    if rope is not None:
        cos2, sin2, rope_pred = rope
        tab_spec = pl.BlockSpec((tm, HEAD_DIM), lambda i, j, k: (i, 0))
        in_specs += [tab_spec, tab_spec]
        args += [cos2, sin2]
    n_split = max(1, tn // MM_CHUNK)
    scratch = [pltpu.VMEM((n_split, tm, tn // n_split), F32)] if nk > 1 else []
    return pl.pallas_call(
        functools.partial(_matmul_kernel, nk=nk, act=act, rope_pred=rope_pred, n_split=n_split),
        out_shape=jax.ShapeDtypeStruct((m, n), out_dtype),
        grid=(m // tm, n // tn, nk),
        in_specs=in_specs,
        out_specs=pl.BlockSpec((tm, tn), lambda i, j, k: (i, j)),
        scratch_shapes=scratch,
        compiler_params=_params(("parallel", "parallel", "arbitrary")),
        name="matmul",
    )(*args)


A_PERIOD = 16


def _to_residue_major(x, batch, seq):
    u = seq // A_PERIOD
    return x.reshape(batch, u, A_PERIOD, -1).transpose(0, 2, 1, 3).reshape(batch * seq, -1)


def _from_residue_major(x, batch, seq):
    u = seq // A_PERIOD
    return x.reshape(batch, A_PERIOD, u, -1).transpose(0, 2, 1, 3).reshape(batch * seq, -1)


def _dilated_kernel(q_ref, kp_ref, kc_ref, vp_ref, vc_ref, o_ref, lse_ref, sp_ref, sc_ref,
                    *, w, rows, local_index, blk_axis):
    has_prev = pl.program_id(blk_axis) > 0
    qi = local_index(lax.broadcasted_iota(jnp.int32, (rows, rows), 0))
    kj = local_index(lax.broadcasted_iota(jnp.int32, (rows, rows), 1))
    mask_prev = jnp.logical_and(qi + rows - kj <= w, has_prev)
    mask_cur = jnp.logical_and(kj <= qi, qi - kj <= w)
    scale = HEAD_DIM ** -0.5 * LOG2E
    lead = (slice(None),) * (len(q_ref.shape) - 1)
    out_shape = o_ref.shape[:-1] + (HEAD_DIM,)

    def head(ref, h):
        return ref[lead + (slice(h * HEAD_DIM, (h + 1) * HEAD_DIM),)].reshape(rows, HEAD_DIM)

    def stage(h):
        q = head(q_ref, h)
        sp_ref[h % 2] = lax.dot_general(q, head(kp_ref, h), NT_DIMS, preferred_element_type=F32)
        sc_ref[h % 2] = lax.dot_general(q, head(kc_ref, h), NT_DIMS, preferred_element_type=F32)

    head_lane = lax.broadcasted_iota(jnp.int32, (rows, LANES), 1)
    lse_tile = jnp.zeros((rows, LANES), F32)

    stage(0)
    for h in range(A_HEADS_PER_GROUP):
        if h + 1 < A_HEADS_PER_GROUP:
            stage(h + 1)
        s_prev = jnp.where(mask_prev, sp_ref[h % 2] * scale, MASK_VALUE)
        s_cur = jnp.where(mask_cur, sc_ref[h % 2] * scale, MASK_VALUE)
        m = jnp.maximum(jnp.max(s_prev, axis=1, keepdims=True), jnp.max(s_cur, axis=1, keepdims=True))
        p_prev = jnp.exp2(s_prev - m)
        p_cur = jnp.exp2(s_cur - m)
        l = jnp.sum(p_prev, axis=1, keepdims=True) + jnp.sum(p_cur, axis=1, keepdims=True)
        o = jnp.dot(p_prev.astype(BF16), head(vp_ref, h), preferred_element_type=F32)
        o += jnp.dot(p_cur.astype(BF16), head(vc_ref, h), preferred_element_type=F32)
        idx = lead + (slice(h * HEAD_DIM, (h + 1) * HEAD_DIM),)
        o_ref[idx] = (o / l).astype(o_ref.dtype).reshape(out_shape)
        lse_tile = jnp.where(head_lane == h, m * (1.0 / LOG2E) + jnp.log(l), lse_tile)
    lse_ref[...] = lse_tile.reshape(lse_ref.shape)


def dilated_attention(qkv, batch, seq, group):
    window, r = DILATED_GROUPS[group]
    w = window // r
    u = seq // A_PERIOD
    width = qkv.shape[1]
    view = qkv.reshape(batch, 4, 4, u, width)
    if r == 16:
        rows, n_blk = A_BLOCK, u // A_BLOCK
        block = (None, None, None, rows, A_GROUP_WIDTH)
        grid = (batch, 4, 4, n_blk)
        place = lambda g, blk: (g[0], g[1], g[2], blk)
        local_index = lambda rho: rho
    elif r == 4:
        rows, n_blk = A_BLOCK, u // (A_BLOCK // 4)
        block = (None, 4, None, rows // 4, A_GROUP_WIDTH)
        grid = (batch, 4, n_blk)
        place = lambda g, blk: (g[0], 0, g[1], blk)
        local_index = lambda rho: 4 * (rho % (rows // 4)) + rho // (rows // 4)
    else:
        assert r == 1
        rows, n_blk = 2 * A_BLOCK, u // (2 * A_BLOCK // A_PERIOD)
        per = rows // A_PERIOD
        block = (None, 4, 4, per, A_GROUP_WIDTH)
        grid = (batch, n_blk)
        place = lambda g, blk: (g[0], 0, 0, blk)
        local_index = lambda rho: A_PERIOD * (rho % per) + rho // per
    assert w <= rows and n_blk * rows * r == seq

    def spec(which, prev):
        def index(*g):
            blk = jnp.maximum(g[-1] - 1, 0) if prev else g[-1]
            return place(g, blk) + (group * 3 + which,)
        return pl.BlockSpec(block, index)

    out_spec = pl.BlockSpec(block, lambda *g: place(g, g[-1]) + (0,))
    lse_spec = pl.BlockSpec(block[:-1] + (LANES,), lambda *g: place(g, g[-1]) + (0,))
    o, lse = pl.pallas_call(
        functools.partial(_dilated_kernel, w=w, rows=rows, local_index=local_index, blk_axis=len(grid) - 1),
        out_shape=[jax.ShapeDtypeStruct((batch, 4, 4, u, A_GROUP_WIDTH), BF16),
                   jax.ShapeDtypeStruct((batch, 4, 4, u, LANES), F32)],
        grid=grid,
        in_specs=[spec(0, False), spec(1, True), spec(1, False), spec(2, True), spec(2, False)],
        out_specs=[out_spec, lse_spec],
        scratch_shapes=[pltpu.VMEM((2, rows, rows), F32),
                        pltpu.VMEM((2, rows, rows), F32)],
        compiler_params=_params(("parallel",) * (len(grid) - 1) + ("arbitrary",)),
        name="dilated_attention",
    )(view, view, view, view, view)
    return o.reshape(batch * seq, A_GROUP_WIDTH), lse.reshape(batch * seq, LANES)


def _merge_kernel(o0_ref, o1_ref, o2_ref, l0_ref, l1_ref, l2_ref, out_ref):
    l0, l1, l2 = l0_ref[...], l1_ref[...], l2_ref[...]
    m = jnp.maximum(jnp.maximum(l0, l1), l2)
    e0, e1, e2 = jnp.exp(l0 - m), jnp.exp(l1 - m), jnp.exp(l2 - m)
    inv = 1.0 / (e0 + e1 + e2)
    weights = (e0 * inv, e1 * inv, e2 * inv)
    rows = out_ref.shape[0]
    for h in range(A_HEADS_PER_GROUP):
        sl = slice(h * HEAD_DIM, (h + 1) * HEAD_DIM)
        acc = jnp.zeros((rows, HEAD_DIM), F32)
        for wgt, o_ref in zip(weights, (o0_ref, o1_ref, o2_ref)):
            acc += jnp.broadcast_to(wgt[:, h:h + 1], (rows, HEAD_DIM)) * o_ref[:, sl].astype(F32)
        out_ref[:, sl] = acc.astype(out_ref.dtype)


def merge_groups(outs, lses):
    rows, width = outs[0].shape
    tm = _tile(rows, 512)
    spec = pl.BlockSpec((tm, width), lambda i: (i, 0))
    lse_spec = pl.BlockSpec((tm, LANES), lambda i: (i, 0))
    return pl.pallas_call(
        _merge_kernel,
        out_shape=jax.ShapeDtypeStruct((rows, width), BF16),
        grid=(rows // tm,),
        in_specs=[spec] * 3 + [lse_spec] * 3,
        out_specs=spec,
        compiler_params=_params(("parallel",)),
        name="merge_groups",
    )(*outs, *lses)


def _indexer_prep_kernel(lat_ref, gq_ref, lnw_ref, lnb_ref, cos_ref, sin_ref, iqn_ref, ik_ref, iw_ref):
    iqn_ref[...] = _rms(lat_ref[:, :IDX_Q_RANK], gq_ref[...]).astype(iqn_ref.dtype)
    k = lat_ref[:, IDX_Q_RANK:IDX_Q_RANK + IDX_DIM]
    kc = k - jnp.mean(k, axis=-1, keepdims=True)
    kn = kc * lax.rsqrt(jnp.mean(kc * kc, axis=-1, keepdims=True) + NORM_EPS) * lnw_ref[...] + lnb_ref[...]
    ik_ref[...] = _rope(kn, cos_ref[...], sin_ref[...]).astype(ik_ref.dtype)
    iw_ref[...] = lat_ref[:, IDX_Q_RANK + IDX_DIM:] * (IDX_HEADS ** -0.5 * IDX_DIM ** -0.5)


def indexer_prep(lat, gq, lnw, lnb, cos2, sin2):
    rows, width = lat.shape
    tm = _tile(rows, 512)
    row = lambda wd: pl.BlockSpec((tm, wd), lambda i: (i, 0))
    vec = lambda wd: pl.BlockSpec((1, wd), lambda i: (0, 0))
    return pl.pallas_call(
        _indexer_prep_kernel,
        out_shape=[jax.ShapeDtypeStruct((rows, IDX_Q_RANK), BF16),
                   jax.ShapeDtypeStruct((rows, IDX_DIM), BF16),
                   jax.ShapeDtypeStruct((rows, LANES), F32)],
        grid=(rows // tm,),
        in_specs=[row(width), vec(IDX_Q_RANK), vec(IDX_DIM), vec(IDX_DIM), row(HEAD_DIM), row(HEAD_DIM)],
        out_specs=[row(IDX_Q_RANK), row(IDX_DIM), row(LANES)],
        compiler_params=_params(("parallel",)),
        name="indexer_prep",
    )(lat, gq.reshape(1, -1), lnw.reshape(1, -1), lnb.reshape(1, -1), cos2, sin2)


def _ordered_bits(x):
    bits = lax.bitcast_convert_type(x, jnp.int32)
    key = jnp.where(bits >= 0, bits, bits ^ jnp.int32(0x7FFFFFFF))
    return jnp.where(key == -1, 0, key)


def _score_of(key):
    return lax.bitcast_convert_type(jnp.where(key >= 0, key, key ^ jnp.int32(0x7FFFFFFF)), F32)


IDX_TQ = 128
IDX_TS = 512
IDX_HEAD_BATCH = 8
IDX_SCORE_STEPS = 24


def _indexer_kernel(iq_ref, ik_ref, iw_ref, bias_ref, key_ref, qs_ref, wb_ref, *, n_sel):
    tq, seq = key_ref.shape
    i = pl.program_id(1)
    n_chunks = ((i + 1) * tq + IDX_TS - 1) // IDX_TS
    key_ref[...] = jnp.full((tq, seq), INT_MIN, jnp.int32)
    iw = iw_ref[...]
    for h in range(IDX_HEADS):
        qs_ref[h * tq:(h + 1) * tq, :] = iq_ref[:, h * IDX_DIM:(h + 1) * IDX_DIM]
        wb_ref[h] = jnp.broadcast_to(iw[:, h:h + 1], (tq, LANES))
    t_pos = i * tq + lax.broadcasted_iota(jnp.int32, (tq, IDX_TS), 0)
    lane = lax.broadcasted_iota(jnp.int32, (tq, IDX_TS), 1)
    hb_rows = IDX_HEAD_BATCH * tq

    def score_chunk(c, carry):
        start = pl.multiple_of(c * IDX_TS, IDX_TS)
        keys = ik_ref[pl.ds(start, IDX_TS), :]
        acc = jnp.zeros((tq, IDX_TS), F32)
        for hb in range(IDX_HEADS // IDX_HEAD_BATCH):
            rel = lax.dot_general(qs_ref[hb * hb_rows:(hb + 1) * hb_rows, :], keys, NT_DIMS,
                                  preferred_element_type=F32)
            for hh in range(IDX_HEAD_BATCH):
                wb = _lanes(wb_ref[hb * IDX_HEAD_BATCH + hh], IDX_TS)
                acc += jnp.maximum(rel[hh * tq:(hh + 1) * tq], 0.0) * wb
        causal = start + lane <= t_pos
        key_ref[:, pl.ds(start, IDX_TS)] = jnp.where(causal, _ordered_bits(acc), INT_MIN)
        hi_part = jnp.where(causal, acc, -jnp.inf)
        lo_part = jnp.where(causal, acc, jnp.inf)
        smax, smin = carry
        for b in range(IDX_TS // LANES):
            smax = jnp.maximum(smax, hi_part[:, b * LANES:(b + 1) * LANES])
            smin = jnp.minimum(smin, lo_part[:, b * LANES:(b + 1) * LANES])
        return smax, smin

    smax, smin = lax.fori_loop(0, n_chunks, score_chunk,
                               (jnp.full((tq, LANES), -jnp.inf, F32), jnp.full((tq, LANES), jnp.inf, F32)))
    smax = jnp.max(smax, axis=1, keepdims=True)
    smin = jnp.min(smin, axis=1, keepdims=True)

    def count_where(pred):
        def body(c, cnt):
            start = pl.multiple_of(c * IDX_TS, IDX_TS)
            hit = jnp.where(pred(key_ref[:, pl.ds(start, IDX_TS)], start + lane), 1.0, 0.0)
            for b in range(IDX_TS // LANES):
                cnt = cnt + hit[:, b * LANES:(b + 1) * LANES]
            return cnt

        cnt = lax.fori_loop(0, n_chunks, body, jnp.zeros((tq, LANES), F32))
        return jnp.sum(cnt, axis=1, keepdims=True)

    def count_ge(cand):
        cand_w = _lanes(cand, IDX_TS)
        return count_where(lambda keys, pos: keys >= cand_w)

    ones_i = jnp.ones((tq, LANES), jnp.int32)
    n_causal = i * tq + 1 + lax.broadcasted_iota(jnp.int32, (tq, LANES), 0)
    lo0 = _ordered_bits(smin) * ones_i
    hi0 = _ordered_bits(smax) * ones_i + 1
    open0 = jnp.where(n_causal > n_sel, 1.0, 0.0)
    thr0 = jnp.full((tq, LANES), INT_MIN + 1, jnp.int32)

    def search_cond(state):
        step, _, n_open = state
        return jnp.logical_and(step < IDX_SCORE_STEPS + 33, n_open > 0.0)

    def search_step(state):
        step, (lo, hi, n_lo, thr, n_thr, still_open), _ = state
        score_mid = _ordered_bits(0.5 * _score_of(lo) + 0.5 * _score_of(hi))
        key_mid = (lo >> 1) + (hi >> 1) + (lo & hi & 1)
        cand = jnp.where(step < IDX_SCORE_STEPS, score_mid, key_mid)
        cand = jnp.maximum(jnp.minimum(cand, hi - 1), lo + 1)
        cnt = count_ge(cand)
        enough = cnt >= n_sel
        lo = jnp.where(enough, cand, lo)
        n_lo = jnp.where(enough, cnt, n_lo)
        hi = jnp.where(enough, hi, cand)
        settled = jnp.logical_and(still_open > 0.0, jnp.logical_or(cnt == n_sel, hi <= lo + 1))
        thr = jnp.where(settled, lo, thr)
        n_thr = jnp.where(settled, n_lo, n_thr)
        still_open = jnp.where(settled, 0.0, still_open)
        return step + 1, (lo, hi, n_lo, thr, n_thr, still_open), jnp.max(still_open)

    n_causal_f = n_causal.astype(F32)
    init = (lo0, hi0, n_causal_f, thr0, jnp.minimum(n_causal_f, n_sel), open0)
    _, (_, _, _, thr, n_thr, _), _ = lax.while_loop(search_cond, search_step,
                                                    (jnp.int32(0), init, jnp.max(open0)))
    thr_w = _lanes(thr, IDX_TS)

    tied = n_thr > n_sel
    any_tied = jnp.max(jnp.where(tied, 1.0, 0.0)) > 0.0

    @pl.when(jnp.logical_not(any_tied))
    def _():
        for c in range(seq // IDX_TS):
            sl = slice(c * IDX_TS, (c + 1) * IDX_TS)
            bias_ref[:, sl] = jnp.where(key_ref[:, sl] >= thr_w, 0.0, MASK_VALUE).astype(bias_ref.dtype)

    @pl.when(any_tied)
    def _():
        need = n_sel - count_ge(thr + 1)

        def halve(_, bracket):
            below, upto = bracket
            mid = (below + upto) >> 1
            mid_w = _lanes(mid, IDX_TS)
            cnt = count_where(lambda keys, pos: jnp.logical_and(keys == thr_w, pos <= mid_w))
            ok = cnt >= need
            return jnp.where(ok, below, mid), jnp.where(ok, mid, upto)

        _, upto = lax.fori_loop(0, seq.bit_length(), halve,
                                (jnp.full((tq, LANES), -1, jnp.int32), jnp.full((tq, LANES), seq - 1, jnp.int32)))
        last_tie_w = _lanes(jnp.where(tied, upto, INT_MAX), IDX_TS)
        for c in range(seq // IDX_TS):
            sl = slice(c * IDX_TS, (c + 1) * IDX_TS)
            keys = key_ref[:, sl]
            at_thr = jnp.logical_and(keys == thr_w, c * IDX_TS + lane <= last_tie_w)
            bias_ref[:, sl] = jnp.where(jnp.logical_or(keys > thr_w, at_thr), 0.0, MASK_VALUE).astype(bias_ref.dtype)


def indexer_mask(iq, ik, iw, batch, seq):
    n_sel = min(TOPK_MAX, seq // 4)
    tq = IDX_TQ
    assert seq % IDX_TS == 0 and seq % tq == 0
    return pl.pallas_call(
        functools.partial(_indexer_kernel, n_sel=n_sel),
        out_shape=jax.ShapeDtypeStruct((batch, seq, seq), BF16),
        grid=(batch, seq // tq),
        in_specs=[pl.BlockSpec((None, tq, IDX_HEADS * IDX_DIM), lambda b, i: (b, i, 0)),
                  pl.BlockSpec((None, seq, IDX_DIM), lambda b, i: (b, 0, 0)),
                  pl.BlockSpec((None, tq, LANES), lambda b, i: (b, i, 0))],
        out_specs=pl.BlockSpec((None, tq, seq), lambda b, i: (b, i, 0)),
        scratch_shapes=[pltpu.VMEM((tq, seq), jnp.int32),
                        pltpu.VMEM((IDX_HEADS * tq, IDX_DIM), BF16),
                        pltpu.VMEM((IDX_HEADS, tq, LANES), F32)],
        compiler_params=_params(("parallel", "arbitrary")),
        name="indexer_mask",
    )(iq.reshape(batch, seq, -1), ik.reshape(batch, seq, -1), iw.reshape(batch, seq, -1))


B_TQ = 256
B_TS = 512
B_REP = B_HEADS // B_KV_HEADS
B_CHUNK = 128
B_UNIT = 512


def _sparse_attn_kernel(qblk_ref, kblk_ref, q_ref, k_ref, v_ref, bias_ref, o_ref,
                        qs_ref, vo_ref, biasf_ref, s_ref, p_ref, alpha_ref, m_ref, accl_ref):
    tq, ts = bias_ref.shape
    rows = B_REP * tq
    step = pl.program_id(1)
    i, j = qblk_ref[step], kblk_ref[step]
    last = ((i + 1) * tq - 1) // ts

    @pl.when(j == 0)
    def _():
        scale = HEAD_DIM ** -0.5 * LOG2E
        for g in range(B_KV_HEADS):
            for r in range(B_REP):
                h = g * B_REP + r
                qh = q_ref[:, h * HEAD_DIM:(h + 1) * HEAD_DIM].astype(F32) * scale
                qs_ref[g, r * tq:(r + 1) * tq, :] = qh.astype(qs_ref.dtype)
        m_ref[...] = jnp.full(m_ref.shape, MASK_VALUE, F32)
        accl_ref[...] = jnp.zeros(accl_ref.shape, F32)

    biasf_ref[...] = bias_ref[...].astype(F32)
    ones = jnp.ones((ts, LANES), BF16)
    for g in range(B_KV_HEADS):
        vo_ref[g, :, :HEAD_DIM] = v_ref[:, g * HEAD_DIM:(g + 1) * HEAD_DIM]
        vo_ref[g, :, HEAD_DIM:] = ones

    unit_rows = min(B_UNIT, rows)
    units = [(g, r0) for g in range(B_KV_HEADS) for r0 in range(0, rows, unit_rows)]

    def logits(u):
        g, r0 = units[u]
        kg = k_ref[:, g * HEAD_DIM:(g + 1) * HEAD_DIM]
        s_ref[u % 2] = lax.dot_general(qs_ref[g, r0:r0 + unit_rows, :], kg, NT_DIMS,
                                       preferred_element_type=F32)

    logits(0)
    for u, (g, r0) in enumerate(units):
        if u + 1 < len(units):
            logits(u + 1)
        par = u % 2
        for c in range(0, unit_rows, B_CHUNK):
            rs = slice(c, c + B_CHUNK)
            gs = slice(r0 + c, r0 + c + B_CHUNK)
            qrow = (r0 + c) % tq
            s = s_ref[par, rs, :] + biasf_ref[qrow:qrow + B_CHUNK, :]
            m_prev = m_ref[g, gs, :]
            m_new = jnp.maximum(m_prev, jnp.max(s, axis=1, keepdims=True))
            p_ref[par, rs, :] = jnp.exp2(s - _lanes(m_new, ts)).astype(BF16)
            alpha_ref[par, rs, :] = jnp.exp2(m_prev - m_new)
            m_ref[g, gs, :] = m_new
        pv = jnp.dot(p_ref[par], vo_ref[g], preferred_element_type=F32)
        alpha = alpha_ref[par]
        us = slice(r0, r0 + unit_rows)
        accl_ref[g, us, :] = jnp.concatenate([alpha, alpha], axis=1) * accl_ref[g, us, :] + pv

    @pl.when(j == last)
    def _():
        for g in range(B_KV_HEADS):
            out = accl_ref[g, :, :HEAD_DIM] / accl_ref[g, :, HEAD_DIM:]
            for r in range(B_REP):
                h = g * B_REP + r
                o_ref[:, h * HEAD_DIM:(h + 1) * HEAD_DIM] = out[r * tq:(r + 1) * tq].astype(o_ref.dtype)


def sparse_attention(qkv, bias, batch, seq):
    tq, ts = _tile(seq, B_TQ), _tile(seq, B_TS)
    view = qkv.reshape(batch, seq, B_QKV_WIDTH)
    k_tile = B_Q_WIDTH // B_KV_WIDTH
    pairs = [(i, j) for i in range(seq // tq) for j in range(((i + 1) * tq - 1) // ts + 1)]
    qblk = jnp.asarray([p[0] for p in pairs], jnp.int32)
    kblk = jnp.asarray([p[1] for p in pairs], jnp.int32)
    rows = B_REP * tq
    unit_rows = min(B_UNIT, rows)
    grid_spec = pltpu.PrefetchScalarGridSpec(
        num_scalar_prefetch=2,
        grid=(batch, len(pairs)),
        in_specs=[pl.BlockSpec((None, tq, B_Q_WIDTH), lambda b, s, qb, kb: (b, qb[s], 0)),
                  pl.BlockSpec((None, ts, B_KV_WIDTH), lambda b, s, qb, kb: (b, kb[s], k_tile)),
                  pl.BlockSpec((None, ts, B_KV_WIDTH), lambda b, s, qb, kb: (b, kb[s], k_tile + 1)),
                  pl.BlockSpec((None, tq, ts), lambda b, s, qb, kb: (b, qb[s], kb[s]))],
        out_specs=pl.BlockSpec((None, tq, B_Q_WIDTH), lambda b, s, qb, kb: (b, qb[s], 0)),
        scratch_shapes=[pltpu.VMEM((B_KV_HEADS, rows, HEAD_DIM), BF16),
                        pltpu.VMEM((B_KV_HEADS, ts, 2 * HEAD_DIM), BF16),
                        pltpu.VMEM((tq, ts), F32),
                        pltpu.VMEM((2, unit_rows, ts), F32),
                        pltpu.VMEM((2, unit_rows, ts), BF16),
                        pltpu.VMEM((2, unit_rows, LANES), F32),
                        pltpu.VMEM((B_KV_HEADS, rows, LANES), F32),
                        pltpu.VMEM((B_KV_HEADS, rows, 2 * HEAD_DIM), F32)])
    out = pl.pallas_call(
        _sparse_attn_kernel,
        out_shape=jax.ShapeDtypeStruct((batch, seq, B_Q_WIDTH), BF16),
        grid_spec=grid_spec,
        compiler_params=_params(("parallel", "arbitrary")),
        name="sparse_attention",
    )(qblk, kblk, view, view, view, bias)
    return out.reshape(batch * seq, B_Q_WIDTH)


def _rope_tables(positions):
    inv_freq = ROPE_THETA ** (-jnp.arange(0, HEAD_DIM, 2, dtype=F32) / HEAD_DIM)
    ang = positions.astype(F32).reshape(-1, 1) * inv_freq
    cos, sin = jnp.cos(ang), jnp.sin(ang)
    return jnp.concatenate([cos, cos], axis=-1), jnp.concatenate([-sin, sin], axis=-1)


def _mlp(f_in, w_up, w_down, layer):
    u = matmul(f_in, cast_weight(w_up, layer), act="relu2")
    return matmul(u, cast_weight(w_down, layer))


def _mixer_dilated(a, cos2, sin2, w_in, w_out, layer, batch, seq):
    tn = 1024
    per_part = A_GROUP_WIDTH // tn
    rope_pred = lambda j: (j // per_part) % 3 != 2
    a, cos2, sin2 = (_to_residue_major(t, batch, seq) for t in (a, cos2, sin2))
    qkv = matmul(a, cast_weight(w_in, layer), rope=(cos2, sin2, rope_pred), tn=tn)
    outs, lses = zip(*[dilated_attention(qkv, batch, seq, g) for g in range(N_GROUPS)])
    o = _from_residue_major(merge_groups(outs, lses), batch, seq)
    return matmul(o, cast_weight(w_out, layer))


def _mixer_sparse(a, cos2, sin2, w_in, idx_q_norm, w_idx_up, idx_k_w, idx_k_b, w_out, layer, batch, seq):
    tn = 1024
    n_rope = (B_Q_WIDTH + B_KV_WIDTH) // tn
    idx_width = IDX_Q_RANK + IDX_DIM + LANES
    w_all = cast_weight(w_in, layer, n_pad=B_QKV_WIDTH + idx_width)
    qkv = matmul(a, w_all, cols=(0, B_QKV_WIDTH), rope=(cos2, sin2, lambda j: j < n_rope), tn=tn)
    lat = matmul(a, w_all, cols=(B_QKV_WIDTH, idx_width), out_dtype=F32, tn=256)
    iq_n, ik, iw = indexer_prep(lat, idx_q_norm[layer], idx_k_w[layer], idx_k_b[layer], cos2, sin2)
    iq = matmul(iq_n, cast_weight(w_idx_up, layer), rope=(cos2, sin2, lambda j: j >= 0), tn=tn)
    bias = indexer_mask(iq, ik, iw, batch, seq)
    o = sparse_attention(qkv, bias, batch, seq)
    return matmul(o, cast_weight(w_out, layer))


def kernel(x, positions, attn_pre_norm, attn_post_norm, mlp_pre_norm, mlp_post_norm, w_in_a, w_out_a,
           w_in_b, idx_q_norm, w_idx_up, idx_k_norm_w, idx_k_norm_b, w_out_b, w_up, w_down):
    batch, seq, d_model = x.shape
    depth = attn_pre_norm.shape[0]
    cos2, sin2 = _rope_tables(positions)
    h = x.reshape(batch * seq, d_model)
    a = rms_norm_cast(h, attn_pre_norm[0])
    for i in range(depth):
        j = i // 2
        if i % 2 == 0:
            m = _mixer_dilated(a, cos2, sin2, w_in_a, w_out_a, j, batch, seq)
        else:
            m = _mixer_sparse(a, cos2, sin2, w_in_b, idx_q_norm, w_idx_up,
                              idx_k_norm_w, idx_k_norm_b, w_out_b, j, batch, seq)
        h, f_in = residual_norm(h, m, attn_post_norm[i], mlp_pre_norm[i])
        f = _mlp(f_in, w_up, w_down, i)
        h, a = residual_norm(h, f, mlp_post_norm[i], attn_pre_norm[i + 1] if i + 1 < depth else None)
    return h.reshape(batch, seq, d_model)
```

```python
import functools
import math

import jax
import jax.numpy as jnp
from jax import lax
from jax.experimental import pallas as pl
from jax.experimental.pallas import tpu as pltpu

HEAD_DIM = 128
ROPE_THETA = 10000.0
NORM_EPS = 1e-6
DILATED_GROUPS = ((128, 1), (512, 4), (2048, 16))
N_GROUPS = 3
A_HEADS_PER_GROUP = 16
A_BLOCK = 128
A_GROUP_WIDTH = A_HEADS_PER_GROUP * HEAD_DIM
B_HEADS = 32
B_KV_HEADS = 8
IDX_HEADS = 32
IDX_DIM = 128
IDX_Q_RANK = 1024
TOPK_MAX = 256
B_Q_WIDTH = B_HEADS * HEAD_DIM
B_KV_WIDTH = B_KV_HEADS * HEAD_DIM
B_QKV_WIDTH = B_Q_WIDTH + 2 * B_KV_WIDTH

LANES = 128
VMEM_LIMIT_BYTES = 56 * 2**20
MASK_VALUE = -1e30
INT_MIN = -2**31
INT_MAX = 2**31 - 1
KEY_BITS = 32
LOG2E = math.log2(math.e)
NT_DIMS = (((1,), (1,)), ((), ()))

F32 = jnp.float32
BF16 = jnp.bfloat16


def _tile(dim, pref):
    t = min(dim, pref)
    if dim % t:
        t = 1 << (t.bit_length() - 1)
    while dim % t:
        t //= 2
    return t


def _params(semantics):
    return pltpu.CompilerParams(dimension_semantics=semantics, vmem_limit_bytes=VMEM_LIMIT_BYTES)


def _lanes(x, width):
    return jnp.concatenate([x] * (width // LANES), axis=1)


NORM_ROWS = 256


def _rms(x, g):
    return x * lax.rsqrt(jnp.mean(x * x, axis=-1, keepdims=True) + NORM_EPS) * g


def _norm_kernel(x_ref, g_ref, a_ref):
    a_ref[...] = _rms(x_ref[...], g_ref[...]).astype(a_ref.dtype)


def rms_norm_cast(x, g):
    m, d = x.shape
    tm = _tile(m, NORM_ROWS)
    return pl.pallas_call(
        _norm_kernel,
        out_shape=jax.ShapeDtypeStruct((m, d), BF16),
        grid=(m // tm,),
        in_specs=[pl.BlockSpec((tm, d), lambda i: (i, 0)), pl.BlockSpec((1, d), lambda i: (0, 0))],
        out_specs=pl.BlockSpec((tm, d), lambda i: (i, 0)),
        compiler_params=_params(("parallel",)),
        name="rms_norm_cast",
    )(x, g.reshape(1, d))


def _residual_kernel(h_ref, m_ref, gpost_ref, *rest, with_next):
    h = h_ref[...] + _rms(m_ref[...].astype(F32), gpost_ref[...])
    if with_next:
        gpre_ref, h_out_ref, a_ref = rest
        a_ref[...] = _rms(h, gpre_ref[...]).astype(a_ref.dtype)
    else:
        (h_out_ref,) = rest
    h_out_ref[...] = h


def residual_norm(h, m, g_post, g_pre_next=None):
    rows, d = h.shape
    tm = _tile(rows, NORM_ROWS)
    with_next = g_pre_next is not None
    row_spec = pl.BlockSpec((tm, d), lambda i: (i, 0))
    g_spec = pl.BlockSpec((1, d), lambda i: (0, 0))
    args = [h, m, g_post.reshape(1, d)]
    in_specs = [row_spec, row_spec, g_spec]
    out_shape = [jax.ShapeDtypeStruct((rows, d), F32)]
    out_specs = [row_spec]
    if with_next:
        args.append(g_pre_next.reshape(1, d))
        in_specs.append(g_spec)
        out_shape.append(jax.ShapeDtypeStruct((rows, d), BF16))
        out_specs.append(row_spec)
    out = pl.pallas_call(
        functools.partial(_residual_kernel, with_next=with_next),
        out_shape=out_shape,
        grid=(rows // tm,),
        in_specs=in_specs,
        out_specs=out_specs,
        compiler_params=_params(("parallel",)),
        name="residual_norm",
    )(*args)
    return (out[0], out[1]) if with_next else (out[0], None)


CAST_BLOCK_BYTES = 8 * 2**20


def _cast_kernel(w_ref, o_ref):
    rows, n = w_ref.shape
    n_pad = o_ref.shape[1]
    n_full = n // LANES * LANES
    o_ref[:, :n_full] = w_ref[:, :n_full].astype(o_ref.dtype)
    if n_pad > n_full:
        tail = [w_ref[:, n_full:].astype(o_ref.dtype)] if n > n_full else []
        o_ref[:, n_full:] = jnp.concatenate(tail + [jnp.zeros((rows, n_pad - n), o_ref.dtype)], axis=1)


def cast_weight(w, layer, n_pad=None):
    _, kdim, n = w.shape
    n_pad = n if n_pad is None else n_pad
    tk = _tile(kdim, max(8, CAST_BLOCK_BYTES // (4 * n)))
    return pl.pallas_call(
        _cast_kernel,
        out_shape=jax.ShapeDtypeStruct((kdim, n_pad), BF16),
        grid=(kdim // tk,),
        in_specs=[pl.BlockSpec((None, tk, n), lambda i: (layer, i, 0))],
        out_specs=pl.BlockSpec((tk, n_pad), lambda i: (i, 0)),
        compiler_params=_params(("parallel",)),
        name="cast_weight",
    )(w)


MM_CHUNK = 256


def _rope(x, cos2, sin2):
    return x * cos2 + pltpu.roll(x, HEAD_DIM // 2, 1) * sin2


def _matmul_kernel(a_ref, w_ref, *rest, nk, act, rope_pred, n_split):
    if rope_pred is not None:
        cos_ref, sin_ref, *rest = rest
        do_rope = rope_pred(pl.program_id(1))
        cos2 = jnp.where(do_rope, cos_ref[...], 1.0)
        sin2 = jnp.where(do_rope, sin_ref[...], 0.0)
    o_ref, *scratch = rest
    sub = o_ref.shape[1] // n_split
    if nk > 1:
        (acc_ref,) = scratch

        @pl.when(pl.program_id(2) == 0)
        def _():
            acc_ref[...] = jnp.zeros(acc_ref.shape, F32)

    for c in range(n_split):
        acc = jnp.dot(a_ref[...], w_ref[:, c * sub:(c + 1) * sub], preferred_element_type=F32)
        if nk > 1:
            acc = acc_ref[c] + acc
            acc_ref[c] = acc
        if act == "relu2":
            r = jnp.maximum(acc, 0.0)
            acc = r * r
        if rope_pred is None:
            o_ref[:, c * sub:(c + 1) * sub] = acc.astype(o_ref.dtype)
        else:
            for hh in range(sub // HEAD_DIM):
                x = acc[:, hh * HEAD_DIM:(hh + 1) * HEAD_DIM]
                col = c * sub + hh * HEAD_DIM
                o_ref[:, col:col + HEAD_DIM] = _rope(x, cos2, sin2).astype(o_ref.dtype)


def matmul(a, w, *, cols=None, out_dtype=BF16, act=None, rope=None, tm=1024, tn=1024, tk=4096):
    m, kdim = a.shape
    col0, n = (0, w.shape[1]) if cols is None else cols
    tm, tn, tk = _tile(m, tm), _tile(n, tn), _tile(kdim, tk)
    assert col0 % tn == 0
    j0 = col0 // tn
    nk = kdim // tk
    in_specs = [pl.BlockSpec((tm, tk), lambda i, j, k: (i, k)),
                pl.BlockSpec((tk, tn), lambda i, j, k: (k, j + j0))]
    args = [a, w]
    rope_pred = None
    if rope is not None:
        cos2, sin2, rope_pred = rope
        tab_spec = pl.BlockSpec((tm, HEAD_DIM), lambda i, j, k: (i, 0))
        in_specs += [tab_spec, tab_spec]
        args += [cos2, sin2]
    n_split = max(1, tn // MM_CHUNK)
    scratch = [pltpu.VMEM((n_split, tm, tn // n_split), F32)] if nk > 1 else []
    return pl.pallas_call(
        functools.partial(_matmul_kernel, nk=nk, act=act, rope_pred=rope_pred, n_split=n_split),
        out_shape=jax.ShapeDtypeStruct((m, n), out_dtype),
        grid=(m // tm, n // tn, nk),
        in_specs=in_specs,
        out_specs=pl.BlockSpec((tm, tn), lambda i, j, k: (i, j)),
        scratch_shapes=scratch,
        compiler_params=_params(("parallel", "parallel", "arbitrary")),
        name="matmul",
    )(*args)


A_PERIOD = 16


def _to_residue_major(x, batch, seq):
    u = seq // A_PERIOD
    return x.reshape(batch, u, A_PERIOD, -1).transpose(0, 2, 1, 3).reshape(batch * seq, -1)


def _from_residue_major(x, batch, seq):
    u = seq // A_PERIOD
    return x.reshape(batch, A_PERIOD, u, -1).transpose(0, 2, 1, 3).reshape(batch * seq, -1)


def _dilated_kernel(q_ref, kp_ref, kc_ref, vp_ref, vc_ref, o_ref, lse_ref, sp_ref, sc_ref,
                    *, w, rows, local_index, blk_axis):
    has_prev = pl.program_id(blk_axis) > 0
    qi = local_index(lax.broadcasted_iota(jnp.int32, (rows, rows), 0))
    kj = local_index(lax.broadcasted_iota(jnp.int32, (rows, rows), 1))
    mask_prev = jnp.logical_and(qi + rows - kj <= w, has_prev)
    mask_cur = jnp.logical_and(kj <= qi, qi - kj <= w)
    scale = HEAD_DIM ** -0.5 * LOG2E
    lead = (slice(None),) * (len(q_ref.shape) - 1)
    out_shape = o_ref.shape[:-1] + (HEAD_DIM,)

    def head(ref, h):
        return ref[lead + (slice(h * HEAD_DIM, (h + 1) * HEAD_DIM),)].reshape(rows, HEAD_DIM)

    def stage(h):
        q = head(q_ref, h)
        sp_ref[h % 2] = lax.dot_general(q, head(kp_ref, h), NT_DIMS, preferred_element_type=F32)
        sc_ref[h % 2] = lax.dot_general(q, head(kc_ref, h), NT_DIMS, preferred_element_type=F32)

    head_lane = lax.broadcasted_iota(jnp.int32, (rows, LANES), 1)
    lse_tile = jnp.zeros((rows, LANES), F32)

    stage(0)
    for h in range(A_HEADS_PER_GROUP):
        if h + 1 < A_HEADS_PER_GROUP:
            stage(h + 1)
        s_prev = jnp.where(mask_prev, sp_ref[h % 2] * scale, MASK_VALUE)
        s_cur = jnp.where(mask_cur, sc_ref[h % 2] * scale, MASK_VALUE)
        m = jnp.maximum(jnp.max(s_prev, axis=1, keepdims=True), jnp.max(s_cur, axis=1, keepdims=True))
        p_prev = jnp.exp2(s_prev - m)
        p_cur = jnp.exp2(s_cur - m)
        l = jnp.sum(p_prev, axis=1, keepdims=True) + jnp.sum(p_cur, axis=1, keepdims=True)
        o = jnp.dot(p_prev.astype(BF16), head(vp_ref, h), preferred_element_type=F32)
        o += jnp.dot(p_cur.astype(BF16), head(vc_ref, h), preferred_element_type=F32)
        idx = lead + (slice(h * HEAD_DIM, (h + 1) * HEAD_DIM),)
        o_ref[idx] = (o / l).astype(o_ref.dtype).reshape(out_shape)
        lse_tile = jnp.where(head_lane == h, m * (1.0 / LOG2E) + jnp.log(l), lse_tile)
    lse_ref[...] = lse_tile.reshape(lse_ref.shape)


def dilated_attention(qkv, batch, seq, group):
    window, r = DILATED_GROUPS[group]
    w = window // r
    u = seq // A_PERIOD
    width = qkv.shape[1]
    view = qkv.reshape(batch, 4, 4, u, width)
    if r == 16:
        rows, n_blk = A_BLOCK, u // A_BLOCK
        block = (None, None, None, rows, A_GROUP_WIDTH)
        grid = (batch, 4, 4, n_blk)
        place = lambda g, blk: (g[0], g[1], g[2], blk)
        local_index = lambda rho: rho
    elif r == 4:
        rows, n_blk = A_BLOCK, u // (A_BLOCK // 4)
        block = (None, 4, None, rows // 4, A_GROUP_WIDTH)
        grid = (batch, 4, n_blk)
        place = lambda g, blk: (g[0], 0, g[1], blk)
        local_index = lambda rho: 4 * (rho % (rows // 4)) + rho // (rows // 4)
    else:
        assert r == 1
        rows, n_blk = 2 * A_BLOCK, u // (2 * A_BLOCK // A_PERIOD)
        per = rows // A_PERIOD
        block = (None, 4, 4, per, A_GROUP_WIDTH)
        grid = (batch, n_blk)
        place = lambda g, blk: (g[0], 0, 0, blk)
        local_index = lambda rho: A_PERIOD * (rho % per) + rho // per
    assert w <= rows and n_blk * rows * r == seq

    def spec(which, prev):
        def index(*g):
            blk = jnp.maximum(g[-1] - 1, 0) if prev else g[-1]
            return place(g, blk) + (group * 3 + which,)
        return pl.BlockSpec(block, index)

    out_spec = pl.BlockSpec(block, lambda *g: place(g, g[-1]) + (0,))
    lse_spec = pl.BlockSpec(block[:-1] + (LANES,), lambda *g: place(g, g[-1]) + (0,))
    o, lse = pl.pallas_call(
        functools.partial(_dilated_kernel, w=w, rows=rows, local_index=local_index, blk_axis=len(grid) - 1),
        out_shape=[jax.ShapeDtypeStruct((batch, 4, 4, u, A_GROUP_WIDTH), BF16),
                   jax.ShapeDtypeStruct((batch, 4, 4, u, LANES), F32)],
        grid=grid,
        in_specs=[spec(0, False), spec(1, True), spec(1, False), spec(2, True), spec(2, False)],
        out_specs=[out_spec, lse_spec],
        scratch_shapes=[pltpu.VMEM((2, rows, rows), F32),
                        pltpu.VMEM((2, rows, rows), F32)],
        compiler_params=_params(("parallel",) * (len(grid) - 1) + ("arbitrary",)),
        name="dilated_attention",
    )(view, view, view, view, view)
    return o.reshape(batch * seq, A_GROUP_WIDTH), lse.reshape(batch * seq, LANES)


def _merge_kernel(o0_ref, o1_ref, o2_ref, l0_ref, l1_ref, l2_ref, out_ref):
    l0, l1, l2 = l0_ref[...], l1_ref[...], l2_ref[...]
    m = jnp.maximum(jnp.maximum(l0, l1), l2)
    e0, e1, e2 = jnp.exp(l0 - m), jnp.exp(l1 - m), jnp.exp(l2 - m)
    inv = 1.0 / (e0 + e1 + e2)
    weights = (e0 * inv, e1 * inv, e2 * inv)
    rows = out_ref.shape[0]
    for h in range(A_HEADS_PER_GROUP):
        sl = slice(h * HEAD_DIM, (h + 1) * HEAD_DIM)
        acc = jnp.zeros((rows, HEAD_DIM), F32)
        for wgt, o_ref in zip(weights, (o0_ref, o1_ref, o2_ref)):
            acc += jnp.broadcast_to(wgt[:, h:h + 1], (rows, HEAD_DIM)) * o_ref[:, sl].astype(F32)
        out_ref[:, sl] = acc.astype(out_ref.dtype)


def merge_groups(outs, lses):
    rows, width = outs[0].shape
    tm = _tile(rows, 512)
    spec = pl.BlockSpec((tm, width), lambda i: (i, 0))
    lse_spec = pl.BlockSpec((tm, LANES), lambda i: (i, 0))
    return pl.pallas_call(
        _merge_kernel,
        out_shape=jax.ShapeDtypeStruct((rows, width), BF16),
        grid=(rows // tm,),
        in_specs=[spec] * 3 + [lse_spec] * 3,
        out_specs=spec,
        compiler_params=_params(("parallel",)),
        name="merge_groups",
    )(*outs, *lses)


def _indexer_prep_kernel(lat_ref, gq_ref, lnw_ref, lnb_ref, cos_ref, sin_ref, iqn_ref, ik_ref, iw_ref):
    iqn_ref[...] = _rms(lat_ref[:, :IDX_Q_RANK], gq_ref[...]).astype(iqn_ref.dtype)
    k = lat_ref[:, IDX_Q_RANK:IDX_Q_RANK + IDX_DIM]
    kc = k - jnp.mean(k, axis=-1, keepdims=True)
    kn = kc * lax.rsqrt(jnp.mean(kc * kc, axis=-1, keepdims=True) + NORM_EPS) * lnw_ref[...] + lnb_ref[...]
    ik_ref[...] = _rope(kn, cos_ref[...], sin_ref[...]).astype(ik_ref.dtype)
    iw_ref[...] = lat_ref[:, IDX_Q_RANK + IDX_DIM:] * (IDX_HEADS ** -0.5 * IDX_DIM ** -0.5)


def indexer_prep(lat, gq, lnw, lnb, cos2, sin2):
    rows, width = lat.shape
    tm = _tile(rows, 512)
    row = lambda wd: pl.BlockSpec((tm, wd), lambda i: (i, 0))
    vec = lambda wd: pl.BlockSpec((1, wd), lambda i: (0, 0))
    return pl.pallas_call(
        _indexer_prep_kernel,
        out_shape=[jax.ShapeDtypeStruct((rows, IDX_Q_RANK), BF16),
                   jax.ShapeDtypeStruct((rows, IDX_DIM), BF16),
                   jax.ShapeDtypeStruct((rows, LANES), F32)],
        grid=(rows // tm,),
        in_specs=[row(width), vec(IDX_Q_RANK), vec(IDX_DIM), vec(IDX_DIM), row(HEAD_DIM), row(HEAD_DIM)],
        out_specs=[row(IDX_Q_RANK), row(IDX_DIM), row(LANES)],
        compiler_params=_params(("parallel",)),
        name="indexer_prep",
    )(lat, gq.reshape(1, -1), lnw.reshape(1, -1), lnb.reshape(1, -1), cos2, sin2)


def _ordered_bits(x):
    bits = lax.bitcast_convert_type(x, jnp.int32)
    key = jnp.where(bits >= 0, bits, bits ^ jnp.int32(0x7FFFFFFF))
    return jnp.where(key == -1, 0, key)


def _score_of(key):
    return lax.bitcast_convert_type(jnp.where(key >= 0, key, key ^ jnp.int32(0x7FFFFFFF)), F32)


IDX_TQ = 128
IDX_TS = 512
IDX_HEAD_BATCH = 8
IDX_SCORE_STEPS = 24


def _indexer_kernel(iq_ref, ik_ref, iw_ref, bias_ref, key_ref, qs_ref, wb_ref, *, n_sel):
    tq, seq = key_ref.shape
    i = pl.program_id(1)
    n_chunks = ((i + 1) * tq + IDX_TS - 1) // IDX_TS
    key_ref[...] = jnp.full((tq, seq), INT_MIN, jnp.int32)
    iw = iw_ref[...]
    for h in range(IDX_HEADS):
        qs_ref[h * tq:(h + 1) * tq, :] = iq_ref[:, h * IDX_DIM:(h + 1) * IDX_DIM]
        wb_ref[h] = jnp.broadcast_to(iw[:, h:h + 1], (tq, LANES))
    t_pos = i * tq + lax.broadcasted_iota(jnp.int32, (tq, IDX_TS), 0)
    lane = lax.broadcasted_iota(jnp.int32, (tq, IDX_TS), 1)
    hb_rows = IDX_HEAD_BATCH * tq

    def score_chunk(c, carry):
        start = pl.multiple_of(c * IDX_TS, IDX_TS)
        keys = ik_ref[pl.ds(start, IDX_TS), :]
        acc = jnp.zeros((tq, IDX_TS), F32)
        for hb in range(IDX_HEADS // IDX_HEAD_BATCH):
            rel = lax.dot_general(qs_ref[hb * hb_rows:(hb + 1) * hb_rows, :], keys, NT_DIMS,
                                  preferred_element_type=F32)
            for hh in range(IDX_HEAD_BATCH):
                wb = _lanes(wb_ref[hb * IDX_HEAD_BATCH + hh], IDX_TS)
                acc += jnp.maximum(rel[hh * tq:(hh + 1) * tq], 0.0) * wb
        causal = start + lane <= t_pos
        key_ref[:, pl.ds(start, IDX_TS)] = jnp.where(causal, _ordered_bits(acc), INT_MIN)
        hi_part = jnp.where(causal, acc, -jnp.inf)
        lo_part = jnp.where(causal, acc, jnp.inf)
        smax, smin = carry
        for b in range(IDX_TS // LANES):
            smax = jnp.maximum(smax, hi_part[:, b * LANES:(b + 1) * LANES])
            smin = jnp.minimum(smin, lo_part[:, b * LANES:(b + 1) * LANES])
        return smax, smin

    smax, smin = lax.fori_loop(0, n_chunks, score_chunk,
                               (jnp.full((tq, LANES), -jnp.inf, F32), jnp.full((tq, LANES), jnp.inf, F32)))
    smax = jnp.max(smax, axis=1, keepdims=True)
    smin = jnp.min(smin, axis=1, keepdims=True)

    def count_where(pred):
        def body(c, cnt):
            start = pl.multiple_of(c * IDX_TS, IDX_TS)
            hit = jnp.where(pred(key_ref[:, pl.ds(start, IDX_TS)], start + lane), 1.0, 0.0)
            for b in range(IDX_TS // LANES):
                cnt = cnt + hit[:, b * LANES:(b + 1) * LANES]
            return cnt

        cnt = lax.fori_loop(0, n_chunks, body, jnp.zeros((tq, LANES), F32))
        return jnp.sum(cnt, axis=1, keepdims=True)

    def count_ge(cand):
        cand_w = _lanes(cand, IDX_TS)
        return count_where(lambda keys, pos: keys >= cand_w)

    ones_i = jnp.ones((tq, LANES), jnp.int32)
    n_causal = i * tq + 1 + lax.broadcasted_iota(jnp.int32, (tq, LANES), 0)
    lo0 = _ordered_bits(smin) * ones_i
    hi0 = _ordered_bits(smax) * ones_i + 1
    open0 = jnp.where(n_causal > n_sel, 1.0, 0.0)
    thr0 = jnp.full((tq, LANES), INT_MIN + 1, jnp.int32)

    def search_cond(state):
        step, _, n_open = state
        return jnp.logical_and(step < IDX_SCORE_STEPS + KEY_BITS + 1, n_open > 0.0)

    def search_step(state):
        step, (lo, hi, n_lo, thr, n_thr, still_open), _ = state
        score_mid = _ordered_bits(0.5 * _score_of(lo) + 0.5 * _score_of(hi))
        key_mid = (lo >> 1) + (hi >> 1) + (lo & hi & 1)
        cand = jnp.where(step < IDX_SCORE_STEPS, score_mid, key_mid)
        cand = jnp.maximum(jnp.minimum(cand, hi - 1), lo + 1)
        cnt = count_ge(cand)
        enough = cnt >= n_sel
        lo = jnp.where(enough, cand, lo)
        n_lo = jnp.where(enough, cnt, n_lo)
        hi = jnp.where(enough, hi, cand)
        settled = jnp.logical_and(still_open > 0.0, jnp.logical_or(cnt == n_sel, hi <= lo + 1))
        thr = jnp.where(settled, lo, thr)
        n_thr = jnp.where(settled, n_lo, n_thr)
        still_open = jnp.where(settled, 0.0, still_open)
        return step + 1, (lo, hi, n_lo, thr, n_thr, still_open), jnp.max(still_open)

    n_causal_f = n_causal.astype(F32)
    init = (lo0, hi0, n_causal_f, thr0, jnp.minimum(n_causal_f, n_sel), open0)
    _, (_, _, _, thr, n_thr, _), _ = lax.while_loop(search_cond, search_step,
                                                    (jnp.int32(0), init, jnp.max(open0)))
    thr_w = _lanes(thr, IDX_TS)

    tied = n_thr > n_sel
    any_tied = jnp.max(jnp.where(tied, 1.0, 0.0)) > 0.0

    @pl.when(jnp.logical_not(any_tied))
    def _():
        for c in range(seq // IDX_TS):
            sl = slice(c * IDX_TS, (c + 1) * IDX_TS)
            bias_ref[:, sl] = jnp.where(key_ref[:, sl] >= thr_w, 0.0, MASK_VALUE).astype(bias_ref.dtype)

    @pl.when(any_tied)
    def _():
        need = n_sel - count_ge(thr + 1)

        def halve(_, bracket):
            below, upto = bracket
            mid = (below + upto) >> 1
            mid_w = _lanes(mid, IDX_TS)
            cnt = count_where(lambda keys, pos: jnp.logical_and(keys == thr_w, pos <= mid_w))
            ok = cnt >= need
            return jnp.where(ok, below, mid), jnp.where(ok, mid, upto)

        _, upto = lax.fori_loop(0, seq.bit_length(), halve,
                                (jnp.full((tq, LANES), -1, jnp.int32), jnp.full((tq, LANES), seq - 1, jnp.int32)))
        last_tie_w = _lanes(jnp.where(tied, upto, INT_MAX), IDX_TS)
        for c in range(seq // IDX_TS):
            sl = slice(c * IDX_TS, (c + 1) * IDX_TS)
            keys = key_ref[:, sl]
            at_thr = jnp.logical_and(keys == thr_w, c * IDX_TS + lane <= last_tie_w)
            bias_ref[:, sl] = jnp.where(jnp.logical_or(keys > thr_w, at_thr), 0.0, MASK_VALUE).astype(bias_ref.dtype)


def indexer_mask(iq, ik, iw, batch, seq):
    n_sel = min(TOPK_MAX, seq // 4)
    tq = IDX_TQ
    assert seq % IDX_TS == 0 and seq % tq == 0
    return pl.pallas_call(
        functools.partial(_indexer_kernel, n_sel=n_sel),
        out_shape=jax.ShapeDtypeStruct((batch, seq, seq), BF16),
        grid=(batch, seq // tq),
        in_specs=[pl.BlockSpec((None, tq, IDX_HEADS * IDX_DIM), lambda b, i: (b, i, 0)),
                  pl.BlockSpec((None, seq, IDX_DIM), lambda b, i: (b, 0, 0)),
                  pl.BlockSpec((None, tq, LANES), lambda b, i: (b, i, 0))],
        out_specs=pl.BlockSpec((None, tq, seq), lambda b, i: (b, i, 0)),
        scratch_shapes=[pltpu.VMEM((tq, seq), jnp.int32),
                        pltpu.VMEM((IDX_HEADS * tq, IDX_DIM), BF16),
                        pltpu.VMEM((IDX_HEADS, tq, LANES), F32)],
        compiler_params=_params(("parallel", "arbitrary")),
        name="indexer_mask",
    )(iq.reshape(batch, seq, -1), ik.reshape(batch, seq, -1), iw.reshape(batch, seq, -1))


B_TQ = 256
B_TS = 512
B_REP = B_HEADS // B_KV_HEADS
B_CHUNK = 128
B_UNIT = 512


def _sparse_attn_kernel(qblk_ref, kblk_ref, q_ref, k_ref, v_ref, bias_ref, o_ref,
                        qs_ref, vo_ref, biasf_ref, s_ref, p_ref, alpha_ref, m_ref, accl_ref):
    tq, ts = bias_ref.shape
    rows = B_REP * tq
    step = pl.program_id(1)
    i, j = qblk_ref[step], kblk_ref[step]
    last = ((i + 1) * tq - 1) // ts

    @pl.when(j == 0)
    def _():
        scale = HEAD_DIM ** -0.5 * LOG2E
        for g in range(B_KV_HEADS):
            for r in range(B_REP):
                h = g * B_REP + r
                qh = q_ref[:, h * HEAD_DIM:(h + 1) * HEAD_DIM].astype(F32) * scale
                qs_ref[g, r * tq:(r + 1) * tq, :] = qh.astype(qs_ref.dtype)
        m_ref[...] = jnp.full(m_ref.shape, MASK_VALUE, F32)
        accl_ref[...] = jnp.zeros(accl_ref.shape, F32)

    biasf_ref[...] = bias_ref[...].astype(F32)
    ones = jnp.ones((ts, LANES), BF16)
    for g in range(B_KV_HEADS):
        vo_ref[g, :, :HEAD_DIM] = v_ref[:, g * HEAD_DIM:(g + 1) * HEAD_DIM]
        vo_ref[g, :, HEAD_DIM:] = ones

    unit_rows = min(B_UNIT, rows)
    units = [(g, r0) for g in range(B_KV_HEADS) for r0 in range(0, rows, unit_rows)]

    def logits(u):
        g, r0 = units[u]
        kg = k_ref[:, g * HEAD_DIM:(g + 1) * HEAD_DIM]
        s_ref[u % 2] = lax.dot_general(qs_ref[g, r0:r0 + unit_rows, :], kg, NT_DIMS,
                                       preferred_element_type=F32)

    logits(0)
    for u, (g, r0) in enumerate(units):
        if u + 1 < len(units):
            logits(u + 1)
        par = u % 2
        for c in range(0, unit_rows, B_CHUNK):
            rs = slice(c, c + B_CHUNK)
            gs = slice(r0 + c, r0 + c + B_CHUNK)
            qrow = (r0 + c) % tq
            s = s_ref[par, rs, :] + biasf_ref[qrow:qrow + B_CHUNK, :]
            m_prev = m_ref[g, gs, :]
            m_new = jnp.maximum(m_prev, jnp.max(s, axis=1, keepdims=True))
            p_ref[par, rs, :] = jnp.exp2(s - _lanes(m_new, ts)).astype(BF16)
            alpha_ref[par, rs, :] = jnp.exp2(m_prev - m_new)
            m_ref[g, gs, :] = m_new
        pv = jnp.dot(p_ref[par], vo_ref[g], preferred_element_type=F32)
        alpha = alpha_ref[par]
        us = slice(r0, r0 + unit_rows)
        accl_ref[g, us, :] = jnp.concatenate([alpha, alpha], axis=1) * accl_ref[g, us, :] + pv

    @pl.when(j == last)
    def _():
        for g in range(B_KV_HEADS):
            out = accl_ref[g, :, :HEAD_DIM] / accl_ref[g, :, HEAD_DIM:]
            for r in range(B_REP):
                h = g * B_REP + r
                o_ref[:, h * HEAD_DIM:(h + 1) * HEAD_DIM] = out[r * tq:(r + 1) * tq].astype(o_ref.dtype)


def sparse_attention(qkv, bias, batch, seq):
    tq, ts = _tile(seq, B_TQ), _tile(seq, B_TS)
    view = qkv.reshape(batch, seq, B_QKV_WIDTH)
    k_tile = B_Q_WIDTH // B_KV_WIDTH
    pairs = [(i, j) for i in range(seq // tq) for j in range(((i + 1) * tq - 1) // ts + 1)]
    qblk = jnp.asarray([p[0] for p in pairs], jnp.int32)
    kblk = jnp.asarray([p[1] for p in pairs], jnp.int32)
    rows = B_REP * tq
    unit_rows = min(B_UNIT, rows)
    grid_spec = pltpu.PrefetchScalarGridSpec(
        num_scalar_prefetch=2,
        grid=(batch, len(pairs)),
        in_specs=[pl.BlockSpec((None, tq, B_Q_WIDTH), lambda b, s, qb, kb: (b, qb[s], 0)),
                  pl.BlockSpec((None, ts, B_KV_WIDTH), lambda b, s, qb, kb: (b, kb[s], k_tile)),
                  pl.BlockSpec((None, ts, B_KV_WIDTH), lambda b, s, qb, kb: (b, kb[s], k_tile + 1)),
                  pl.BlockSpec((None, tq, ts), lambda b, s, qb, kb: (b, qb[s], kb[s]))],
        out_specs=pl.BlockSpec((None, tq, B_Q_WIDTH), lambda b, s, qb, kb: (b, qb[s], 0)),
        scratch_shapes=[pltpu.VMEM((B_KV_HEADS, rows, HEAD_DIM), BF16),
                        pltpu.VMEM((B_KV_HEADS, ts, 2 * HEAD_DIM), BF16),
                        pltpu.VMEM((tq, ts), F32),
                        pltpu.VMEM((2, unit_rows, ts), F32),
                        pltpu.VMEM((2, unit_rows, ts), BF16),
                        pltpu.VMEM((2, unit_rows, LANES), F32),
                        pltpu.VMEM((B_KV_HEADS, rows, LANES), F32),
                        pltpu.VMEM((B_KV_HEADS, rows, 2 * HEAD_DIM), F32)])
    out = pl.pallas_call(
        _sparse_attn_kernel,
        out_shape=jax.ShapeDtypeStruct((batch, seq, B_Q_WIDTH), BF16),
        grid_spec=grid_spec,
        compiler_params=_params(("parallel", "arbitrary")),
        name="sparse_attention",
    )(qblk, kblk, view, view, view, bias)
    return out.reshape(batch * seq, B_Q_WIDTH)


def _rope_tables(positions):
    inv_freq = ROPE_THETA ** (-jnp.arange(0, HEAD_DIM, 2, dtype=F32) / HEAD_DIM)
    ang = positions.astype(F32).reshape(-1, 1) * inv_freq
    cos, sin = jnp.cos(ang), jnp.sin(ang)
    return jnp.concatenate([cos, cos], axis=-1), jnp.concatenate([-sin, sin], axis=-1)


def _mlp(f_in, w_up, w_down, layer):
    u = matmul(f_in, cast_weight(w_up, layer), act="relu2")
    return matmul(u, cast_weight(w_down, layer))


def _mixer_dilated(a, cos2, sin2, w_in, w_out, layer, batch, seq):
    tn = 1024
    per_part = A_GROUP_WIDTH // tn
    rope_pred = lambda j: (j // per_part) % 3 != 2
    a, cos2, sin2 = (_to_residue_major(t, batch, seq) for t in (a, cos2, sin2))
    qkv = matmul(a, cast_weight(w_in, layer), rope=(cos2, sin2, rope_pred), tn=tn)
    outs, lses = zip(*[dilated_attention(qkv, batch, seq, g) for g in range(N_GROUPS)])
    o = _from_residue_major(merge_groups(outs, lses), batch, seq)
    return matmul(o, cast_weight(w_out, layer))


def _mixer_sparse(a, cos2, sin2, w_in, idx_q_norm, w_idx_up, idx_k_w, idx_k_b, w_out, layer, batch, seq):
    tn = 1024
    n_rope = (B_Q_WIDTH + B_KV_WIDTH) // tn
    idx_width = IDX_Q_RANK + IDX_DIM + LANES
    w_all = cast_weight(w_in, layer, n_pad=B_QKV_WIDTH + idx_width)
    qkv = matmul(a, w_all, cols=(0, B_QKV_WIDTH), rope=(cos2, sin2, lambda j: j < n_rope), tn=tn)
    lat = matmul(a, w_all, cols=(B_QKV_WIDTH, idx_width), out_dtype=F32, tn=256)
    iq_n, ik, iw = indexer_prep(lat, idx_q_norm[layer], idx_k_w[layer], idx_k_b[layer], cos2, sin2)
    iq = matmul(iq_n, cast_weight(w_idx_up, layer), rope=(cos2, sin2, lambda j: j >= 0), tn=tn)
    bias = indexer_mask(iq, ik, iw, batch, seq)
    o = sparse_attention(qkv, bias, batch, seq)
    return matmul(o, cast_weight(w_out, layer))


def kernel(x, positions, attn_pre_norm, attn_post_norm, mlp_pre_norm, mlp_post_norm, w_in_a, w_out_a,
           w_in_b, idx_q_norm, w_idx_up, idx_k_norm_w, idx_k_norm_b, w_out_b, w_up, w_down):
    batch, seq, d_model = x.shape
    depth = attn_pre_norm.shape[0]
    cos2, sin2 = _rope_tables(positions)
    h = x.reshape(batch * seq, d_model)
    a = rms_norm_cast(h, attn_pre_norm[0])
    for i in range(depth):
        j = i // 2
        if i % 2 == 0:
            m = _mixer_dilated(a, cos2, sin2, w_in_a, w_out_a, j, batch, seq)
        else:
            m = _mixer_sparse(a, cos2, sin2, w_in_b, idx_q_norm, w_idx_up,
                              idx_k_norm_w, idx_k_norm_b, w_out_b, j, batch, seq)
        h, f_in = residual_norm(h, m, attn_post_norm[i], mlp_pre_norm[i])
        f = _mlp(f_in, w_up, w_down, i)
        h, a = residual_norm(h, f, mlp_post_norm[i], attn_pre_norm[i + 1] if i + 1 < depth else None)
    return h.reshape(batch, seq, d_model)
```

```python
import functools
import math

import jax
import jax.numpy as jnp
from jax import lax
from jax.experimental import pallas as pl
from jax.experimental.pallas import tpu as pltpu

HEAD_DIM = 128
ROPE_THETA = 10000.0
NORM_EPS = 1e-6
DILATED_GROUPS = ((128, 1), (512, 4), (2048, 16))
N_GROUPS = 3
A_HEADS_PER_GROUP = 16
A_BLOCK = 128
A_GROUP_WIDTH = A_HEADS_PER_GROUP * HEAD_DIM
B_HEADS = 32
B_KV_HEADS = 8
IDX_HEADS = 32
IDX_DIM = 128
IDX_Q_RANK = 1024
TOPK_MAX = 256
B_Q_WIDTH = B_HEADS * HEAD_DIM
B_KV_WIDTH = B_KV_HEADS * HEAD_DIM
B_QKV_WIDTH = B_Q_WIDTH + 2 * B_KV_WIDTH

LANES = 128
VMEM_LIMIT_BYTES = 56 * 2**20
MASK_VALUE = -1e30
INT_MIN = -2**31
INT_MAX = 2**31 - 1
KEY_BITS = 32
LOG2E = math.log2(math.e)
NT_DIMS = (((1,), (1,)), ((), ()))

F32 = jnp.float32
BF16 = jnp.bfloat16


def _tile(dim, pref):
    t = min(dim, pref)
    if dim % t:
        t = 1 << (t.bit_length() - 1)
    while dim % t:
        t //= 2
    return t


def _params(semantics):
    return pltpu.CompilerParams(dimension_semantics=semantics, vmem_limit_bytes=VMEM_LIMIT_BYTES)


def _lanes(x, width):
    return jnp.concatenate([x] * (width // LANES), axis=1)


NORM_ROWS = 256


def _rms(x, g):
    return x * lax.rsqrt(jnp.mean(x * x, axis=-1, keepdims=True) + NORM_EPS) * g


def _norm_kernel(x_ref, g_ref, a_ref):
    a_ref[...] = _rms(x_ref[...], g_ref[...]).astype(a_ref.dtype)


def rms_norm_cast(x, g):
    m, d = x.shape
    tm = _tile(m, NORM_ROWS)
    return pl.pallas_call(
        _norm_kernel,
        out_shape=jax.ShapeDtypeStruct((m, d), BF16),
        grid=(m // tm,),
        in_specs=[pl.BlockSpec((tm, d), lambda i: (i, 0)), pl.BlockSpec((1, d), lambda i: (0, 0))],
        out_specs=pl.BlockSpec((tm, d), lambda i: (i, 0)),
        compiler_params=_params(("parallel",)),
        name="rms_norm_cast",
    )(x, g.reshape(1, d))


def _residual_kernel(h_ref, m_ref, gpost_ref, *rest, with_next):
    h = h_ref[...] + _rms(m_ref[...].astype(F32), gpost_ref[...])
    if with_next:
        gpre_ref, h_out_ref, a_ref = rest
        a_ref[...] = _rms(h, gpre_ref[...]).astype(a_ref.dtype)
    else:
        (h_out_ref,) = rest
    h_out_ref[...] = h


def residual_norm(h, m, g_post, g_pre_next=None):
    rows, d = h.shape
    tm = _tile(rows, NORM_ROWS)
    with_next = g_pre_next is not None
    row_spec = pl.BlockSpec((tm, d), lambda i: (i, 0))
    g_spec = pl.BlockSpec((1, d), lambda i: (0, 0))
    args = [h, m, g_post.reshape(1, d)]
    in_specs = [row_spec, row_spec, g_spec]
    out_shape = [jax.ShapeDtypeStruct((rows, d), F32)]
    out_specs = [row_spec]
    if with_next:
        args.append(g_pre_next.reshape(1, d))
        in_specs.append(g_spec)
        out_shape.append(jax.ShapeDtypeStruct((rows, d), BF16))
        out_specs.append(row_spec)
    out = pl.pallas_call(
        functools.partial(_residual_kernel, with_next=with_next),
        out_shape=out_shape,
        grid=(rows // tm,),
        in_specs=in_specs,
        out_specs=out_specs,
        compiler_params=_params(("parallel",)),
        name="residual_norm",
    )(*args)
    return (out[0], out[1]) if with_next else (out[0], None)


CAST_BLOCK_BYTES = 8 * 2**20


def _cast_kernel(w_ref, o_ref):
    rows, n = w_ref.shape
    n_pad = o_ref.shape[1]
    n_full = n // LANES * LANES
    o_ref[:, :n_full] = w_ref[:, :n_full].astype(o_ref.dtype)
    if n_pad > n_full:
        tail = [w_ref[:, n_full:].astype(o_ref.dtype)] if n > n_full else []
        o_ref[:, n_full:] = jnp.concatenate(tail + [jnp.zeros((rows, n_pad - n), o_ref.dtype)], axis=1)


def cast_weight(w, layer, n_pad=None):
    _, kdim, n = w.shape
    n_pad = n if n_pad is None else n_pad
    tk = _tile(kdim, max(8, CAST_BLOCK_BYTES // (4 * n)))
    return pl.pallas_call(
        _cast_kernel,
        out_shape=jax.ShapeDtypeStruct((kdim, n_pad), BF16),
        grid=(kdim // tk,),
        in_specs=[pl.BlockSpec((None, tk, n), lambda i: (layer, i, 0))],
        out_specs=pl.BlockSpec((tk, n_pad), lambda i: (i, 0)),
        compiler_params=_params(("parallel",)),
        name="cast_weight",
    )(w)


MM_CHUNK = 256


def _rope(x, cos2, sin2):
    return x * cos2 + pltpu.roll(x, HEAD_DIM // 2, 1) * sin2


def _matmul_kernel(a_ref, w_ref, *rest, nk, act, rope_pred, n_split):
    if rope_pred is not None:
        cos_ref, sin_ref, *rest = rest
        do_rope = rope_pred(pl.program_id(1))
        cos2 = jnp.where(do_rope, cos_ref[...], 1.0)
        sin2 = jnp.where(do_rope, sin_ref[...], 0.0)
    o_ref, *scratch = rest
    sub = o_ref.shape[1] // n_split
    if nk > 1:
        (acc_ref,) = scratch

        @pl.when(pl.program_id(2) == 0)
        def _():
            acc_ref[...] = jnp.zeros(acc_ref.shape, F32)

    for c in range(n_split):
        acc = jnp.dot(a_ref[...], w_ref[:, c * sub:(c + 1) * sub], preferred_element_type=F32)
        if nk > 1:
            acc = acc_ref[c] + acc
            acc_ref[c] = acc
        if act == "relu2":
            r = jnp.maximum(acc, 0.0)
            acc = r * r
        if rope_pred is None:
            o_ref[:, c * sub:(c + 1) * sub] = acc.astype(o_ref.dtype)
        else:
            for hh in range(sub // HEAD_DIM):
                x = acc[:, hh * HEAD_DIM:(hh + 1) * HEAD_DIM]
                col = c * sub + hh * HEAD_DIM
                o_ref[:, col:col + HEAD_DIM] = _rope(x, cos2, sin2).astype(o_ref.dtype)


def matmul(a, w, *, cols=None, out_dtype=BF16, act=None, rope=None, tm=1024, tn=1024, tk=4096):
    m, kdim = a.shape
    col0, n = (0, w.shape[1]) if cols is None else cols
    tm, tn, tk = _tile(m, tm), _tile(n, tn), _tile(kdim, tk)
    assert col0 % tn == 0
    j0 = col0 // tn
    nk = kdim // tk
    in_specs = [pl.BlockSpec((tm, tk), lambda i, j, k: (i, k)),
                pl.BlockSpec((tk, tn), lambda i, j, k: (k, j + j0))]
    args = [a, w]
    rope_pred = None
    if rope is not None:
        cos2, sin2, rope_pred = rope
        tab_spec = pl.BlockSpec((tm, HEAD_DIM), lambda i, j, k: (i, 0))
        in_specs += [tab_spec, tab_spec]
        args += [cos2, sin2]
    n_split = max(1, tn // MM_CHUNK)
    scratch = [pltpu.VMEM((n_split, tm, tn // n_split), F32)] if nk > 1 else []
    return pl.pallas_call(
        functools.partial(_matmul_kernel, nk=nk, act=act, rope_pred=rope_pred, n_split=n_split),
        out_shape=jax.ShapeDtypeStruct((m, n), out_dtype),
        grid=(m // tm, n // tn, nk),
        in_specs=in_specs,
        out_specs=pl.BlockSpec((tm, tn), lambda i, j, k: (i, j)),
        scratch_shapes=scratch,
        compiler_params=_params(("parallel", "parallel", "arbitrary")),
        name="matmul",
    )(*args)


A_PERIOD = 16
A_ROWS = 2 * A_BLOCK


def _to_residue_major(x, batch, seq):
    u = seq // A_PERIOD
    return x.reshape(batch, u, A_PERIOD, -1).transpose(0, 2, 1, 3).reshape(batch * seq, -1)


def _from_residue_major(x, batch, seq):
    u = seq // A_PERIOD
    return x.reshape(batch, A_PERIOD, u, -1).transpose(0, 2, 1, 3).reshape(batch * seq, -1)


def _dilated_kernel(q_ref, kp_ref, kc_ref, vp_ref, vc_ref, o_ref, lse_ref, sp_ref, sc_ref,
                    *, w, rows, local_index, blk_axis):
    has_prev = pl.program_id(blk_axis) > 0
    qi = local_index(lax.broadcasted_iota(jnp.int32, (rows, rows), 0))
    kj = local_index(lax.broadcasted_iota(jnp.int32, (rows, rows), 1))
    mask_prev = jnp.logical_and(qi + rows - kj <= w, has_prev)
    mask_cur = jnp.logical_and(kj <= qi, qi - kj <= w)
    scale = HEAD_DIM ** -0.5 * LOG2E
    lead = (slice(None),) * (len(q_ref.shape) - 1)
    out_shape = o_ref.shape[:-1] + (HEAD_DIM,)

    def head(ref, h):
        return ref[lead + (slice(h * HEAD_DIM, (h + 1) * HEAD_DIM),)].reshape(rows, HEAD_DIM)

    def stage(h):
        q = head(q_ref, h)
        sp_ref[h % 2] = lax.dot_general(q, head(kp_ref, h), NT_DIMS, preferred_element_type=F32)
        sc_ref[h % 2] = lax.dot_general(q, head(kc_ref, h), NT_DIMS, preferred_element_type=F32)

    head_lane = lax.broadcasted_iota(jnp.int32, (rows, LANES), 1)
    lse_tile = jnp.zeros((rows, LANES), F32)

    stage(0)
    for h in range(A_HEADS_PER_GROUP):
        if h + 1 < A_HEADS_PER_GROUP:
            stage(h + 1)
        s_prev = jnp.where(mask_prev, sp_ref[h % 2] * scale, MASK_VALUE)
        s_cur = jnp.where(mask_cur, sc_ref[h % 2] * scale, MASK_VALUE)
        m = jnp.maximum(jnp.max(s_prev, axis=1, keepdims=True), jnp.max(s_cur, axis=1, keepdims=True))
        p_prev = jnp.exp2(s_prev - m)
        p_cur = jnp.exp2(s_cur - m)
        l = jnp.sum(p_prev, axis=1, keepdims=True) + jnp.sum(p_cur, axis=1, keepdims=True)
        o = jnp.dot(p_prev.astype(BF16), head(vp_ref, h), preferred_element_type=F32)
        o += jnp.dot(p_cur.astype(BF16), head(vc_ref, h), preferred_element_type=F32)
        idx = lead + (slice(h * HEAD_DIM, (h + 1) * HEAD_DIM),)
        o_ref[idx] = (o / l).astype(o_ref.dtype).reshape(out_shape)
        lse_tile = jnp.where(head_lane == h, m * (1.0 / LOG2E) + jnp.log(l), lse_tile)
    lse_ref[...] = lse_tile.reshape(lse_ref.shape)


def dilated_attention(qkv, batch, seq, group):
    window, r = DILATED_GROUPS[group]
    w = window // r
    u = seq // A_PERIOD
    width = qkv.shape[1]
    view = qkv.reshape(batch, 4, 4, u, width)
    rows = min(A_ROWS, seq // r)
    if r == 16:
        n_blk = u // rows
        block = (None, None, None, rows, A_GROUP_WIDTH)
        grid = (batch, 4, 4, n_blk)
        place = lambda g, blk: (g[0], g[1], g[2], blk)
        local_index = lambda rho: rho
    elif r == 4:
        n_blk = u // (rows // 4)
        block = (None, 4, None, rows // 4, A_GROUP_WIDTH)
        grid = (batch, 4, n_blk)
        place = lambda g, blk: (g[0], 0, g[1], blk)
        local_index = lambda rho: 4 * (rho % (rows // 4)) + rho // (rows // 4)
    else:
        assert r == 1
        per = rows // A_PERIOD
        n_blk = u // per
        block = (None, 4, 4, per, A_GROUP_WIDTH)
        grid = (batch, n_blk)
        place = lambda g, blk: (g[0], 0, 0, blk)
        local_index = lambda rho: A_PERIOD * (rho % per) + rho // per
    assert w <= rows and n_blk * rows * r == seq

    def spec(which, prev):
        def index(*g):
            blk = jnp.maximum(g[-1] - 1, 0) if prev else g[-1]
            return place(g, blk) + (group * 3 + which,)
        return pl.BlockSpec(block, index)

    out_spec = pl.BlockSpec(block, lambda *g: place(g, g[-1]) + (0,))
    lse_spec = pl.BlockSpec(block[:-1] + (LANES,), lambda *g: place(g, g[-1]) + (0,))
    o, lse = pl.pallas_call(
        functools.partial(_dilated_kernel, w=w, rows=rows, local_index=local_index, blk_axis=len(grid) - 1),
        out_shape=[jax.ShapeDtypeStruct((batch, 4, 4, u, A_GROUP_WIDTH), BF16),
                   jax.ShapeDtypeStruct((batch, 4, 4, u, LANES), F32)],
        grid=grid,
        in_specs=[spec(0, False), spec(1, True), spec(1, False), spec(2, True), spec(2, False)],
        out_specs=[out_spec, lse_spec],
        scratch_shapes=[pltpu.VMEM((2, rows, rows), F32),
                        pltpu.VMEM((2, rows, rows), F32)],
        compiler_params=_params(("parallel",) * (len(grid) - 1) + ("arbitrary",)),
        name="dilated_attention",
    )(view, view, view, view, view)
    return o.reshape(batch * seq, A_GROUP_WIDTH), lse.reshape(batch * seq, LANES)


def _merge_kernel(o0_ref, o1_ref, o2_ref, l0_ref, l1_ref, l2_ref, out_ref):
    l0, l1, l2 = l0_ref[...], l1_ref[...], l2_ref[...]
    m = jnp.maximum(jnp.maximum(l0, l1), l2)
    e0, e1, e2 = jnp.exp(l0 - m), jnp.exp(l1 - m), jnp.exp(l2 - m)
    inv = 1.0 / (e0 + e1 + e2)
    weights = (e0 * inv, e1 * inv, e2 * inv)
    rows = out_ref.shape[0]
    for h in range(A_HEADS_PER_GROUP):
        sl = slice(h * HEAD_DIM, (h + 1) * HEAD_DIM)
        acc = jnp.zeros((rows, HEAD_DIM), F32)
        for wgt, o_ref in zip(weights, (o0_ref, o1_ref, o2_ref)):
            acc += jnp.broadcast_to(wgt[:, h:h + 1], (rows, HEAD_DIM)) * o_ref[:, sl].astype(F32)
        out_ref[:, sl] = acc.astype(out_ref.dtype)


def merge_groups(outs, lses):
    rows, width = outs[0].shape
    tm = _tile(rows, 512)
    spec = pl.BlockSpec((tm, width), lambda i: (i, 0))
    lse_spec = pl.BlockSpec((tm, LANES), lambda i: (i, 0))
    return pl.pallas_call(
        _merge_kernel,
        out_shape=jax.ShapeDtypeStruct((rows, width), BF16),
        grid=(rows // tm,),
        in_specs=[spec] * 3 + [lse_spec] * 3,
        out_specs=spec,
        compiler_params=_params(("parallel",)),
        name="merge_groups",
    )(*outs, *lses)


def _indexer_prep_kernel(lat_ref, gq_ref, lnw_ref, lnb_ref, cos_ref, sin_ref, iqn_ref, ik_ref, iw_ref):
    iqn_ref[...] = _rms(lat_ref[:, :IDX_Q_RANK], gq_ref[...]).astype(iqn_ref.dtype)
    k = lat_ref[:, IDX_Q_RANK:IDX_Q_RANK + IDX_DIM]
    kc = k - jnp.mean(k, axis=-1, keepdims=True)
    kn = kc * lax.rsqrt(jnp.mean(kc * kc, axis=-1, keepdims=True) + NORM_EPS) * lnw_ref[...] + lnb_ref[...]
    ik_ref[...] = _rope(kn, cos_ref[...], sin_ref[...]).astype(ik_ref.dtype)
    iw_ref[...] = lat_ref[:, IDX_Q_RANK + IDX_DIM:] * (IDX_HEADS ** -0.5 * IDX_DIM ** -0.5)


def indexer_prep(lat, gq, lnw, lnb, cos2, sin2):
    rows, width = lat.shape
    tm = _tile(rows, 512)
    row = lambda wd: pl.BlockSpec((tm, wd), lambda i: (i, 0))
    vec = lambda wd: pl.BlockSpec((1, wd), lambda i: (0, 0))
    return pl.pallas_call(
        _indexer_prep_kernel,
        out_shape=[jax.ShapeDtypeStruct((rows, IDX_Q_RANK), BF16),
                   jax.ShapeDtypeStruct((rows, IDX_DIM), BF16),
                   jax.ShapeDtypeStruct((rows, LANES), F32)],
        grid=(rows // tm,),
        in_specs=[row(width), vec(IDX_Q_RANK), vec(IDX_DIM), vec(IDX_DIM), row(HEAD_DIM), row(HEAD_DIM)],
        out_specs=[row(IDX_Q_RANK), row(IDX_DIM), row(LANES)],
        compiler_params=_params(("parallel",)),
        name="indexer_prep",
    )(lat, gq.reshape(1, -1), lnw.reshape(1, -1), lnb.reshape(1, -1), cos2, sin2)


def _ordered_bits(x):
    bits = lax.bitcast_convert_type(x, jnp.int32)
    key = jnp.where(bits >= 0, bits, bits ^ jnp.int32(0x7FFFFFFF))
    return jnp.where(key == -1, 0, key)


def _score_of(key):
    return lax.bitcast_convert_type(jnp.where(key >= 0, key, key ^ jnp.int32(0x7FFFFFFF)), F32)


IDX_TQ = 128
IDX_TS = 512
IDX_HEAD_BATCH = 8
IDX_SCORE_STEPS = 24


def _indexer_kernel(iq_ref, ik_ref, iw_ref, bias_ref, key_ref, qs_ref, wb_ref, *, n_sel):
    tq, seq = key_ref.shape
    i = pl.program_id(1)
    n_chunks = ((i + 1) * tq + IDX_TS - 1) // IDX_TS
    key_ref[...] = jnp.full((tq, seq), INT_MIN, jnp.int32)
    iw = iw_ref[...]
    for h in range(IDX_HEADS):
        qs_ref[h * tq:(h + 1) * tq, :] = iq_ref[:, h * IDX_DIM:(h + 1) * IDX_DIM]
        wb_ref[h] = jnp.broadcast_to(iw[:, h:h + 1], (tq, LANES))
    t_pos = i * tq + lax.broadcasted_iota(jnp.int32, (tq, IDX_TS), 0)
    lane = lax.broadcasted_iota(jnp.int32, (tq, IDX_TS), 1)
    hb_rows = IDX_HEAD_BATCH * tq

    def score_chunk(c, carry):
        start = pl.multiple_of(c * IDX_TS, IDX_TS)
        keys = ik_ref[pl.ds(start, IDX_TS), :]
        acc = jnp.zeros((tq, IDX_TS), F32)
        for hb in range(IDX_HEADS // IDX_HEAD_BATCH):
            rel = lax.dot_general(qs_ref[hb * hb_rows:(hb + 1) * hb_rows, :], keys, NT_DIMS,
                                  preferred_element_type=F32)
            for hh in range(IDX_HEAD_BATCH):
                wb = _lanes(wb_ref[hb * IDX_HEAD_BATCH + hh], IDX_TS)
                acc += jnp.maximum(rel[hh * tq:(hh + 1) * tq], 0.0) * wb
        causal = start + lane <= t_pos
        key_ref[:, pl.ds(start, IDX_TS)] = jnp.where(causal, _ordered_bits(acc), INT_MIN)
        hi_part = jnp.where(causal, acc, -jnp.inf)
        lo_part = jnp.where(causal, acc, jnp.inf)
        smax, smin = carry
        for b in range(IDX_TS // LANES):
            smax = jnp.maximum(smax, hi_part[:, b * LANES:(b + 1) * LANES])
            smin = jnp.minimum(smin, lo_part[:, b * LANES:(b + 1) * LANES])
        return smax, smin

    smax, smin = lax.fori_loop(0, n_chunks, score_chunk,
                               (jnp.full((tq, LANES), -jnp.inf, F32), jnp.full((tq, LANES), jnp.inf, F32)))
    smax = jnp.max(smax, axis=1, keepdims=True)
    smin = jnp.min(smin, axis=1, keepdims=True)

    def count_where(pred):
        def body(c, cnt):
            start = pl.multiple_of(c * IDX_TS, IDX_TS)
            hit = jnp.where(pred(key_ref[:, pl.ds(start, IDX_TS)], start + lane), 1.0, 0.0)
            for b in range(IDX_TS // LANES):
                cnt = cnt + hit[:, b * LANES:(b + 1) * LANES]
            return cnt

        cnt = lax.fori_loop(0, n_chunks, body, jnp.zeros((tq, LANES), F32))
        return jnp.sum(cnt, axis=1, keepdims=True)

    def count_ge(cand):
        cand_w = _lanes(cand, IDX_TS)
        return count_where(lambda keys, pos: keys >= cand_w)

    ones_i = jnp.ones((tq, LANES), jnp.int32)
    n_causal = i * tq + 1 + lax.broadcasted_iota(jnp.int32, (tq, LANES), 0)
    lo0 = _ordered_bits(smin) * ones_i
    hi0 = _ordered_bits(smax) * ones_i + 1
    open0 = jnp.where(n_causal > n_sel, 1.0, 0.0)
    thr0 = jnp.full((tq, LANES), INT_MIN + 1, jnp.int32)

    def search_cond(state):
        step, _, n_open = state
        return jnp.logical_and(step < IDX_SCORE_STEPS + KEY_BITS + 1, n_open > 0.0)

    def search_step(state):
        step, (lo, hi, n_lo, thr, n_thr, still_open), _ = state
        score_mid = _ordered_bits(0.5 * _score_of(lo) + 0.5 * _score_of(hi))
        key_mid = (lo >> 1) + (hi >> 1) + (lo & hi & 1)
        cand = jnp.where(step < IDX_SCORE_STEPS, score_mid, key_mid)
        cand = jnp.maximum(jnp.minimum(cand, hi - 1), lo + 1)
        cnt = count_ge(cand)
        enough = cnt >= n_sel
        lo = jnp.where(enough, cand, lo)
        n_lo = jnp.where(enough, cnt, n_lo)
        hi = jnp.where(enough, hi, cand)
        settled = jnp.logical_and(still_open > 0.0, jnp.logical_or(cnt == n_sel, hi <= lo + 1))
        thr = jnp.where(settled, lo, thr)
        n_thr = jnp.where(settled, n_lo, n_thr)
        still_open = jnp.where(settled, 0.0, still_open)
        return step + 1, (lo, hi, n_lo, thr, n_thr, still_open), jnp.max(still_open)

    n_causal_f = n_causal.astype(F32)
    init = (lo0, hi0, n_causal_f, thr0, jnp.minimum(n_causal_f, n_sel), open0)
    _, (_, _, _, thr, n_thr, _), _ = lax.while_loop(search_cond, search_step,
                                                    (jnp.int32(0), init, jnp.max(open0)))
    thr_w = _lanes(thr, IDX_TS)

    tied = n_thr > n_sel
    any_tied = jnp.max(jnp.where(tied, 1.0, 0.0)) > 0.0

    @pl.when(jnp.logical_not(any_tied))
    def _():
        for c in range(seq // IDX_TS):
            sl = slice(c * IDX_TS, (c + 1) * IDX_TS)
            bias_ref[:, sl] = jnp.where(key_ref[:, sl] >= thr_w, 0.0, MASK_VALUE).astype(bias_ref.dtype)

    @pl.when(any_tied)
    def _():
        need = n_sel - count_ge(thr + 1)

        def halve(_, bracket):
            below, upto = bracket
            mid = (below + upto) >> 1
            mid_w = _lanes(mid, IDX_TS)
            cnt = count_where(lambda keys, pos: jnp.logical_and(keys == thr_w, pos <= mid_w))
            ok = cnt >= need
            return jnp.where(ok, below, mid), jnp.where(ok, mid, upto)

        _, upto = lax.fori_loop(0, seq.bit_length(), halve,
                                (jnp.full((tq, LANES), -1, jnp.int32), jnp.full((tq, LANES), seq - 1, jnp.int32)))
        last_tie_w = _lanes(jnp.where(tied, upto, INT_MAX), IDX_TS)
        for c in range(seq // IDX_TS):
            sl = slice(c * IDX_TS, (c + 1) * IDX_TS)
            keys = key_ref[:, sl]
            at_thr = jnp.logical_and(keys == thr_w, c * IDX_TS + lane <= last_tie_w)
            bias_ref[:, sl] = jnp.where(jnp.logical_or(keys > thr_w, at_thr), 0.0, MASK_VALUE).astype(bias_ref.dtype)


def indexer_mask(iq, ik, iw, batch, seq):
    n_sel = min(TOPK_MAX, seq // 4)
    tq = IDX_TQ
    assert seq % IDX_TS == 0 and seq % tq == 0
    return pl.pallas_call(
        functools.partial(_indexer_kernel, n_sel=n_sel),
        out_shape=jax.ShapeDtypeStruct((batch, seq, seq), BF16),
        grid=(batch, seq // tq),
        in_specs=[pl.BlockSpec((None, tq, IDX_HEADS * IDX_DIM), lambda b, i: (b, i, 0)),
                  pl.BlockSpec((None, seq, IDX_DIM), lambda b, i: (b, 0, 0)),
                  pl.BlockSpec((None, tq, LANES), lambda b, i: (b, i, 0))],
        out_specs=pl.BlockSpec((None, tq, seq), lambda b, i: (b, i, 0)),
        scratch_shapes=[pltpu.VMEM((tq, seq), jnp.int32),
                        pltpu.VMEM((IDX_HEADS * tq, IDX_DIM), BF16),
                        pltpu.VMEM((IDX_HEADS, tq, LANES), F32)],
        compiler_params=_params(("parallel", "arbitrary")),
        name="indexer_mask",
    )(iq.reshape(batch, seq, -1), ik.reshape(batch, seq, -1), iw.reshape(batch, seq, -1))


B_TQ = 256
B_TS = 512
B_REP = B_HEADS // B_KV_HEADS
B_CHUNK = 128
B_UNIT = 512


def _sparse_attn_kernel(qblk_ref, kblk_ref, q_ref, k_ref, v_ref, bias_ref, o_ref,
                        qs_ref, vo_ref, biasf_ref, s_ref, p_ref, alpha_ref, m_ref, accl_ref):
    tq, ts = bias_ref.shape
    rows = B_REP * tq
    step = pl.program_id(1)
    i, j = qblk_ref[step], kblk_ref[step]
    last = ((i + 1) * tq - 1) // ts

    @pl.when(j == 0)
    def _():
        scale = HEAD_DIM ** -0.5 * LOG2E
        for g in range(B_KV_HEADS):
            for r in range(B_REP):
                h = g * B_REP + r
                qh = q_ref[:, h * HEAD_DIM:(h + 1) * HEAD_DIM].astype(F32) * scale
                qs_ref[g, r * tq:(r + 1) * tq, :] = qh.astype(qs_ref.dtype)
        m_ref[...] = jnp.full(m_ref.shape, MASK_VALUE, F32)
        accl_ref[...] = jnp.zeros(accl_ref.shape, F32)

    biasf_ref[...] = bias_ref[...].astype(F32)
    ones = jnp.ones((ts, LANES), BF16)
    for g in range(B_KV_HEADS):
        vo_ref[g, :, :HEAD_DIM] = v_ref[:, g * HEAD_DIM:(g + 1) * HEAD_DIM]
        vo_ref[g, :, HEAD_DIM:] = ones

    unit_rows = min(B_UNIT, rows)
    units = [(g, r0) for g in range(B_KV_HEADS) for r0 in range(0, rows, unit_rows)]

    def logits(u):
        g, r0 = units[u]
        kg = k_ref[:, g * HEAD_DIM:(g + 1) * HEAD_DIM]
        s_ref[u % 2] = lax.dot_general(qs_ref[g, r0:r0 + unit_rows, :], kg, NT_DIMS,
                                       preferred_element_type=F32)

    logits(0)
    for u, (g, r0) in enumerate(units):
        if u + 1 < len(units):
            logits(u + 1)
        par = u % 2
        for c in range(0, unit_rows, B_CHUNK):
            rs = slice(c, c + B_CHUNK)
            gs = slice(r0 + c, r0 + c + B_CHUNK)
            qrow = (r0 + c) % tq
            s = s_ref[par, rs, :] + biasf_ref[qrow:qrow + B_CHUNK, :]
            m_prev = m_ref[g, gs, :]
            m_new = jnp.maximum(m_prev, jnp.max(s, axis=1, keepdims=True))
            p_ref[par, rs, :] = jnp.exp2(s - _lanes(m_new, ts)).astype(BF16)
            alpha_ref[par, rs, :] = jnp.exp2(m_prev - m_new)
            m_ref[g, gs, :] = m_new
        pv = jnp.dot(p_ref[par], vo_ref[g], preferred_element_type=F32)
        alpha = alpha_ref[par]
        us = slice(r0, r0 + unit_rows)
        accl_ref[g, us, :] = jnp.concatenate([alpha, alpha], axis=1) * accl_ref[g, us, :] + pv

    @pl.when(j == last)
    def _():
        for g in range(B_KV_HEADS):
            out = accl_ref[g, :, :HEAD_DIM] / accl_ref[g, :, HEAD_DIM:]
            for r in range(B_REP):
                h = g * B_REP + r
                o_ref[:, h * HEAD_DIM:(h + 1) * HEAD_DIM] = out[r * tq:(r + 1) * tq].astype(o_ref.dtype)


def sparse_attention(qkv, bias, batch, seq):
    tq, ts = _tile(seq, B_TQ), _tile(seq, B_TS)
    view = qkv.reshape(batch, seq, B_QKV_WIDTH)
    k_tile = B_Q_WIDTH // B_KV_WIDTH
    pairs = [(i, j) for i in range(seq // tq) for j in range(((i + 1) * tq - 1) // ts + 1)]
    qblk = jnp.asarray([p[0] for p in pairs], jnp.int32)
    kblk = jnp.asarray([p[1] for p in pairs], jnp.int32)
    rows = B_REP * tq
    unit_rows = min(B_UNIT, rows)
    grid_spec = pltpu.PrefetchScalarGridSpec(
        num_scalar_prefetch=2,
        grid=(batch, len(pairs)),
        in_specs=[pl.BlockSpec((None, tq, B_Q_WIDTH), lambda b, s, qb, kb: (b, qb[s], 0)),
                  pl.BlockSpec((None, ts, B_KV_WIDTH), lambda b, s, qb, kb: (b, kb[s], k_tile)),
                  pl.BlockSpec((None, ts, B_KV_WIDTH), lambda b, s, qb, kb: (b, kb[s], k_tile + 1)),
                  pl.BlockSpec((None, tq, ts), lambda b, s, qb, kb: (b, qb[s], kb[s]))],
        out_specs=pl.BlockSpec((None, tq, B_Q_WIDTH), lambda b, s, qb, kb: (b, qb[s], 0)),
        scratch_shapes=[pltpu.VMEM((B_KV_HEADS, rows, HEAD_DIM), BF16),
                        pltpu.VMEM((B_KV_HEADS, ts, 2 * HEAD_DIM), BF16),
                        pltpu.VMEM((tq, ts), F32),
                        pltpu.VMEM((2, unit_rows, ts), F32),
                        pltpu.VMEM((2, unit_rows, ts), BF16),
                        pltpu.VMEM((2, unit_rows, LANES), F32),
                        pltpu.VMEM((B_KV_HEADS, rows, LANES), F32),
                        pltpu.VMEM((B_KV_HEADS, rows, 2 * HEAD_DIM), F32)])
    out = pl.pallas_call(
        _sparse_attn_kernel,
        out_shape=jax.ShapeDtypeStruct((batch, seq, B_Q_WIDTH), BF16),
        grid_spec=grid_spec,
        compiler_params=_params(("parallel", "arbitrary")),
        name="sparse_attention",
    )(qblk, kblk, view, view, view, bias)
    return out.reshape(batch * seq, B_Q_WIDTH)


def _rope_tables(positions):
    inv_freq = ROPE_THETA ** (-jnp.arange(0, HEAD_DIM, 2, dtype=F32) / HEAD_DIM)
    ang = positions.astype(F32).reshape(-1, 1) * inv_freq
    cos, sin = jnp.cos(ang), jnp.sin(ang)
    return jnp.concatenate([cos, cos], axis=-1), jnp.concatenate([-sin, sin], axis=-1)


def _mlp(f_in, w_up, w_down, layer):
    u = matmul(f_in, cast_weight(w_up, layer), act="relu2")
    return matmul(u, cast_weight(w_down, layer))


def _mixer_dilated(a, cos2, sin2, w_in, w_out, layer, batch, seq):
    tn = 1024
    per_part = A_GROUP_WIDTH // tn
    rope_pred = lambda j: (j // per_part) % 3 != 2
    a, cos2, sin2 = (_to_residue_major(t, batch, seq) for t in (a, cos2, sin2))
    qkv = matmul(a, cast_weight(w_in, layer), rope=(cos2, sin2, rope_pred), tn=tn)
    outs, lses = zip(*[dilated_attention(qkv, batch, seq, g) for g in range(N_GROUPS)])
    o = _from_residue_major(merge_groups(outs, lses), batch, seq)
    return matmul(o, cast_weight(w_out, layer))


def _mixer_sparse(a, cos2, sin2, w_in, idx_q_norm, w_idx_up, idx_k_w, idx_k_b, w_out, layer, batch, seq):
    tn = 1024
    n_rope = (B_Q_WIDTH + B_KV_WIDTH) // tn
    idx_width = IDX_Q_RANK + IDX_DIM + LANES
    w_all = cast_weight(w_in, layer, n_pad=B_QKV_WIDTH + idx_width)
    qkv = matmul(a, w_all, cols=(0, B_QKV_WIDTH), rope=(cos2, sin2, lambda j: j < n_rope), tn=tn)
    lat = matmul(a, w_all, cols=(B_QKV_WIDTH, idx_width), out_dtype=F32, tn=256)
    iq_n, ik, iw = indexer_prep(lat, idx_q_norm[layer], idx_k_w[layer], idx_k_b[layer], cos2, sin2)
    iq = matmul(iq_n, cast_weight(w_idx_up, layer), rope=(cos2, sin2, lambda j: j >= 0), tn=tn)
    bias = indexer_mask(iq, ik, iw, batch, seq)
    o = sparse_attention(qkv, bias, batch, seq)
    return matmul(o, cast_weight(w_out, layer))


def kernel(x, positions, attn_pre_norm, attn_post_norm, mlp_pre_norm, mlp_post_norm, w_in_a, w_out_a,
           w_in_b, idx_q_norm, w_idx_up, idx_k_norm_w, idx_k_norm_b, w_out_b, w_up, w_down):
    batch, seq, d_model = x.shape
    depth = attn_pre_norm.shape[0]
    cos2, sin2 = _rope_tables(positions)
    h = x.reshape(batch * seq, d_model)
    a = rms_norm_cast(h, attn_pre_norm[0])
    for i in range(depth):
        j = i // 2
        if i % 2 == 0:
            m = _mixer_dilated(a, cos2, sin2, w_in_a, w_out_a, j, batch, seq)
        else:
            m = _mixer_sparse(a, cos2, sin2, w_in_b, idx_q_norm, w_idx_up,
                              idx_k_norm_w, idx_k_norm_b, w_out_b, j, batch, seq)
        h, f_in = residual_norm(h, m, attn_post_norm[i], mlp_pre_norm[i])
        f = _mlp(f_in, w_up, w_down, i)
        h, a = residual_norm(h, f, mlp_post_norm[i], attn_pre_norm[i + 1] if i + 1 < depth else None)
    return h.reshape(batch, seq, d_model)
```

```python
import functools
import math

import jax
import jax.numpy as jnp
from jax import lax
from jax.experimental import pallas as pl
from jax.experimental.pallas import tpu as pltpu

HEAD_DIM = 128
ROPE_THETA = 10000.0
NORM_EPS = 1e-6
DILATED_GROUPS = ((128, 1), (512, 4), (2048, 16))
N_GROUPS = 3
A_HEADS_PER_GROUP = 16
A_BLOCK = 128
A_GROUP_WIDTH = A_HEADS_PER_GROUP * HEAD_DIM
B_HEADS = 32
B_KV_HEADS = 8
IDX_HEADS = 32
IDX_DIM = 128
IDX_Q_RANK = 1024
TOPK_MAX = 256
B_Q_WIDTH = B_HEADS * HEAD_DIM
B_KV_WIDTH = B_KV_HEADS * HEAD_DIM
B_QKV_WIDTH = B_Q_WIDTH + 2 * B_KV_WIDTH

LANES = 128
VMEM_LIMIT_BYTES = 56 * 2**20
MASK_VALUE = -1e30
INT_MIN = -2**31
INT_MAX = 2**31 - 1
KEY_BITS = 32
LOG2E = math.log2(math.e)
NT_DIMS = (((1,), (1,)), ((), ()))

F32 = jnp.float32
BF16 = jnp.bfloat16


def _tile(dim, pref):
    t = min(dim, pref)
    if dim % t:
        t = 1 << (t.bit_length() - 1)
    while dim % t:
        t //= 2
    return t


def _params(semantics):
    return pltpu.CompilerParams(dimension_semantics=semantics, vmem_limit_bytes=VMEM_LIMIT_BYTES)


def _lanes(x, width):
    return jnp.concatenate([x] * (width // LANES), axis=1)


NORM_ROWS = 256


def _rms(x, g):
    return x * lax.rsqrt(jnp.mean(x * x, axis=-1, keepdims=True) + NORM_EPS) * g


def _norm_kernel(x_ref, g_ref, a_ref):
    a_ref[...] = _rms(x_ref[...], g_ref[...]).astype(a_ref.dtype)


def rms_norm_cast(x, g):
    m, d = x.shape
    tm = _tile(m, NORM_ROWS)
    return pl.pallas_call(
        _norm_kernel,
        out_shape=jax.ShapeDtypeStruct((m, d), BF16),
        grid=(m // tm,),
        in_specs=[pl.BlockSpec((tm, d), lambda i: (i, 0)), pl.BlockSpec((1, d), lambda i: (0, 0))],
        out_specs=pl.BlockSpec((tm, d), lambda i: (i, 0)),
        compiler_params=_params(("parallel",)),
        name="rms_norm_cast",
    )(x, g.reshape(1, d))


def _residual_kernel(h_ref, m_ref, gpost_ref, *rest, with_next):
    h = h_ref[...] + _rms(m_ref[...].astype(F32), gpost_ref[...])
    if with_next:
        gpre_ref, h_out_ref, a_ref = rest
        a_ref[...] = _rms(h, gpre_ref[...]).astype(a_ref.dtype)
    else:
        (h_out_ref,) = rest
    h_out_ref[...] = h


def residual_norm(h, m, g_post, g_pre_next=None):
    rows, d = h.shape
    tm = _tile(rows, NORM_ROWS)
    with_next = g_pre_next is not None
    row_spec = pl.BlockSpec((tm, d), lambda i: (i, 0))
    g_spec = pl.BlockSpec((1, d), lambda i: (0, 0))
    args = [h, m, g_post.reshape(1, d)]
    in_specs = [row_spec, row_spec, g_spec]
    out_shape = [jax.ShapeDtypeStruct((rows, d), F32)]
    out_specs = [row_spec]
    if with_next:
        args.append(g_pre_next.reshape(1, d))
        in_specs.append(g_spec)
        out_shape.append(jax.ShapeDtypeStruct((rows, d), BF16))
        out_specs.append(row_spec)
    out = pl.pallas_call(
        functools.partial(_residual_kernel, with_next=with_next),
        out_shape=out_shape,
        grid=(rows // tm,),
        in_specs=in_specs,
        out_specs=out_specs,
        compiler_params=_params(("parallel",)),
        name="residual_norm",
    )(*args)
    return (out[0], out[1]) if with_next else (out[0], None)


CAST_BLOCK_BYTES = 8 * 2**20


def _cast_kernel(w_ref, o_ref):
    rows, n = w_ref.shape
    n_pad = o_ref.shape[1]
    n_full = n // LANES * LANES
    o_ref[:, :n_full] = w_ref[:, :n_full].astype(o_ref.dtype)
    if n_pad > n_full:
        tail = [w_ref[:, n_full:].astype(o_ref.dtype)] if n > n_full else []
        o_ref[:, n_full:] = jnp.concatenate(tail + [jnp.zeros((rows, n_pad - n), o_ref.dtype)], axis=1)


def cast_weight(w, layer, n_pad=None):
    _, kdim, n = w.shape
    n_pad = n if n_pad is None else n_pad
    tk = _tile(kdim, max(8, CAST_BLOCK_BYTES // (4 * n)))
    return pl.pallas_call(
        _cast_kernel,
        out_shape=jax.ShapeDtypeStruct((kdim, n_pad), BF16),
        grid=(kdim // tk,),
        in_specs=[pl.BlockSpec((None, tk, n), lambda i: (layer, i, 0))],
        out_specs=pl.BlockSpec((tk, n_pad), lambda i: (i, 0)),
        compiler_params=_params(("parallel",)),
        name="cast_weight",
    )(w)


MM_CHUNK = 256


def _rope(x, cos2, sin2):
    return x * cos2 + pltpu.roll(x, HEAD_DIM // 2, 1) * sin2


def _matmul_kernel(a_ref, w_ref, *rest, nk, act, rope_pred, n_split):
    if rope_pred is not None:
        cos_ref, sin_ref, *rest = rest
        do_rope = rope_pred(pl.program_id(1))
        cos2 = jnp.where(do_rope, cos_ref[...], 1.0)
        sin2 = jnp.where(do_rope, sin_ref[...], 0.0)
    o_ref, *scratch = rest
    sub = o_ref.shape[1] // n_split
    if nk > 1:
        (acc_ref,) = scratch

        @pl.when(pl.program_id(2) == 0)
        def _():
            acc_ref[...] = jnp.zeros(acc_ref.shape, F32)

    for c in range(n_split):
        acc = jnp.dot(a_ref[...], w_ref[:, c * sub:(c + 1) * sub], preferred_element_type=F32)
        if nk > 1:
            acc = acc_ref[c] + acc
            acc_ref[c] = acc
        if act == "relu2":
            r = jnp.maximum(acc, 0.0)
            acc = r * r
        if rope_pred is None:
            o_ref[:, c * sub:(c + 1) * sub] = acc.astype(o_ref.dtype)
        else:
            for hh in range(sub // HEAD_DIM):
                x = acc[:, hh * HEAD_DIM:(hh + 1) * HEAD_DIM]
                col = c * sub + hh * HEAD_DIM
                o_ref[:, col:col + HEAD_DIM] = _rope(x, cos2, sin2).astype(o_ref.dtype)


def matmul(a, w, *, cols=None, out_dtype=BF16, act=None, rope=None, tm=1024, tn=1024, tk=4096):
    m, kdim = a.shape
    col0, n = (0, w.shape[1]) if cols is None else cols
    tm, tn, tk = _tile(m, tm), _tile(n, tn), _tile(kdim, tk)
    assert col0 % tn == 0
    j0 = col0 // tn
    nk = kdim // tk
    in_specs = [pl.BlockSpec((tm, tk), lambda i, j, k: (i, k)),
                pl.BlockSpec((tk, tn), lambda i, j, k: (k, j + j0))]
    args = [a, w]
    rope_pred = None
    if rope is not None:
        cos2, sin2, rope_pred = rope
        tab_spec = pl.BlockSpec((tm, HEAD_DIM), lambda i, j, k: (i, 0))
        in_specs += [tab_spec, tab_spec]
        args += [cos2, sin2]
    n_split = max(1, tn // MM_CHUNK)
    scratch = [pltpu.VMEM((n_split, tm, tn // n_split), F32)] if nk > 1 else []
    return pl.pallas_call(
        functools.partial(_matmul_kernel, nk=nk, act=act, rope_pred=rope_pred, n_split=n_split),
        out_shape=jax.ShapeDtypeStruct((m, n), out_dtype),
        grid=(m // tm, n // tn, nk),
        in_specs=in_specs,
        out_specs=pl.BlockSpec((tm, tn), lambda i, j, k: (i, j)),
        scratch_shapes=scratch,
        compiler_params=_params(("parallel", "parallel", "arbitrary")),
        name="matmul",
    )(*args)


A_PERIOD = 16
A_ROWS = 2 * A_BLOCK


def _to_residue_major(x, batch, seq):
    u = seq // A_PERIOD
    return x.reshape(batch, u, A_PERIOD, -1).transpose(0, 2, 1, 3).reshape(batch * seq, -1)


def _from_residue_major(x, batch, seq):
    u = seq // A_PERIOD
    return x.reshape(batch, A_PERIOD, u, -1).transpose(0, 2, 1, 3).reshape(batch * seq, -1)


def _dilated_kernel(q_ref, kp_ref, kc_ref, vp_ref, vc_ref, o_ref, lse_ref, sp_ref, sc_ref,
                    *, w, rows, local_index, blk_axis):
    has_prev = pl.program_id(blk_axis) > 0
    qi = local_index(lax.broadcasted_iota(jnp.int32, (rows, rows), 0))
    kj = local_index(lax.broadcasted_iota(jnp.int32, (rows, rows), 1))
    mask_prev = jnp.logical_and(qi + rows - kj <= w, has_prev)
    mask_cur = jnp.logical_and(kj <= qi, qi - kj <= w)
    scale = HEAD_DIM ** -0.5 * LOG2E
    lead = (slice(None),) * (len(q_ref.shape) - 1)
    out_shape = o_ref.shape[:-1] + (HEAD_DIM,)

    def head(ref, h):
        return ref[lead + (slice(h * HEAD_DIM, (h + 1) * HEAD_DIM),)].reshape(rows, HEAD_DIM)

    def stage(h):
        q = head(q_ref, h)
        sp_ref[h % 2] = lax.dot_general(q, head(kp_ref, h), NT_DIMS, preferred_element_type=F32)
        sc_ref[h % 2] = lax.dot_general(q, head(kc_ref, h), NT_DIMS, preferred_element_type=F32)

    head_lane = lax.broadcasted_iota(jnp.int32, (rows, LANES), 1)
    lse_tile = jnp.zeros((rows, LANES), F32)

    stage(0)
    for h in range(A_HEADS_PER_GROUP):
        if h + 1 < A_HEADS_PER_GROUP:
            stage(h + 1)
        s_prev = jnp.where(mask_prev, sp_ref[h % 2] * scale, MASK_VALUE)
        s_cur = jnp.where(mask_cur, sc_ref[h % 2] * scale, MASK_VALUE)
        m = jnp.maximum(jnp.max(s_prev, axis=1, keepdims=True), jnp.max(s_cur, axis=1, keepdims=True))
        p_prev = jnp.exp2(s_prev - m)
        p_cur = jnp.exp2(s_cur - m)
        l = jnp.sum(p_prev, axis=1, keepdims=True) + jnp.sum(p_cur, axis=1, keepdims=True)
        o = jnp.dot(p_prev.astype(BF16), head(vp_ref, h), preferred_element_type=F32)
        o += jnp.dot(p_cur.astype(BF16), head(vc_ref, h), preferred_element_type=F32)
        idx = lead + (slice(h * HEAD_DIM, (h + 1) * HEAD_DIM),)
        o_ref[idx] = (o / l).astype(o_ref.dtype).reshape(out_shape)
        lse_tile = jnp.where(head_lane == h, m * (1.0 / LOG2E) + jnp.log(l), lse_tile)
    lse_ref[...] = lse_tile.reshape(lse_ref.shape)


def dilated_attention(qkv, batch, seq, group):
    window, r = DILATED_GROUPS[group]
    w = window // r
    u = seq // A_PERIOD
    width = qkv.shape[1]
    view = qkv.reshape(batch, 4, 4, u, width)
    rows = min(A_ROWS, seq // r)
    if r == 16:
        n_blk = u // rows
        block = (None, None, None, rows, A_GROUP_WIDTH)
        grid = (batch, 4, 4, n_blk)
        place = lambda g, blk: (g[0], g[1], g[2], blk)
        local_index = lambda rho: rho
    elif r == 4:
        n_blk = u // (rows // 4)
        block = (None, 4, None, rows // 4, A_GROUP_WIDTH)
        grid = (batch, 4, n_blk)
        place = lambda g, blk: (g[0], 0, g[1], blk)
        local_index = lambda rho: 4 * (rho % (rows // 4)) + rho // (rows // 4)
    else:
        assert r == 1
        per = rows // A_PERIOD
        n_blk = u // per
        block = (None, 4, 4, per, A_GROUP_WIDTH)
        grid = (batch, n_blk)
        place = lambda g, blk: (g[0], 0, 0, blk)
        local_index = lambda rho: A_PERIOD * (rho % per) + rho // per
    assert w <= rows and n_blk * rows * r == seq

    def spec(which, prev):
        def index(*g):
            blk = jnp.maximum(g[-1] - 1, 0) if prev else g[-1]
            return place(g, blk) + (group * 3 + which,)
        return pl.BlockSpec(block, index)

    out_spec = pl.BlockSpec(block, lambda *g: place(g, g[-1]) + (0,))
    lse_spec = pl.BlockSpec(block[:-1] + (LANES,), lambda *g: place(g, g[-1]) + (0,))
    o, lse = pl.pallas_call(
        functools.partial(_dilated_kernel, w=w, rows=rows, local_index=local_index, blk_axis=len(grid) - 1),
        out_shape=[jax.ShapeDtypeStruct((batch, 4, 4, u, A_GROUP_WIDTH), BF16),
                   jax.ShapeDtypeStruct((batch, 4, 4, u, LANES), F32)],
        grid=grid,
        in_specs=[spec(0, False), spec(1, True), spec(1, False), spec(2, True), spec(2, False)],
        out_specs=[out_spec, lse_spec],
        scratch_shapes=[pltpu.VMEM((2, rows, rows), F32),
                        pltpu.VMEM((2, rows, rows), F32)],
        compiler_params=_params(("parallel",) * (len(grid) - 1) + ("arbitrary",)),
        name="dilated_attention",
    )(view, view, view, view, view)
    return o.reshape(batch * seq, A_GROUP_WIDTH), lse.reshape(batch * seq, LANES)


def _merge_kernel(o0_ref, o1_ref, o2_ref, l0_ref, l1_ref, l2_ref, out_ref):
    l0, l1, l2 = l0_ref[...], l1_ref[...], l2_ref[...]
    m = jnp.maximum(jnp.maximum(l0, l1), l2)
    e0, e1, e2 = jnp.exp(l0 - m), jnp.exp(l1 - m), jnp.exp(l2 - m)
    inv = 1.0 / (e0 + e1 + e2)
    weights = (e0 * inv, e1 * inv, e2 * inv)
    rows = out_ref.shape[0]
    for h in range(A_HEADS_PER_GROUP):
        sl = slice(h * HEAD_DIM, (h + 1) * HEAD_DIM)
        acc = jnp.zeros((rows, HEAD_DIM), F32)
        for wgt, o_ref in zip(weights, (o0_ref, o1_ref, o2_ref)):
            acc += jnp.broadcast_to(wgt[:, h:h + 1], (rows, HEAD_DIM)) * o_ref[:, sl].astype(F32)
        out_ref[:, sl] = acc.astype(out_ref.dtype)


def merge_groups(outs, lses):
    rows, width = outs[0].shape
    tm = _tile(rows, 512)
    spec = pl.BlockSpec((tm, width), lambda i: (i, 0))
    lse_spec = pl.BlockSpec((tm, LANES), lambda i: (i, 0))
    return pl.pallas_call(
        _merge_kernel,
        out_shape=jax.ShapeDtypeStruct((rows, width), BF16),
        grid=(rows // tm,),
        in_specs=[spec] * 3 + [lse_spec] * 3,
        out_specs=spec,
        compiler_params=_params(("parallel",)),
        name="merge_groups",
    )(*outs, *lses)


def _indexer_prep_kernel(lat_ref, gq_ref, lnw_ref, lnb_ref, cos_ref, sin_ref, iqn_ref, ik_ref, iw_ref):
    iqn_ref[...] = _rms(lat_ref[:, :IDX_Q_RANK], gq_ref[...]).astype(iqn_ref.dtype)
    k = lat_ref[:, IDX_Q_RANK:IDX_Q_RANK + IDX_DIM]
    kc = k - jnp.mean(k, axis=-1, keepdims=True)
    kn = kc * lax.rsqrt(jnp.mean(kc * kc, axis=-1, keepdims=True) + NORM_EPS) * lnw_ref[...] + lnb_ref[...]
    ik_ref[...] = _rope(kn, cos_ref[...], sin_ref[...]).astype(ik_ref.dtype)
    iw_ref[...] = lat_ref[:, IDX_Q_RANK + IDX_DIM:] * (IDX_HEADS ** -0.5 * IDX_DIM ** -0.5)


def indexer_prep(lat, gq, lnw, lnb, cos2, sin2):
    rows, width = lat.shape
    tm = _tile(rows, 512)
    row = lambda wd: pl.BlockSpec((tm, wd), lambda i: (i, 0))
    vec = lambda wd: pl.BlockSpec((1, wd), lambda i: (0, 0))
    return pl.pallas_call(
        _indexer_prep_kernel,
        out_shape=[jax.ShapeDtypeStruct((rows, IDX_Q_RANK), BF16),
                   jax.ShapeDtypeStruct((rows, IDX_DIM), BF16),
                   jax.ShapeDtypeStruct((rows, LANES), F32)],
        grid=(rows // tm,),
        in_specs=[row(width), vec(IDX_Q_RANK), vec(IDX_DIM), vec(IDX_DIM), row(HEAD_DIM), row(HEAD_DIM)],
        out_specs=[row(IDX_Q_RANK), row(IDX_DIM), row(LANES)],
        compiler_params=_params(("parallel",)),
        name="indexer_prep",
    )(lat, gq.reshape(1, -1), lnw.reshape(1, -1), lnb.reshape(1, -1), cos2, sin2)


def _ordered_bits(x):
    bits = lax.bitcast_convert_type(x, jnp.int32)
    key = jnp.where(bits >= 0, bits, bits ^ jnp.int32(0x7FFFFFFF))
    return jnp.where(key == -1, 0, key)


def _score_of(key):
    return lax.bitcast_convert_type(jnp.where(key >= 0, key, key ^ jnp.int32(0x7FFFFFFF)), F32)


IDX_TQ = 128
IDX_TS = 512
IDX_HEAD_BATCH = 8
IDX_SCORE_STEPS = 24


def _indexer_kernel(iq_ref, ik_ref, iw_ref, bias_ref, key_ref, qs_ref, wb_ref, *, n_sel):
    tq, seq = key_ref.shape
    i = pl.program_id(1)
    n_chunks = ((i + 1) * tq + IDX_TS - 1) // IDX_TS
    key_ref[...] = jnp.full((tq, seq), INT_MIN, jnp.int32)
    iw = iw_ref[...]
    for h in range(IDX_HEADS):
        qs_ref[h * tq:(h + 1) * tq, :] = iq_ref[:, h * IDX_DIM:(h + 1) * IDX_DIM]
        wb_ref[h] = jnp.broadcast_to(iw[:, h:h + 1], (tq, LANES))
    t_pos = i * tq + lax.broadcasted_iota(jnp.int32, (tq, IDX_TS), 0)
    lane = lax.broadcasted_iota(jnp.int32, (tq, IDX_TS), 1)
    hb_rows = IDX_HEAD_BATCH * tq

    def score_chunk(c, carry):
        start = pl.multiple_of(c * IDX_TS, IDX_TS)
        keys_t = ik_ref[:, pl.ds(start, IDX_TS)]
        acc = jnp.zeros((tq, IDX_TS), F32)
        for hb in range(IDX_HEADS // IDX_HEAD_BATCH):
            rel = jnp.dot(qs_ref[hb * hb_rows:(hb + 1) * hb_rows, :], keys_t, preferred_element_type=F32)
            for hh in range(IDX_HEAD_BATCH):
                wb = _lanes(wb_ref[hb * IDX_HEAD_BATCH + hh], IDX_TS)
                acc += jnp.maximum(rel[hh * tq:(hh + 1) * tq], 0.0) * wb
        causal = start + lane <= t_pos
        key_ref[:, pl.ds(start, IDX_TS)] = jnp.where(causal, _ordered_bits(acc), INT_MIN)
        hi_part = jnp.where(causal, acc, -jnp.inf)
        lo_part = jnp.where(causal, acc, jnp.inf)
        smax, smin = carry
        for b in range(IDX_TS // LANES):
            smax = jnp.maximum(smax, hi_part[:, b * LANES:(b + 1) * LANES])
            smin = jnp.minimum(smin, lo_part[:, b * LANES:(b + 1) * LANES])
        return smax, smin

    smax, smin = lax.fori_loop(0, n_chunks, score_chunk,
                               (jnp.full((tq, LANES), -jnp.inf, F32), jnp.full((tq, LANES), jnp.inf, F32)))
    smax = jnp.max(smax, axis=1, keepdims=True)
    smin = jnp.min(smin, axis=1, keepdims=True)

    def count_where(pred):
        def body(c, cnt):
            start = pl.multiple_of(c * IDX_TS, IDX_TS)
            hit = jnp.where(pred(key_ref[:, pl.ds(start, IDX_TS)], start + lane), 1.0, 0.0)
            for b in range(IDX_TS // LANES):
                cnt = cnt + hit[:, b * LANES:(b + 1) * LANES]
            return cnt

        cnt = lax.fori_loop(0, n_chunks, body, jnp.zeros((tq, LANES), F32))
        return jnp.sum(cnt, axis=1, keepdims=True)

    def count_ge(cand):
        cand_w = _lanes(cand, IDX_TS)
        return count_where(lambda keys, pos: keys >= cand_w)

    ones_i = jnp.ones((tq, LANES), jnp.int32)
    n_causal = i * tq + 1 + lax.broadcasted_iota(jnp.int32, (tq, LANES), 0)
    lo0 = _ordered_bits(smin) * ones_i
    hi0 = _ordered_bits(smax) * ones_i + 1
    open0 = jnp.where(n_causal > n_sel, 1.0, 0.0)
    thr0 = jnp.full((tq, LANES), INT_MIN + 1, jnp.int32)

    def search_cond(state):
        step, _, n_open = state
        return jnp.logical_and(step < IDX_SCORE_STEPS + KEY_BITS + 1, n_open > 0.0)

    def search_step(state):
        step, (lo, hi, n_lo, thr, n_thr, still_open), _ = state
        score_mid = _ordered_bits(0.5 * _score_of(lo) + 0.5 * _score_of(hi))
        key_mid = (lo >> 1) + (hi >> 1) + (lo & hi & 1)
        cand = jnp.where(step < IDX_SCORE_STEPS, score_mid, key_mid)
        cand = jnp.maximum(jnp.minimum(cand, hi - 1), lo + 1)
        cnt = count_ge(cand)
        enough = cnt >= n_sel
        lo = jnp.where(enough, cand, lo)
        n_lo = jnp.where(enough, cnt, n_lo)
        hi = jnp.where(enough, hi, cand)
        settled = jnp.logical_and(still_open > 0.0, jnp.logical_or(cnt == n_sel, hi <= lo + 1))
        thr = jnp.where(settled, lo, thr)
        n_thr = jnp.where(settled, n_lo, n_thr)
        still_open = jnp.where(settled, 0.0, still_open)
        return step + 1, (lo, hi, n_lo, thr, n_thr, still_open), jnp.max(still_open)

    n_causal_f = n_causal.astype(F32)
    init = (lo0, hi0, n_causal_f, thr0, jnp.minimum(n_causal_f, n_sel), open0)
    _, (_, _, _, thr, n_thr, _), _ = lax.while_loop(search_cond, search_step,
                                                    (jnp.int32(0), init, jnp.max(open0)))
    thr_w = _lanes(thr, IDX_TS)

    tied = n_thr > n_sel
    any_tied = jnp.max(jnp.where(tied, 1.0, 0.0)) > 0.0

    @pl.when(jnp.logical_not(any_tied))
    def _():
        for c in range(seq // IDX_TS):
            sl = slice(c * IDX_TS, (c + 1) * IDX_TS)
            bias_ref[:, sl] = jnp.where(key_ref[:, sl] >= thr_w, 0.0, MASK_VALUE).astype(bias_ref.dtype)

    @pl.when(any_tied)
    def _():
        need = n_sel - count_ge(thr + 1)

        def halve(_, bracket):
            below, upto = bracket
            mid = (below + upto) >> 1
            mid_w = _lanes(mid, IDX_TS)
            cnt = count_where(lambda keys, pos: jnp.logical_and(keys == thr_w, pos <= mid_w))
            ok = cnt >= need
            return jnp.where(ok, below, mid), jnp.where(ok, mid, upto)

        _, upto = lax.fori_loop(0, seq.bit_length(), halve,
                                (jnp.full((tq, LANES), -1, jnp.int32), jnp.full((tq, LANES), seq - 1, jnp.int32)))
        last_tie_w = _lanes(jnp.where(tied, upto, INT_MAX), IDX_TS)
        for c in range(seq // IDX_TS):
            sl = slice(c * IDX_TS, (c + 1) * IDX_TS)
            keys = key_ref[:, sl]
            at_thr = jnp.logical_and(keys == thr_w, c * IDX_TS + lane <= last_tie_w)
            bias_ref[:, sl] = jnp.where(jnp.logical_or(keys > thr_w, at_thr), 0.0, MASK_VALUE).astype(bias_ref.dtype)


def indexer_mask(iq, ik, iw, batch, seq):
    n_sel = min(TOPK_MAX, seq // 4)
    tq = IDX_TQ
    assert seq % IDX_TS == 0 and seq % tq == 0
    return pl.pallas_call(
        functools.partial(_indexer_kernel, n_sel=n_sel),
        out_shape=jax.ShapeDtypeStruct((batch, seq, seq), BF16),
        grid=(batch, seq // tq),
        in_specs=[pl.BlockSpec((None, tq, IDX_HEADS * IDX_DIM), lambda b, i: (b, i, 0)),
                  pl.BlockSpec((None, IDX_DIM, seq), lambda b, i: (b, 0, 0)),
                  pl.BlockSpec((None, tq, LANES), lambda b, i: (b, i, 0))],
        out_specs=pl.BlockSpec((None, tq, seq), lambda b, i: (b, i, 0)),
        scratch_shapes=[pltpu.VMEM((tq, seq), jnp.int32),
                        pltpu.VMEM((IDX_HEADS * tq, IDX_DIM), BF16),
                        pltpu.VMEM((IDX_HEADS, tq, LANES), F32)],
        compiler_params=_params(("parallel", "arbitrary")),
        name="indexer_mask",
    )(iq.reshape(batch, seq, -1), ik.reshape(batch, seq, -1).transpose(0, 2, 1), iw.reshape(batch, seq, -1))


B_TQ = 256
B_TS = 512
B_REP = B_HEADS // B_KV_HEADS
B_CHUNK = 128
B_UNIT = 512


def _sparse_attn_kernel(qblk_ref, kblk_ref, q_ref, k_ref, v_ref, bias_ref, o_ref,
                        qs_ref, vo_ref, biasf_ref, s_ref, p_ref, alpha_ref, m_ref, accl_ref):
    tq, ts = bias_ref.shape
    rows = B_REP * tq
    step = pl.program_id(1)
    i, j = qblk_ref[step], kblk_ref[step]
    last = ((i + 1) * tq - 1) // ts

    @pl.when(j == 0)
    def _():
        scale = HEAD_DIM ** -0.5 * LOG2E
        for g in range(B_KV_HEADS):
            for r in range(B_REP):
                h = g * B_REP + r
                qh = q_ref[:, h * HEAD_DIM:(h + 1) * HEAD_DIM].astype(F32) * scale
                qs_ref[g, r * tq:(r + 1) * tq, :] = qh.astype(qs_ref.dtype)
        m_ref[...] = jnp.full(m_ref.shape, MASK_VALUE, F32)
        accl_ref[...] = jnp.zeros(accl_ref.shape, F32)

    biasf_ref[...] = bias_ref[...].astype(F32)
    ones = jnp.ones((ts, LANES), BF16)
    for g in range(B_KV_HEADS):
        vo_ref[g, :, :HEAD_DIM] = v_ref[:, g * HEAD_DIM:(g + 1) * HEAD_DIM]
        vo_ref[g, :, HEAD_DIM:] = ones

    unit_rows = min(B_UNIT, rows)
    units = [(g, r0) for g in range(B_KV_HEADS) for r0 in range(0, rows, unit_rows)]

    def logits(u):
        g, r0 = units[u]
        kg_t = k_ref[g * HEAD_DIM:(g + 1) * HEAD_DIM, :]
        s_ref[u % 2] = jnp.dot(qs_ref[g, r0:r0 + unit_rows, :], kg_t, preferred_element_type=F32)

    logits(0)
    for u, (g, r0) in enumerate(units):
        if u + 1 < len(units):
            logits(u + 1)
        par = u % 2
        for c in range(0, unit_rows, B_CHUNK):
            rs = slice(c, c + B_CHUNK)
            gs = slice(r0 + c, r0 + c + B_CHUNK)
            qrow = (r0 + c) % tq
            s = s_ref[par, rs, :] + biasf_ref[qrow:qrow + B_CHUNK, :]
            m_prev = m_ref[g, gs, :]
            m_new = jnp.maximum(m_prev, jnp.max(s, axis=1, keepdims=True))
            p_ref[par, rs, :] = jnp.exp2(s - _lanes(m_new, ts)).astype(BF16)
            alpha_ref[par, rs, :] = jnp.exp2(m_prev - m_new)
            m_ref[g, gs, :] = m_new
        pv = jnp.dot(p_ref[par], vo_ref[g], preferred_element_type=F32)
        alpha = alpha_ref[par]
        us = slice(r0, r0 + unit_rows)
        accl_ref[g, us, :] = jnp.concatenate([alpha, alpha], axis=1) * accl_ref[g, us, :] + pv

    @pl.when(j == last)
    def _():
        for g in range(B_KV_HEADS):
            out = accl_ref[g, :, :HEAD_DIM] / accl_ref[g, :, HEAD_DIM:]
            for r in range(B_REP):
                h = g * B_REP + r
                o_ref[:, h * HEAD_DIM:(h + 1) * HEAD_DIM] = out[r * tq:(r + 1) * tq].astype(o_ref.dtype)


def sparse_attention(qkv, bias, batch, seq):
    tq, ts = _tile(seq, B_TQ), _tile(seq, B_TS)
    view = qkv.reshape(batch, seq, B_QKV_WIDTH)
    k_tile = B_Q_WIDTH // B_KV_WIDTH
    keys_t = view[:, :, B_Q_WIDTH:B_Q_WIDTH + B_KV_WIDTH].transpose(0, 2, 1)
    pairs =[(i, j) for i in range(seq // tq) for j in range(((i + 1) * tq - 1) // ts + 1)]
    qblk = jnp.asarray([p[0] for p in pairs], jnp.int32)
    kblk = jnp.asarray([p[1] for p in pairs], jnp.int32)
    rows = B_REP * tq
    unit_rows = min(B_UNIT, rows)
    grid_spec = pltpu.PrefetchScalarGridSpec(
        num_scalar_prefetch=2,
        grid=(batch, len(pairs)),
        in_specs=[pl.BlockSpec((None, tq, B_Q_WIDTH), lambda b, s, qb, kb: (b, qb[s], 0)),
                  pl.BlockSpec((None, B_KV_WIDTH, ts), lambda b, s, qb, kb: (b, 0, kb[s])),
                  pl.BlockSpec((None, ts, B_KV_WIDTH), lambda b, s, qb, kb: (b, kb[s], k_tile + 1)),
                  pl.BlockSpec((None, tq, ts), lambda b, s, qb, kb: (b, qb[s], kb[s]))],
        out_specs=pl.BlockSpec((None, tq, B_Q_WIDTH), lambda b, s, qb, kb: (b, qb[s], 0)),
        scratch_shapes=[pltpu.VMEM((B_KV_HEADS, rows, HEAD_DIM), BF16),
                        pltpu.VMEM((B_KV_HEADS, ts, 2 * HEAD_DIM), BF16),
                        pltpu.VMEM((tq, ts), F32),
                        pltpu.VMEM((2, unit_rows, ts), F32),
                        pltpu.VMEM((2, unit_rows, ts), BF16),
                        pltpu.VMEM((2, unit_rows, LANES), F32),
                        pltpu.VMEM((B_KV_HEADS, rows, LANES), F32),
                        pltpu.VMEM((B_KV_HEADS, rows, 2 * HEAD_DIM), F32)])
    out = pl.pallas_call(
        _sparse_attn_kernel,
        out_shape=jax.ShapeDtypeStruct((batch, seq, B_Q_WIDTH), BF16),
        grid_spec=grid_spec,
        compiler_params=_params(("parallel", "arbitrary")),
        name="sparse_attention",
    )(qblk, kblk, view, keys_t, view, bias)
    return out.reshape(batch * seq, B_Q_WIDTH)


def _rope_tables(positions):
    inv_freq = ROPE_THETA ** (-jnp.arange(0, HEAD_DIM, 2, dtype=F32) / HEAD_DIM)
    ang = positions.astype(F32).reshape(-1, 1) * inv_freq
    cos, sin = jnp.cos(ang), jnp.sin(ang)
    return jnp.concatenate([cos, cos], axis=-1), jnp.concatenate([-sin, sin], axis=-1)


def _mlp(f_in, w_up, w_down, layer):
    u = matmul(f_in, cast_weight(w_up, layer), act="relu2")
    return matmul(u, cast_weight(w_down, layer))


def _mixer_dilated(a, cos2, sin2, w_in, w_out, layer, batch, seq):
    tn = 1024
    per_part = A_GROUP_WIDTH // tn
    rope_pred = lambda j: (j // per_part) % 3 != 2
    a, cos2, sin2 = (_to_residue_major(t, batch, seq) for t in (a, cos2, sin2))
    qkv = matmul(a, cast_weight(w_in, layer), rope=(cos2, sin2, rope_pred), tn=tn)
    outs, lses = zip(*[dilated_attention(qkv, batch, seq, g) for g in range(N_GROUPS)])
    o = _from_residue_major(merge_groups(outs, lses), batch, seq)
    return matmul(o, cast_weight(w_out, layer))


def _mixer_sparse(a, cos2, sin2, w_in, idx_q_norm, w_idx_up, idx_k_w, idx_k_b, w_out, layer, batch, seq):
    tn = 1024
    n_rope = (B_Q_WIDTH + B_KV_WIDTH) // tn
    idx_width = IDX_Q_RANK + IDX_DIM + LANES
    w_all = cast_weight(w_in, layer, n_pad=B_QKV_WIDTH + idx_width)
    qkv = matmul(a, w_all, cols=(0, B_QKV_WIDTH), rope=(cos2, sin2, lambda j: j < n_rope), tn=tn)
    lat = matmul(a, w_all, cols=(B_QKV_WIDTH, idx_width), out_dtype=F32, tn=256)
    iq_n, ik, iw = indexer_prep(lat, idx_q_norm[layer], idx_k_w[layer], idx_k_b[layer], cos2, sin2)
    iq = matmul(iq_n, cast_weight(w_idx_up, layer), rope=(cos2, sin2, lambda j: j >= 0), tn=tn)
    bias = indexer_mask(iq, ik, iw, batch, seq)
    o = sparse_attention(qkv, bias, batch, seq)
    return matmul(o, cast_weight(w_out, layer))


def kernel(x, positions, attn_pre_norm, attn_post_norm, mlp_pre_norm, mlp_post_norm, w_in_a, w_out_a,
           w_in_b, idx_q_norm, w_idx_up, idx_k_norm_w, idx_k_norm_b, w_out_b, w_up, w_down):
    batch, seq, d_model = x.shape
    depth = attn_pre_norm.shape[0]
    cos2, sin2 = _rope_tables(positions)
    h = x.reshape(batch * seq, d_model)
    a = rms_norm_cast(h, attn_pre_norm[0])
    for i in range(depth):
        j = i // 2
        if i % 2 == 0:
            m = _mixer_dilated(a, cos2, sin2, w_in_a, w_out_a, j, batch, seq)
        else:
            m = _mixer_sparse(a, cos2, sin2, w_in_b, idx_q_norm, w_idx_up,
                              idx_k_norm_w, idx_k_norm_b, w_out_b, j, batch, seq)
        h, f_in = residual_norm(h, m, attn_post_norm[i], mlp_pre_norm[i])
        f = _mlp(f_in, w_up, w_down, i)
        h, a = residual_norm(h, f, mlp_post_norm[i], attn_pre_norm[i + 1] if i + 1 < depth else None)
    return h.reshape(batch, seq, d_model)
```
